```python
import jax, jax.numpy as jnp
from jax import lax
import numpy as np

D_MODEL = 1024
BATCH = 8
SEQ = 2048
DEPTH = 2
DEC_BATCH = 128
DEC_SEQ = 8
PAST_LEN = 16384
PAGE_SIZE = 128

W_A = D_MODEL // 2
W_B = D_MODEL // 2
W_C = D_MODEL // 2
W_D = D_MODEL // 2
K_A = 3
K_B = 31
POOL_WINDOWS = (2, 4, 8, 16)
N_POOL_GROUPS = len(POOL_WINDOWS)
POOL_GROUP = W_C // N_POOL_GROUPS
POOL_HIST = max(POOL_WINDOWS) - 1
CHUNK = 128
N_SGU_GROUPS = 4
SGU_GROUP = W_D // N_SGU_GROUPS
D_FF = 4 * D_MODEL
N_EVEN = (DEPTH + 1) // 2
N_ODD = DEPTH // 2
EPS = 1e-6

kernel_name = "hybrid_conv_pool_sgu_decoder_step"


def rms_norm(x, g):
    xf = x.astype(jnp.float32)
    y = xf * lax.rsqrt(jnp.mean(xf * xf, axis=-1, keepdims=True) + EPS)
    return (y * g.astype(jnp.float32)).astype(x.dtype)


def layer_norm(x, g, b):
    xf = x.astype(jnp.float32)
    mu = jnp.mean(xf, axis=-1, keepdims=True)
    xc = xf - mu
    y = xc * lax.rsqrt(jnp.mean(xc * xc, axis=-1, keepdims=True) + EPS)
    return (y * g.astype(jnp.float32) + b.astype(jnp.float32)).astype(x.dtype)


def causal_dwconv(xp, w):
    return lax.conv_general_dilated(
        xp, w[:, None, :], window_strides=(1,), padding='VALID',
        dimension_numbers=('NWC', 'WIO', 'NWC'), feature_group_count=xp.shape[-1])


def even_mixer(h, hist_a, hist_b, w_in, conv_a, conv_b, conv_b_bias, ln_g, ln_b, w_out):
    z = h @ w_in
    gate_b, gate_c, xa, ga, gb = jnp.split(
        z, [W_A, 2 * W_A, 3 * W_A, 3 * W_A + W_B], axis=-1)
    ua_p = jnp.concatenate([hist_a, gate_c * xa], axis=1)
    ya = gate_b * causal_dwconv(ua_p, conv_a)
    ub_p = jnp.concatenate([hist_b, ga * jax.nn.sigmoid(gb)], axis=1)
    cb = causal_dwconv(ub_p, conv_b) + conv_b_bias
    yb = jax.nn.silu(layer_norm(cb, ln_g, ln_b))
    y = jnp.concatenate([ya, yb], axis=-1) @ w_out
    return y, ua_p[:, -(K_A - 1):], ub_p[:, -(K_B - 1):]


def pool_mix(p, start_pos, w_pool, pool_scale):
    n, L, _ = p.shape
    T = L - POOL_HIST
    pf = p.astype(jnp.float32)
    cs = jnp.concatenate([jnp.zeros((n, 1, W_C), jnp.float32), jnp.cumsum(pf, axis=1)], axis=1)
    pos = jnp.arange(T, dtype=jnp.int32) + start_pos
    means = []
    for g, w in enumerate(POOL_WINDOWS):
        csg = cs[..., g * POOL_GROUP:(g + 1) * POOL_GROUP]
        wsum = csg[:, POOL_HIST + 1:] - csg[:, POOL_HIST + 1 - w:POOL_HIST + 1 - w + T]
        cnt = jnp.minimum(pos + 1, w).astype(jnp.float32)
        means.append(wsum / cnt[None, :, None])
    mean = jnp.stack(means, axis=2)
    tok = pf[:, POOL_HIST:].reshape(n, T, N_POOL_GROUPS, POOL_GROUP)
    d = (mean - tok).astype(p.dtype)
    y = jnp.einsum('btgc,gcd->btgd', d, w_pool).reshape(n, T, W_C)
    return y * pool_scale


def sgu_mix(u, v, w_s, b_s, ln_g, ln_b):
    n, T, _ = v.shape
    nc = -(-T // CHUNK)
    Tp = nc * CHUNK
    vn = layer_norm(v, ln_g, ln_b)
    vp = jnp.pad(vn, ((0, 0), (0, Tp - T), (0, 0))).reshape(n, nc, CHUNK, N_SGU_GROUPS, SGU_GROUP)
    mask = jnp.tril(jnp.ones((CHUNK, CHUNK), dtype=bool))
    wm = jnp.where(mask[None], w_s, 0)
    mixed = jnp.einsum('gts,bcsgd->bctgd', wm, vp) + b_s.T[None, None, :, :, None]
    mixed = mixed.reshape(n, Tp, W_D)[:, :T]
    return u * mixed, vn


def odd_mixer(h, hist_c, start_pos, w_in, w_pool, pool_scale, sgu_w, sgu_b, sgu_ln_g, sgu_ln_b, w_out):
    z = h @ w_in
    pc, u, v = jnp.split(z, [W_C, W_C + W_D], axis=-1)
    pc_p = jnp.concatenate([hist_c, pc], axis=1)
    yc = pool_mix(pc_p, start_pos, w_pool, pool_scale)
    yd, vn = sgu_mix(u, v, sgu_w, sgu_b, sgu_ln_g, sgu_ln_b)
    y = jnp.concatenate([yc, yd], axis=-1) @ w_out
    return y, pc_p[:, -POOL_HIST:], vn


def setup_inputs(seed: int = 0) -> dict:
    key = jax.random.key(seed)
    ks = iter(jax.random.split(key, 32))

    def nrm(shape, scale):
        return jax.random.normal(next(ks), shape, jnp.float32) * scale

    return {
        "x_prompt": nrm((BATCH, SEQ, D_MODEL), 1.0),
        "x_sample": nrm((DEC_BATCH, DEC_SEQ, D_MODEL), 1.0),
        "state_conv_a": nrm((N_EVEN, DEC_BATCH, K_A - 1, W_A), 1.0),
        "state_conv_b": nrm((N_EVEN, DEC_BATCH, K_B - 1, W_B), 0.5),
        "state_pool": nrm((N_ODD, DEC_BATCH, POOL_HIST, W_C), 1.0),
        "norm_mix": 1.0 + nrm((DEPTH, D_MODEL), 0.02),
        "norm_ffn": 1.0 + nrm((DEPTH, D_MODEL), 0.02),
        "ev_w_in": nrm((N_EVEN, D_MODEL, 3 * W_A + 2 * W_B), D_MODEL ** -0.5),
        "ev_conv_a": nrm((N_EVEN, K_A, W_A), K_A ** -0.5),
        "ev_conv_b": nrm((N_EVEN, K_B, W_B), K_B ** -0.5),
        "ev_conv_b_bias": nrm((N_EVEN, W_B), 0.02),
        "ev_ln_g": 1.0 + nrm((N_EVEN, W_B), 0.02),
        "ev_ln_b": nrm((N_EVEN, W_B), 0.02),
        "ev_w_out": nrm((N_EVEN, W_A + W_B, D_MODEL), (W_A + W_B) ** -0.5),
        "od_w_in": nrm((N_ODD, D_MODEL, W_C + 2 * W_D), D_MODEL ** -0.5),
        "od_pool_w": nrm((N_ODD, N_POOL_GROUPS, POOL_GROUP, POOL_GROUP), POOL_GROUP ** -0.5),
        "od_pool_scale": 1.0 + nrm((N_ODD, W_C), 0.1),
        "od_sgu_w": nrm((N_ODD, N_SGU_GROUPS, CHUNK, CHUNK), CHUNK ** -0.5),
        "od_sgu_b": 1.0 + nrm((N_ODD, N_SGU_GROUPS, CHUNK), 0.02),
        "od_sgu_ln_g": 1.0 + nrm((N_ODD, W_D), 0.02),
        "od_sgu_ln_b": nrm((N_ODD, W_D), 0.02),
        "od_w_out": nrm((N_ODD, W_C + W_D, D_MODEL), (W_C + W_D) ** -0.5),
        "ffn_w1": nrm((DEPTH, D_MODEL, D_FF), D_MODEL ** -0.5),
        "ffn_w2": nrm((DEPTH, D_FF, D_MODEL), D_FF ** -0.5),
        "norm_final": 1.0 + nrm((D_MODEL,), 0.02),
    }


def reference(x_prompt, x_sample, state_conv_a, state_conv_b, state_pool,
              norm_mix, norm_ffn, ev_w_in, ev_conv_a, ev_conv_b, ev_conv_b_bias,
              ev_ln_g, ev_ln_b, ev_w_out, od_w_in, od_pool_w, od_pool_scale,
              od_sgu_w, od_sgu_b, od_sgu_ln_g, od_sgu_ln_b, od_w_out,
              ffn_w1, ffn_w2, norm_final):

    def run(x, start_pos, hist_a, hist_b, hist_c):
        sa_l, sb_l, sc_l, v_l = [], [], [], []
        for l in range(DEPTH):
            i = l // 2
            h = rms_norm(x, norm_mix[l])
            if l % 2 == 0:
                y, sa, sb = even_mixer(h, hist_a[i], hist_b[i], ev_w_in[i], ev_conv_a[i],
                                       ev_conv_b[i], ev_conv_b_bias[i], ev_ln_g[i],
                                       ev_ln_b[i], ev_w_out[i])
                sa_l.append(sa)
                sb_l.append(sb)
            else:
                y, sc, vn = odd_mixer(h, hist_c[i], start_pos, od_w_in[i], od_pool_w[i],
                                      od_pool_scale[i], od_sgu_w[i], od_sgu_b[i],
                                      od_sgu_ln_g[i], od_sgu_ln_b[i], od_w_out[i])
                sc_l.append(sc)
                v_l.append(vn)
            x = x + y
            hf = rms_norm(x, norm_ffn[l])
            x = x + jnp.square(jax.nn.relu(hf @ ffn_w1[l])) @ ffn_w2[l]
        return (rms_norm(x, norm_final), jnp.stack(sa_l), jnp.stack(sb_l),
                jnp.stack(sc_l), jnp.stack(v_l))

    dt = x_prompt.dtype
    zeros_a = jnp.zeros((N_EVEN, BATCH, K_A - 1, W_A), dt)
    zeros_b = jnp.zeros((N_EVEN, BATCH, K_B - 1, W_B), dt)
    zeros_c = jnp.zeros((N_ODD, BATCH, POOL_HIST, W_C), dt)
    y_prompt, new_conv_a_prompt, new_conv_b_prompt, new_pool_prompt, _ = run(
        x_prompt, 0, zeros_a, zeros_b, zeros_c)
    y_sample, new_conv_a_sample, new_conv_b_sample, new_pool_sample, new_chunk_v_sample = run(
        x_sample, PAST_LEN, state_conv_a, state_conv_b, state_pool)
    return (y_prompt, y_sample, new_conv_a_prompt, new_conv_a_sample,
            new_conv_b_prompt, new_conv_b_sample, new_pool_prompt, new_pool_sample,
            new_chunk_v_sample)
```

```python
import functools

import jax
import jax.numpy as jnp
from jax import lax
from jax.experimental import pallas as pl
from jax.experimental.pallas import tpu as pltpu

F32 = jnp.float32
BF16 = jnp.bfloat16

D_MODEL = 1024
W_MIX = D_MODEL // 2
K_A = 3
K_B = 31
POOL_WINDOWS = (2, 4, 8, 16)
POOL_HIST = max(POOL_WINDOWS) - 1
CHUNK = 128
N_GROUPS = 4
GROUP_W = W_MIX // N_GROUPS
D_FF = 4 * D_MODEL
PAST_LEN = 16384
EPS = 1e-6

SUBLANES = 8
HIST_A = 8
HIST_B = 32
HIST_P = 16
ROW_CHUNK = 64
FFN_COL_CHUNK = 1024
VMEM_LIMIT = 56 * 1024 * 1024


def _rms(x, g):
    return x * lax.rsqrt(jnp.mean(x * x, axis=-1, keepdims=True) + EPS) * g


def _ln(x, g, b):
    mu = jnp.mean(x, axis=-1, keepdims=True)
    xc = x - mu
    return xc * lax.rsqrt(jnp.mean(xc * xc, axis=-1, keepdims=True) + EPS) * g + b


def _dot(a, b):
    return jnp.dot(a, b, preferred_element_type=F32)


def _silu(x):
    return x * jax.nn.sigmoid(x)


def _const_spec(shape):
    nd = len(shape)
    return pl.BlockSpec(shape, lambda *_: (0,) * nd, pipeline_mode=pl.Buffered(1))


def _params(n_grid_axes):
    return pltpu.CompilerParams(
        dimension_semantics=("arbitrary",) * n_grid_axes,
        vmem_limit_bytes=VMEM_LIMIT,
    )


def _ffn_kernel(*refs, final):
    if final:
        x_ref, g_ref, w1_ref, w2_ref, gf_ref, o_ref, hb_ref, act_ref = refs
    else:
        x_ref, g_ref, w1_ref, w2_ref, o_ref, hb_ref, act_ref = refs
    hb_ref[...] = _rms(x_ref[...], g_ref[...]).astype(BF16)
    for c in range(D_FF // FFN_COL_CHUNK):
        cols = slice(c * FFN_COL_CHUNK, (c + 1) * FFN_COL_CHUNK)
        a = _dot(hb_ref[...], w1_ref[:, cols])
        act_ref[:, cols] = jnp.square(jnp.maximum(a, 0.0)).astype(BF16)
    y = x_ref[...] + _dot(act_ref[...], w2_ref[...])
    if final:
        y = _rms(y, gf_ref[...])
    o_ref[...] = y


def _ffn(x2d, g, w1, w2, g_final, tm):
    m = x2d.shape[0]
    final = g_final is not None
    row_spec = pl.BlockSpec((tm, D_MODEL), lambda i: (i, 0))
    in_specs = [row_spec, _const_spec((1, D_MODEL)), _const_spec((D_MODEL, D_FF)),
                _const_spec((D_FF, D_MODEL))]
    args = [x2d, g.reshape(1, D_MODEL), w1, w2]
    if final:
        in_specs.append(_const_spec((1, D_MODEL)))
        args.append(g_final.reshape(1, D_MODEL))
    return pl.pallas_call(
        functools.partial(_ffn_kernel, final=final),
        grid=(m // tm,),
        in_specs=in_specs,
        out_specs=row_spec,
        out_shape=jax.ShapeDtypeStruct((m, D_MODEL), F32),
        scratch_shapes=[pltpu.VMEM((tm, D_MODEL), BF16), pltpu.VMEM((tm, D_FF), BF16)],
        compiler_params=_params(1),
        name="ffn_final" if final else "ffn",
    )(*args)


def _even_prompt_kernel(x_ref, g_ref, win_ref, ca_ref, cb_ref, cbb_ref, lng_ref, lnb_ref, wout_ref,
                        o_ref, sa_ref, sb_ref,
                        hb_ref, gate_ref, ua_ext, ub_ext, cat_ref, *, tm):
    s = pl.program_id(1)

    @pl.when(s == 0)
    def _():
        ua_ext[0:HIST_A, :] = jnp.zeros((HIST_A, W_MIX), F32)
        ub_ext[0:HIST_B, :] = jnp.zeros((HIST_B, W_MIX), F32)

    @pl.when(s > 0)
    def _():
        ua_ext[0:HIST_A, :] = ua_ext[tm:tm + HIST_A, :]
        ub_ext[0:HIST_B, :] = ub_ext[tm:tm + HIST_B, :]

    hb_ref[...] = _rms(x_ref[...], g_ref[...]).astype(BF16)

    def seg(i):
        return _dot(hb_ref[...], win_ref[:, i * W_MIX:(i + 1) * W_MIX])

    gate_ref[...] = seg(0)
    ua_ext[HIST_A:HIST_A + tm, :] = seg(1) * seg(2)
    ub_ext[HIST_B:HIST_B + tm, :] = seg(3) * jax.nn.sigmoid(seg(4))

    def body(c, carry):
        r0 = pl.multiple_of(c * ROW_CHUNK, ROW_CHUNK)
        win_a = ua_ext.at[pl.ds(r0, ROW_CHUNK + HIST_A), :]
        acc = jnp.zeros((ROW_CHUNK, W_MIX), F32)
        for k in range(K_A):
            off = HIST_A - (K_A - 1) + k
            acc = acc + ca_ref[k:k + 1, :] * win_a[off:off + ROW_CHUNK, :]
        ya = gate_ref[pl.ds(r0, ROW_CHUNK), :] * acc
        cat_ref[pl.ds(r0, ROW_CHUNK), 0:W_MIX] = ya.astype(BF16)

        win_b = ub_ext.at[pl.ds(r0, ROW_CHUNK + HIST_B), :]
        acc = jnp.zeros((ROW_CHUNK, W_MIX), F32)
        for k in range(K_B):
            off = HIST_B - (K_B - 1) + k
            acc = acc + cb_ref[k:k + 1, :] * win_b[off:off + ROW_CHUNK, :]
        yb = _silu(_ln(acc + cbb_ref[...], lng_ref[...], lnb_ref[...]))
        cat_ref[pl.ds(r0, ROW_CHUNK), W_MIX:2 * W_MIX] = yb.astype(BF16)
        return carry

    lax.fori_loop(0, tm // ROW_CHUNK, body, 0)

    o_ref[...] = x_ref[...] + _dot(cat_ref[...], wout_ref[...])

    @pl.when(s == pl.num_programs(1) - 1)
    def _():
        sa_ref[...] = ua_ext[HIST_A + tm - (K_A - 1):HIST_A + tm, :]
        sb_ref[...] = ub_ext[HIST_B + tm - (K_B - 1):HIST_B + tm, :]


def _even_prompt(x, g, win, ca, cb, cbb, lng, lnb, wout, tm):
    b, s, _ = x.shape
    tile = pl.BlockSpec((None, tm, D_MODEL), lambda i, j: (i, j, 0))
    n_in = win.shape[1]
    return pl.pallas_call(
        functools.partial(_even_prompt_kernel, tm=tm),
        grid=(b, s // tm),
        in_specs=[tile, _const_spec((1, D_MODEL)), _const_spec((D_MODEL, n_in)),
                  _const_spec((K_A, W_MIX)), _const_spec((K_B, W_MIX)), _const_spec((1, W_MIX)),
                  _const_spec((1, W_MIX)), _const_spec((1, W_MIX)), _const_spec((2 * W_MIX, D_MODEL))],
        out_specs=[tile,
                   pl.BlockSpec((None, K_A - 1, W_MIX), lambda i, j: (i, 0, 0)),
                   pl.BlockSpec((None, K_B - 1, W_MIX), lambda i, j: (i, 0, 0))],
        out_shape=[jax.ShapeDtypeStruct((b, s, D_MODEL), F32),
                   jax.ShapeDtypeStruct((b, K_A - 1, W_MIX), F32),
                   jax.ShapeDtypeStruct((b, K_B - 1, W_MIX), F32)],
        scratch_shapes=[pltpu.VMEM((tm, D_MODEL), BF16),
                        pltpu.VMEM((tm, W_MIX), F32),
                        pltpu.VMEM((tm + HIST_A, W_MIX), F32),
                        pltpu.VMEM((tm + HIST_B, W_MIX), F32),
                        pltpu.VMEM((tm, 2 * W_MIX), BF16)],
        compiler_params=_params(2),
        name="even_prompt",
    )(x, g.reshape(1, D_MODEL), win, ca, cb, cbb.reshape(1, W_MIX), lng.reshape(1, W_MIX),
      lnb.reshape(1, W_MIX), wout)


def _tril_bf16(w):
    t = lax.broadcasted_iota(jnp.int32, (CHUNK, CHUNK), 0)
    s = lax.broadcasted_iota(jnp.int32, (CHUNK, CHUNK), 1)
    return jnp.where(s <= t, w, 0.0).astype(BF16)


def _odd_prompt_kernel(x_ref, g_ref, win_ref, wpool_ref, pscale_ref, sguw_ref, sgubt_ref, lng_ref,
                       lnb_ref, wout_ref,
                       o_ref, sp_ref,
                       hb_ref, p_ext, u_ref, vn_ref, d_ref, cat_ref, wm_ref, *, tm):
    s = pl.program_id(1)

    @pl.when(jnp.logical_and(pl.program_id(0) == 0, s == 0))
    def _():
        for gi in range(N_GROUPS):
            wm_ref[gi] = _tril_bf16(sguw_ref[gi])

    @pl.when(s == 0)
    def _():
        p_ext[0:HIST_P, :] = jnp.zeros((HIST_P, W_MIX), F32)

    @pl.when(s > 0)
    def _():
        p_ext[0:HIST_P, :] = p_ext[tm:tm + HIST_P, :]

    hb_ref[...] = _rms(x_ref[...], g_ref[...]).astype(BF16)

    def seg(i):
        return _dot(hb_ref[...], win_ref[:, i * W_MIX:(i + 1) * W_MIX])

    p_ext[HIST_P:HIST_P + tm, :] = seg(0)
    u_ref[...] = seg(1)
    vn_ref[...] = _ln(seg(2), lng_ref[...], lnb_ref[...]).astype(BF16)

    def pool_body(c, carry):
        r0 = pl.multiple_of(c * ROW_CHUNK, ROW_CHUNK)
        win = p_ext.at[pl.ds(r0, ROW_CHUNK + HIST_P), :]
        pos = s * tm + r0 + lax.broadcasted_iota(jnp.int32, (ROW_CHUNK, GROUP_W), 0)
        for gi, w in enumerate(POOL_WINDOWS):
            lanes = slice(gi * GROUP_W, (gi + 1) * GROUP_W)
            tok = win[HIST_P:HIST_P + ROW_CHUNK, lanes]
            wsum = tok
            for j in range(1, w):
                wsum = wsum + win[HIST_P - j:HIST_P - j + ROW_CHUNK, lanes]
            cnt = jnp.minimum(pos + 1, w).astype(F32)
            d_ref[pl.ds(r0, ROW_CHUNK), lanes] = (wsum / cnt - tok).astype(BF16)
        return carry

    lax.fori_loop(0, tm // ROW_CHUNK, pool_body, 0)

    for gi in range(N_GROUPS):
        lanes = slice(gi * GROUP_W, (gi + 1) * GROUP_W)
        yc = _dot(d_ref[:, lanes], wpool_ref[gi]) * pscale_ref[:, lanes]
        cat_ref[:, lanes] = yc.astype(BF16)

    for c in range(tm // CHUNK):
        rows = slice(c * CHUNK, (c + 1) * CHUNK)
        for gi in range(N_GROUPS):
            lanes = slice(gi * GROUP_W, (gi + 1) * GROUP_W)
            mixed = _dot(wm_ref[gi], vn_ref[rows, lanes]) + sgubt_ref[:, gi:gi + 1]
            yd = u_ref[rows, lanes] * mixed
            cat_ref[rows, W_MIX + gi * GROUP_W:W_MIX + (gi + 1) * GROUP_W] = yd.astype(BF16)

    o_ref[...] = x_ref[...] + _dot(cat_ref[...], wout_ref[...])

    @pl.when(s == pl.num_programs(1) - 1)
    def _():
        sp_ref[...] = p_ext[HIST_P + tm - POOL_HIST:HIST_P + tm, :]


def _odd_prompt(x, g, win, wpool, pscale, sguw, sgub, lng, lnb, wout, tm):
    b, s, _ = x.shape
    tile = pl.BlockSpec((None, tm, D_MODEL), lambda i, j: (i, j, 0))
    n_in = win.shape[1]
    return pl.pallas_call(
        functools.partial(_odd_prompt_kernel, tm=tm),
        grid=(b, s // tm),
        in_specs=[tile, _const_spec((1, D_MODEL)), _const_spec((D_MODEL, n_in)),
                  _const_spec((N_GROUPS, GROUP_W, GROUP_W)), _const_spec((1, W_MIX)),
                  _const_spec((N_GROUPS, CHUNK, CHUNK)), _const_spec((CHUNK, N_GROUPS)),
                  _const_spec((1, W_MIX)), _const_spec((1, W_MIX)),
                  _const_spec((2 * W_MIX, D_MODEL))],
        out_specs=[tile, pl.BlockSpec((None, POOL_HIST, W_MIX), lambda i, j: (i, 0, 0))],
        out_shape=[jax.ShapeDtypeStruct((b, s, D_MODEL), F32),
                   jax.ShapeDtypeStruct((b, POOL_HIST, W_MIX), F32)],
        scratch_shapes=[pltpu.VMEM((tm, D_MODEL), BF16),
                        pltpu.VMEM((tm + HIST_P, W_MIX), F32),
                        pltpu.VMEM((tm, W_MIX), F32),
                        pltpu.VMEM((tm, W_MIX), BF16),
                        pltpu.VMEM((tm, W_MIX), BF16),
                        pltpu.VMEM((tm, 2 * W_MIX), BF16),
                        pltpu.VMEM((N_GROUPS, CHUNK, CHUNK), BF16)],
        compiler_params=_params(2),
        name="odd_prompt",
    )(x, g.reshape(1, D_MODEL), win, wpool, pscale.reshape(1, W_MIX), sguw, sgub.T,
      lng.reshape(1, W_MIX), lnb.reshape(1, W_MIX), wout)


def _even_sample_kernel(x_ref, g_ref, win_ref, ca_ref, cb_ref, cbb_ref, lng_ref, lnb_ref, wout_ref,
                        ha_ref, hbst_ref,
                        o_ref, sa_ref, sb_ref,
                        hb_ref, gate_ref, ua_ext, ub_ext, cbv_ref, cat_ref, *, t_len, n_seq):
    x = x_ref[...].reshape(t_len * n_seq, D_MODEL)
    hb_ref[...] = _rms(x, g_ref[...]).astype(BF16)

    def seg(i):
        return _dot(hb_ref[...], win_ref[:, i * W_MIX:(i + 1) * W_MIX])

    gate_ref[...] = seg(0)
    ua_ext[0:K_A - 1] = ha_ref[...]
    ub_ext[0:K_B - 1] = hbst_ref[...]
    ua_ext[K_A - 1:K_A - 1 + t_len] = (seg(1) * seg(2)).reshape(t_len, n_seq, W_MIX)
    ub_ext[K_B - 1:K_B - 1 + t_len] = (seg(3) * jax.nn.sigmoid(seg(4))).reshape(t_len, n_seq, W_MIX)

    def body(t, carry):
        rows = pl.ds(pl.multiple_of(t * n_seq, n_seq), n_seq)
        for gi in range(N_GROUPS):
            lanes = slice(gi * GROUP_W, (gi + 1) * GROUP_W)
            acc = jnp.zeros((n_seq, GROUP_W), F32)
            for k in range(K_A):
                acc = acc + ca_ref[k:k + 1, lanes] * ua_ext[t + k, :, lanes]
            cat_ref[rows, lanes] = (gate_ref[rows, lanes] * acc).astype(BF16)
            acc = jnp.zeros((n_seq, GROUP_W), F32)
            for k in range(K_B):
                acc = acc + cb_ref[k:k + 1, lanes] * ub_ext[t + k, :, lanes]
            cbv_ref[:, lanes] = acc + cbb_ref[:, lanes]
        yb = _silu(_ln(cbv_ref[...], lng_ref[...], lnb_ref[...]))
        cat_ref[rows, W_MIX:2 * W_MIX] = yb.astype(BF16)
        return carry

    lax.fori_loop(0, t_len, body, 0)

    o_ref[...] = (x + _dot(cat_ref[...], wout_ref[...])).reshape(t_len, n_seq, D_MODEL)
    sa_ref[...] = ua_ext[t_len:t_len + K_A - 1]
    sb_ref[...] = ub_ext[t_len:t_len + K_B - 1]


def _seq_block_spec(rows, nb, width):
    return pl.BlockSpec((rows, nb, width), lambda i, *_: (0, i, 0))


def _even_sample(x, g, win, ca, cb, cbb, lng, lnb, wout, hist_a, hist_b, nb):
    t_len, n_seq, _ = x.shape
    m = t_len * nb
    n_in = win.shape[1]
    return pl.pallas_call(
        functools.partial(_even_sample_kernel, t_len=t_len, n_seq=nb),
        grid=(n_seq // nb,),
        in_specs=[_seq_block_spec(t_len, nb, D_MODEL), _const_spec((1, D_MODEL)),
                  _const_spec((D_MODEL, n_in)),
                  _const_spec((K_A, W_MIX)), _const_spec((K_B, W_MIX)), _const_spec((1, W_MIX)),
                  _const_spec((1, W_MIX)), _const_spec((1, W_MIX)), _const_spec((2 * W_MIX, D_MODEL)),
                  _seq_block_spec(K_A - 1, nb, W_MIX), _seq_block_spec(K_B - 1, nb, W_MIX)],
        out_specs=[_seq_block_spec(t_len, nb, D_MODEL), _seq_block_spec(K_A - 1, nb, W_MIX),
                   _seq_block_spec(K_B - 1, nb, W_MIX)],
        out_shape=[jax.ShapeDtypeStruct((t_len, n_seq, D_MODEL), F32),
                   jax.ShapeDtypeStruct((K_A - 1, n_seq, W_MIX), F32),
                   jax.ShapeDtypeStruct((K_B - 1, n_seq, W_MIX), F32)],
        scratch_shapes=[pltpu.VMEM((m, D_MODEL), BF16),
                        pltpu.VMEM((m, W_MIX), F32),
                        pltpu.VMEM((t_len + K_A - 1, nb, W_MIX), F32),
                        pltpu.VMEM((t_len + K_B - 1, nb, W_MIX), F32),
                        pltpu.VMEM((nb, W_MIX), F32),
                        pltpu.VMEM((m, 2 * W_MIX), BF16)],
        compiler_params=_params(1),
        name="even_sample",
    )(x, g.reshape(1, D_MODEL), win, ca, cb, cbb.reshape(1, W_MIX), lng.reshape(1, W_MIX),
      lnb.reshape(1, W_MIX), wout, hist_a, hist_b)


def _odd_sample_kernel(sw_ref, sb_ref_smem,
                       x_ref, g_ref, win_ref, wpool_ref, pscale_ref, lng_ref, lnb_ref, wout_ref,
                       hp_ref,
                       o_ref, sp_ref, vn_out_ref,
                       hb_ref, p_ext, u_ref, d_ref, cat_ref, *, t_len, n_seq, start_pos):
    x = x_ref[...].reshape(t_len * n_seq, D_MODEL)
    hb_ref[...] = _rms(x, g_ref[...]).astype(BF16)

    def seg(i):
        return _dot(hb_ref[...], win_ref[:, i * W_MIX:(i + 1) * W_MIX])

    p_ext[0:POOL_HIST] = hp_ref[...]
    p_ext[POOL_HIST:POOL_HIST + t_len] = seg(0).reshape(t_len, n_seq, W_MIX)
    u_ref[...] = seg(1).reshape(t_len, n_seq, W_MIX)
    vn_out_ref[...] = _ln(seg(2), lng_ref[...], lnb_ref[...]).reshape(t_len, n_seq, W_MIX)

    for t in range(t_len):
        rows = slice(t * n_seq, (t + 1) * n_seq)
        for gi, w in enumerate(POOL_WINDOWS):
            lanes = slice(gi * GROUP_W, (gi + 1) * GROUP_W)
            tok = p_ext[POOL_HIST + t, :, lanes]
            wsum = tok
            for j in range(1, w):
                wsum = wsum + p_ext[POOL_HIST + t - j, :, lanes]
            cnt = float(min(start_pos + t + 1, w))
            d_ref[rows, lanes] = (wsum / cnt - tok).astype(BF16)
            mixed = jnp.zeros((n_seq, GROUP_W), F32) + sb_ref_smem[gi * t_len + t]
            for s in range(t + 1):
                mixed = mixed + sw_ref[(gi * t_len + t) * t_len + s] * vn_out_ref[s, :, lanes]
            cat_ref[rows, W_MIX + gi * GROUP_W:W_MIX + (gi + 1) * GROUP_W] = (
                u_ref[t, :, lanes] * mixed).astype(BF16)

    for gi in range(N_GROUPS):
        lanes = slice(gi * GROUP_W, (gi + 1) * GROUP_W)
        yc = _dot(d_ref[:, lanes], wpool_ref[gi]) * pscale_ref[:, lanes]
        cat_ref[:, lanes] = yc.astype(BF16)

    o_ref[...] = (x + _dot(cat_ref[...], wout_ref[...])).reshape(t_len, n_seq, D_MODEL)
    sp_ref[...] = p_ext[t_len:t_len + POOL_HIST]


def _odd_sample(x, g, win, wpool, pscale, sguw, sgub, lng, lnb, wout, hist_p, nb, start_pos):
    t_len, n_seq, _ = x.shape
    m = t_len * nb
    n_in = win.shape[1]
    assert t_len <= CHUNK
    sw = sguw[:, :t_len, :t_len].reshape(-1)
    sb = sgub[:, :t_len].reshape(-1)

    grid_spec = pltpu.PrefetchScalarGridSpec(
        num_scalar_prefetch=2,
        grid=(n_seq // nb,),
        in_specs=[_seq_block_spec(t_len, nb, D_MODEL), _const_spec((1, D_MODEL)),
                  _const_spec((D_MODEL, n_in)),
                  _const_spec((N_GROUPS, GROUP_W, GROUP_W)), _const_spec((1, W_MIX)),
                  _const_spec((1, W_MIX)), _const_spec((1, W_MIX)),
                  _const_spec((2 * W_MIX, D_MODEL)),
                  _seq_block_spec(POOL_HIST, nb, W_MIX)],
        out_specs=[_seq_block_spec(t_len, nb, D_MODEL), _seq_block_spec(POOL_HIST, nb, W_MIX),
                   _seq_block_spec(t_len, nb, W_MIX)],
        scratch_shapes=[pltpu.VMEM((m, D_MODEL), BF16),
                        pltpu.VMEM((t_len + POOL_HIST, nb, W_MIX), F32),
                        pltpu.VMEM((t_len, nb, W_MIX), F32),
                        pltpu.VMEM((m, W_MIX), BF16),
                        pltpu.VMEM((m, 2 * W_MIX), BF16)],
    )
    return pl.pallas_call(
        functools.partial(_odd_sample_kernel, t_len=t_len, n_seq=nb, start_pos=start_pos),
        grid_spec=grid_spec,
        out_shape=[jax.ShapeDtypeStruct((t_len, n_seq, D_MODEL), F32),
                   jax.ShapeDtypeStruct((POOL_HIST, n_seq, W_MIX), F32),
                   jax.ShapeDtypeStruct((t_len, n_seq, W_MIX), F32)],
        compiler_params=_params(1),
        name="odd_sample",
    )(sw, sb, x, g.reshape(1, D_MODEL), win, wpool, pscale.reshape(1, W_MIX),
      lng.reshape(1, W_MIX), lnb.reshape(1, W_MIX), wout, hist_p)


def _time_major(a):
    return jnp.swapaxes(a, 0, 1)


def kernel(x_prompt, x_sample, state_conv_a, state_conv_b, state_pool, norm_mix, norm_ffn, ev_w_in,
           ev_conv_a, ev_conv_b, ev_conv_b_bias, ev_ln_g, ev_ln_b, ev_w_out, od_w_in, od_pool_w,
           od_pool_scale, od_sgu_w, od_sgu_b, od_sgu_ln_g, od_sgu_ln_b, od_w_out, ffn_w1, ffn_w2,
           norm_final):
    depth = norm_mix.shape[0]
    batch, seq, _ = x_prompt.shape
    n_seq, t_len, _ = x_sample.shape
    tm_mix = 512
    tm_ffn = 512
    nb = 32

    xp = x_prompt
    xs = _time_major(x_sample)
    sa_p, sa_s, sb_p, sb_s, sc_p, sc_s, vn_s = [], [], [], [], [], [], []
    for l in range(depth):
        i = l // 2
        if l % 2 == 0:
            win, wout = ev_w_in[i].astype(BF16), ev_w_out[i].astype(BF16)
            common = (norm_mix[l], win, ev_conv_a[i], ev_conv_b[i], ev_conv_b_bias[i], ev_ln_g[i],
                      ev_ln_b[i], wout)
            xp, a_p, b_p = _even_prompt(xp, *common, tm=tm_mix)
            xs, a_s, b_s = _even_sample(xs, *common, _time_major(state_conv_a[i]),
                                        _time_major(state_conv_b[i]), nb)
            sa_p.append(a_p)
            sb_p.append(b_p)
            sa_s.append(_time_major(a_s))
            sb_s.append(_time_major(b_s))
        else:
            win, wout = od_w_in[i].astype(BF16), od_w_out[i].astype(BF16)
            common = (norm_mix[l], win, od_pool_w[i].astype(BF16), od_pool_scale[i], od_sgu_w[i],
                      od_sgu_b[i], od_sgu_ln_g[i], od_sgu_ln_b[i], wout)
            xp, c_p = _odd_prompt(xp, *common, tm=tm_mix)
            xs, c_s, v_s = _odd_sample(xs, *common, _time_major(state_pool[i]), nb, PAST_LEN)
            sc_p.append(c_p)
            sc_s.append(_time_major(c_s))
            vn_s.append(_time_major(v_s))
        w1, w2 = ffn_w1[l].astype(BF16), ffn_w2[l].astype(BF16)
        g_final = norm_final if l == depth - 1 else None
        xp = _ffn(xp.reshape(batch * seq, D_MODEL), norm_ffn[l], w1, w2, g_final, tm_ffn
                  ).reshape(batch, seq, D_MODEL)
        xs = _ffn(xs.reshape(t_len * n_seq, D_MODEL), norm_ffn[l], w1, w2, g_final, tm_ffn
                  ).reshape(t_len, n_seq, D_MODEL)

    y_sample = _time_major(xs)
    return (xp, y_sample, jnp.stack(sa_p), jnp.stack(sa_s), jnp.stack(sb_p), jnp.stack(sb_s),
            jnp.stack(sc_p), jnp.stack(sc_s), jnp.stack(vn_s))
```

```python
import functools

import jax
import jax.numpy as jnp
from jax import lax
from jax.experimental import pallas as pl
from jax.experimental.pallas import tpu as pltpu

F32 = jnp.float32
BF16 = jnp.bfloat16

D_MODEL = 1024
W_MIX = D_MODEL // 2
K_A = 3
K_B = 31
POOL_WINDOWS = (2, 4, 8, 16)
POOL_HIST = max(POOL_WINDOWS) - 1
CHUNK = 128
N_GROUPS = 4
GROUP_W = W_MIX // N_GROUPS
D_FF = 4 * D_MODEL
PAST_LEN = 16384
EPS = 1e-6

SUBLANES = 8
HIST_A = 8
HIST_B = 32
HIST_P = 16
ROW_CHUNK = 64
CONV_ROWS = 128
FFN_COL_CHUNK = 1024
VMEM_LIMIT = 56 * 1024 * 1024


def _rms(x, g):
    return x * lax.rsqrt(jnp.mean(x * x, axis=-1, keepdims=True) + EPS) * g


def _ln(x, g, b):
    mu = jnp.mean(x, axis=-1, keepdims=True)
    xc = x - mu
    return xc * lax.rsqrt(jnp.mean(xc * xc, axis=-1, keepdims=True) + EPS) * g + b


def _dot(a, b):
    return jnp.dot(a, b, preferred_element_type=F32)


def _silu(x):
    return x * jax.nn.sigmoid(x)


def _const_spec(shape):
    nd = len(shape)
    return pl.BlockSpec(shape, lambda *_: (0,) * nd, pipeline_mode=pl.Buffered(1))


def _params(n_grid_axes):
    return pltpu.CompilerParams(
        dimension_semantics=("arbitrary",) * n_grid_axes,
        vmem_limit_bytes=VMEM_LIMIT,
    )


def _ffn_kernel(*refs, final):
    if final:
        x_ref, g_ref, w1_ref, w2_ref, gf_ref, o_ref, hb_ref, act_ref = refs
    else:
        x_ref, g_ref, w1_ref, w2_ref, o_ref, hb_ref, act_ref = refs
    hb_ref[...] = _rms(x_ref[...], g_ref[...]).astype(BF16)
    for c in range(D_FF // FFN_COL_CHUNK):
        cols = slice(c * FFN_COL_CHUNK, (c + 1) * FFN_COL_CHUNK)
        a = _dot(hb_ref[...], w1_ref[:, cols])
        act_ref[:, cols] = jnp.square(jnp.maximum(a, 0.0)).astype(BF16)
    y = x_ref[...] + _dot(act_ref[...], w2_ref[...])
    if final:
        y = _rms(y, gf_ref[...])
    o_ref[...] = y


def _ffn(x2d, g, w1, w2, g_final, tm):
    m = x2d.shape[0]
    final = g_final is not None
    row_spec = pl.BlockSpec((tm, D_MODEL), lambda i: (i, 0))
    in_specs = [row_spec, _const_spec((1, D_MODEL)), _const_spec((D_MODEL, D_FF)),
                _const_spec((D_FF, D_MODEL))]
    args = [x2d, g.reshape(1, D_MODEL), w1, w2]
    if final:
        in_specs.append(_const_spec((1, D_MODEL)))
        args.append(g_final.reshape(1, D_MODEL))
    return pl.pallas_call(
        functools.partial(_ffn_kernel, final=final),
        grid=(m // tm,),
        in_specs=in_specs,
        out_specs=row_spec,
        out_shape=jax.ShapeDtypeStruct((m, D_MODEL), F32),
        scratch_shapes=[pltpu.VMEM((tm, D_MODEL), BF16), pltpu.VMEM((tm, D_FF), BF16)],
        compiler_params=_params(1),
        name="ffn_final" if final else "ffn",
    )(*args)


def _causal_dwconv(win, w_ref, n_taps, hist, rows, lanes):
    out = None
    for r in range(SUBLANES):
        acc = None
        for q in range(hist // SUBLANES):
            k = n_taps - 1 - SUBLANES * q - r
            if k < 0:
                continue
            start = hist - SUBLANES * (q + 1)
            term = w_ref[k:k + 1, lanes] * win[start:start + rows + SUBLANES, lanes]
            acc = term if acc is None else acc + term
        if acc is None:
            continue
        part = acc[SUBLANES - r:SUBLANES - r + rows]
        out = part if out is None else out + part
    return out


def _even_prompt_kernel(x_ref, g_ref, win_ref, ca_ref, cb_ref, cbb_ref, lng_ref, lnb_ref, wout_ref,
                        o_ref, sa_ref, sb_ref,
                        hb_ref, gate_ref, ua_ext, ub_ext, cbv_ref, cat_ref, *, tm):
    s = pl.program_id(1)

    @pl.when(s == 0)
    def _():
        ua_ext[0:HIST_A, :] = jnp.zeros((HIST_A, W_MIX), F32)
        ub_ext[0:HIST_B, :] = jnp.zeros((HIST_B, W_MIX), F32)

    @pl.when(s > 0)
    def _():
        ua_ext[0:HIST_A, :] = ua_ext[tm:tm + HIST_A, :]
        ub_ext[0:HIST_B, :] = ub_ext[tm:tm + HIST_B, :]

    hb_ref[...] = _rms(x_ref[...], g_ref[...]).astype(BF16)

    def seg(i):
        return _dot(hb_ref[...], win_ref[:, i * W_MIX:(i + 1) * W_MIX])

    gate_ref[...] = seg(0)
    ua_ext[HIST_A:HIST_A + tm, :] = seg(1) * seg(2)
    ub_ext[HIST_B:HIST_B + tm, :] = seg(3) * jax.nn.sigmoid(seg(4))

    def body(c, carry):
        r0 = pl.multiple_of(c * CONV_ROWS, CONV_ROWS)
        win_a = ua_ext.at[pl.ds(r0, CONV_ROWS + HIST_A), :]
        win_b = ub_ext.at[pl.ds(r0, CONV_ROWS + HIST_B), :]
        for gi in range(N_GROUPS):
            lanes = slice(gi * GROUP_W, (gi + 1) * GROUP_W)
            ya = gate_ref[pl.ds(r0, CONV_ROWS), lanes] * _causal_dwconv(
                win_a, ca_ref, K_A, HIST_A, CONV_ROWS, lanes)
            cat_ref[pl.ds(r0, CONV_ROWS), lanes] = ya.astype(BF16)
            cbv_ref[:, lanes] = _causal_dwconv(win_b, cb_ref, K_B, HIST_B, CONV_ROWS, lanes)
        yb = _silu(_ln(cbv_ref[...] + cbb_ref[...], lng_ref[...], lnb_ref[...]))
        cat_ref[pl.ds(r0, CONV_ROWS), W_MIX:2 * W_MIX] = yb.astype(BF16)
        return carry

    lax.fori_loop(0, tm // CONV_ROWS, body, 0)

    o_ref[...] = x_ref[...] + _dot(cat_ref[...], wout_ref[...])

    @pl.when(s == pl.num_programs(1) - 1)
    def _():
        sa_ref[...] = ua_ext[HIST_A + tm - (K_A - 1):HIST_A + tm, :]
        sb_ref[...] = ub_ext[HIST_B + tm - (K_B - 1):HIST_B + tm, :]


def _even_prompt(x, g, win, ca, cb, cbb, lng, lnb, wout, tm):
    b, s, _ = x.shape
    tile = pl.BlockSpec((None, tm, D_MODEL), lambda i, j: (i, j, 0))
    n_in = win.shape[1]
    return pl.pallas_call(
        functools.partial(_even_prompt_kernel, tm=tm),
        grid=(b, s // tm),
        in_specs=[tile, _const_spec((1, D_MODEL)), _const_spec((D_MODEL, n_in)),
                  _const_spec((K_A, W_MIX)), _const_spec((K_B, W_MIX)), _const_spec((1, W_MIX)),
                  _const_spec((1, W_MIX)), _const_spec((1, W_MIX)), _const_spec((2 * W_MIX, D_MODEL))],
        out_specs=[tile,
                   pl.BlockSpec((None, K_A - 1, W_MIX), lambda i, j: (i, 0, 0)),
                   pl.BlockSpec((None, K_B - 1, W_MIX), lambda i, j: (i, 0, 0))],
        out_shape=[jax.ShapeDtypeStruct((b, s, D_MODEL), F32),
                   jax.ShapeDtypeStruct((b, K_A - 1, W_MIX), F32),
                   jax.ShapeDtypeStruct((b, K_B - 1, W_MIX), F32)],
        scratch_shapes=[pltpu.VMEM((tm, D_MODEL), BF16),
                        pltpu.VMEM((tm, W_MIX), F32),
                        pltpu.VMEM((tm + HIST_A, W_MIX), F32),
                        pltpu.VMEM((tm + HIST_B, W_MIX), F32),
                        pltpu.VMEM((CONV_ROWS, W_MIX), F32),
                        pltpu.VMEM((tm, 2 * W_MIX), BF16)],
        compiler_params=_params(2),
        name="even_prompt",
    )(x, g.reshape(1, D_MODEL), win, ca, cb, cbb.reshape(1, W_MIX), lng.reshape(1, W_MIX),
      lnb.reshape(1, W_MIX), wout)


def _tril_bf16(w):
    t = lax.broadcasted_iota(jnp.int32, (CHUNK, CHUNK), 0)
    s = lax.broadcasted_iota(jnp.int32, (CHUNK, CHUNK), 1)
    return jnp.where(s <= t, w, 0.0).astype(BF16)


def _odd_prompt_kernel(x_ref, g_ref, win_ref, wpool_ref, pscale_ref, sguw_ref, sgubt_ref, lng_ref,
                       lnb_ref, wout_ref,
                       o_ref, sp_ref,
                       hb_ref, p_ext, u_ref, vn_ref, d_ref, cat_ref, wm_ref, *, tm):
    s = pl.program_id(1)

    @pl.when(jnp.logical_and(pl.program_id(0) == 0, s == 0))
    def _():
        for gi in range(N_GROUPS):
            wm_ref[gi] = _tril_bf16(sguw_ref[gi])

    @pl.when(s == 0)
    def _():
        p_ext[0:HIST_P, :] = jnp.zeros((HIST_P, W_MIX), F32)

    @pl.when(s > 0)
    def _():
        p_ext[0:HIST_P, :] = p_ext[tm:tm + HIST_P, :]

    hb_ref[...] = _rms(x_ref[...], g_ref[...]).astype(BF16)

    def seg(i):
        return _dot(hb_ref[...], win_ref[:, i * W_MIX:(i + 1) * W_MIX])

    p_ext[HIST_P:HIST_P + tm, :] = seg(0)
    u_ref[...] = seg(1)
    vn_ref[...] = _ln(seg(2), lng_ref[...], lnb_ref[...]).astype(BF16)

    def pool_body(c, carry):
        r0 = pl.multiple_of(c * ROW_CHUNK, ROW_CHUNK)
        win = p_ext.at[pl.ds(r0, ROW_CHUNK + HIST_P), :]
        pos = s * tm + r0 + lax.broadcasted_iota(jnp.int32, (ROW_CHUNK, GROUP_W), 0)
        for gi, w in enumerate(POOL_WINDOWS):
            lanes = slice(gi * GROUP_W, (gi + 1) * GROUP_W)
            tok = win[HIST_P:HIST_P + ROW_CHUNK, lanes]
            wsum = tok
            for j in range(1, w):
                wsum = wsum + win[HIST_P - j:HIST_P - j + ROW_CHUNK, lanes]
            cnt = jnp.minimum(pos + 1, w).astype(F32)
            d_ref[pl.ds(r0, ROW_CHUNK), lanes] = (wsum / cnt - tok).astype(BF16)
        return carry

    lax.fori_loop(0, tm // ROW_CHUNK, pool_body, 0)

    for gi in range(N_GROUPS):
        lanes = slice(gi * GROUP_W, (gi + 1) * GROUP_W)
        yc = _dot(d_ref[:, lanes], wpool_ref[gi]) * pscale_ref[:, lanes]
        cat_ref[:, lanes] = yc.astype(BF16)

    for c in range(tm // CHUNK):
        rows = slice(c * CHUNK, (c + 1) * CHUNK)
        for gi in range(N_GROUPS):
            lanes = slice(gi * GROUP_W, (gi + 1) * GROUP_W)
            mixed = _dot(wm_ref[gi], vn_ref[rows, lanes]) + sgubt_ref[:, gi:gi + 1]
            yd = u_ref[rows, lanes] * mixed
            cat_ref[rows, W_MIX + gi * GROUP_W:W_MIX + (gi + 1) * GROUP_W] = yd.astype(BF16)

    o_ref[...] = x_ref[...] + _dot(cat_ref[...], wout_ref[...])

    @pl.when(s == pl.num_programs(1) - 1)
    def _():
        sp_ref[...] = p_ext[HIST_P + tm - POOL_HIST:HIST_P + tm, :]


def _odd_prompt(x, g, win, wpool, pscale, sguw, sgub, lng, lnb, wout, tm):
    b, s, _ = x.shape
    tile = pl.BlockSpec((None, tm, D_MODEL), lambda i, j: (i, j, 0))
    n_in = win.shape[1]
    return pl.pallas_call(
        functools.partial(_odd_prompt_kernel, tm=tm),
        grid=(b, s // tm),
        in_specs=[tile, _const_spec((1, D_MODEL)), _const_spec((D_MODEL, n_in)),
                  _const_spec((N_GROUPS, GROUP_W, GROUP_W)), _const_spec((1, W_MIX)),
                  _const_spec((N_GROUPS, CHUNK, CHUNK)), _const_spec((CHUNK, N_GROUPS)),
                  _const_spec((1, W_MIX)), _const_spec((1, W_MIX)),
                  _const_spec((2 * W_MIX, D_MODEL))],
        out_specs=[tile, pl.BlockSpec((None, POOL_HIST, W_MIX), lambda i, j: (i, 0, 0))],
        out_shape=[jax.ShapeDtypeStruct((b, s, D_MODEL), F32),
                   jax.ShapeDtypeStruct((b, POOL_HIST, W_MIX), F32)],
        scratch_shapes=[pltpu.VMEM((tm, D_MODEL), BF16),
                        pltpu.VMEM((tm + HIST_P, W_MIX), F32),
                        pltpu.VMEM((tm, W_MIX), F32),
                        pltpu.VMEM((tm, W_MIX), BF16),
                        pltpu.VMEM((tm, W_MIX), BF16),
                        pltpu.VMEM((tm, 2 * W_MIX), BF16),
                        pltpu.VMEM((N_GROUPS, CHUNK, CHUNK), BF16)],
        compiler_params=_params(2),
        name="odd_prompt",
    )(x, g.reshape(1, D_MODEL), win, wpool, pscale.reshape(1, W_MIX), sguw, sgub.T,
      lng.reshape(1, W_MIX), lnb.reshape(1, W_MIX), wout)


def _even_sample_kernel(x_ref, g_ref, win_ref, ca_ref, cb_ref, cbb_ref, lng_ref, lnb_ref, wout_ref,
                        ha_ref, hbst_ref,
                        o_ref, sa_ref, sb_ref,
                        hb_ref, gate_ref, ua_ext, ub_ext, cbv_ref, cat_ref, *, t_len, n_seq):
    x = x_ref[...].reshape(t_len * n_seq, D_MODEL)
    hb_ref[...] = _rms(x, g_ref[...]).astype(BF16)

    def seg(i):
        return _dot(hb_ref[...], win_ref[:, i * W_MIX:(i + 1) * W_MIX])

    gate_ref[...] = seg(0)
    ua_ext[0:K_A - 1] = ha_ref[...]
    ub_ext[0:K_B - 1] = hbst_ref[...]
    ua_ext[K_A - 1:K_A - 1 + t_len] = (seg(1) * seg(2)).reshape(t_len, n_seq, W_MIX)
    ub_ext[K_B - 1:K_B - 1 + t_len] = (seg(3) * jax.nn.sigmoid(seg(4))).reshape(t_len, n_seq, W_MIX)

    def body(t, carry):
        rows = pl.ds(pl.multiple_of(t * n_seq, n_seq), n_seq)
        for gi in range(N_GROUPS):
            lanes = slice(gi * GROUP_W, (gi + 1) * GROUP_W)
            acc = jnp.zeros((n_seq, GROUP_W), F32)
            for k in range(K_A):
                acc = acc + ca_ref[k:k + 1, lanes] * ua_ext[t + k, :, lanes]
            cat_ref[rows, lanes] = (gate_ref[rows, lanes] * acc).astype(BF16)
            acc = jnp.zeros((n_seq, GROUP_W), F32)
            for k in range(K_B):
                acc = acc + cb_ref[k:k + 1, lanes] * ub_ext[t + k, :, lanes]
            cbv_ref[:, lanes] = acc + cbb_ref[:, lanes]
        yb = _silu(_ln(cbv_ref[...], lng_ref[...], lnb_ref[...]))
        cat_ref[rows, W_MIX:2 * W_MIX] = yb.astype(BF16)
        return carry

    lax.fori_loop(0, t_len, body, 0)

    o_ref[...] = (x + _dot(cat_ref[...], wout_ref[...])).reshape(t_len, n_seq, D_MODEL)
    sa_ref[...] = ua_ext[t_len:t_len + K_A - 1]
    sb_ref[...] = ub_ext[t_len:t_len + K_B - 1]


def _seq_block_spec(rows, nb, width):
    return pl.BlockSpec((rows, nb, width), lambda i, *_: (0, i, 0))


def _even_sample(x, g, win, ca, cb, cbb, lng, lnb, wout, hist_a, hist_b, nb):
    t_len, n_seq, _ = x.shape
    m = t_len * nb
    n_in = win.shape[1]
    return pl.pallas_call(
        functools.partial(_even_sample_kernel, t_len=t_len, n_seq=nb),
        grid=(n_seq // nb,),
        in_specs=[_seq_block_spec(t_len, nb, D_MODEL), _const_spec((1, D_MODEL)),
                  _const_spec((D_MODEL, n_in)),
                  _const_spec((K_A, W_MIX)), _const_spec((K_B, W_MIX)), _const_spec((1, W_MIX)),
                  _const_spec((1, W_MIX)), _const_spec((1, W_MIX)), _const_spec((2 * W_MIX, D_MODEL)),
                  _seq_block_spec(K_A - 1, nb, W_MIX), _seq_block_spec(K_B - 1, nb, W_MIX)],
        out_specs=[_seq_block_spec(t_len, nb, D_MODEL), _seq_block_spec(K_A - 1, nb, W_MIX),
                   _seq_block_spec(K_B - 1, nb, W_MIX)],
        out_shape=[jax.ShapeDtypeStruct((t_len, n_seq, D_MODEL), F32),
                   jax.ShapeDtypeStruct((K_A - 1, n_seq, W_MIX), F32),
                   jax.ShapeDtypeStruct((K_B - 1, n_seq, W_MIX), F32)],
        scratch_shapes=[pltpu.VMEM((m, D_MODEL), BF16),
                        pltpu.VMEM((m, W_MIX), F32),
                        pltpu.VMEM((t_len + K_A - 1, nb, W_MIX), F32),
                        pltpu.VMEM((t_len + K_B - 1, nb, W_MIX), F32),
                        pltpu.VMEM((nb, W_MIX), F32),
                        pltpu.VMEM((m, 2 * W_MIX), BF16)],
        compiler_params=_params(1),
        name="even_sample",
    )(x, g.reshape(1, D_MODEL), win, ca, cb, cbb.reshape(1, W_MIX), lng.reshape(1, W_MIX),
      lnb.reshape(1, W_MIX), wout, hist_a, hist_b)


def _odd_sample_kernel(sw_ref, sb_ref_smem,
                       x_ref, g_ref, win_ref, wpool_ref, pscale_ref, lng_ref, lnb_ref, wout_ref,
                       hp_ref,
                       o_ref, sp_ref, vn_out_ref,
                       hb_ref, p_ext, u_ref, d_ref, cat_ref, *, t_len, n_seq, start_pos):
    x = x_ref[...].reshape(t_len * n_seq, D_MODEL)
    hb_ref[...] = _rms(x, g_ref[...]).astype(BF16)

    def seg(i):
        return _dot(hb_ref[...], win_ref[:, i * W_MIX:(i + 1) * W_MIX])

    p_ext[0:POOL_HIST] = hp_ref[...]
    p_ext[POOL_HIST:POOL_HIST + t_len] = seg(0).reshape(t_len, n_seq, W_MIX)
    u_ref[...] = seg(1).reshape(t_len, n_seq, W_MIX)
    vn_out_ref[...] = _ln(seg(2), lng_ref[...], lnb_ref[...]).reshape(t_len, n_seq, W_MIX)

    for t in range(t_len):
        rows = slice(t * n_seq, (t + 1) * n_seq)
        for gi, w in enumerate(POOL_WINDOWS):
            lanes = slice(gi * GROUP_W, (gi + 1) * GROUP_W)
            tok = p_ext[POOL_HIST + t, :, lanes]
            wsum = tok
            for j in range(1, w):
                wsum = wsum + p_ext[POOL_HIST + t - j, :, lanes]
            cnt = float(min(start_pos + t + 1, w))
            d_ref[rows, lanes] = (wsum / cnt - tok).astype(BF16)
            mixed = jnp.zeros((n_seq, GROUP_W), F32) + sb_ref_smem[gi * t_len + t]
            for s in range(t + 1):
                mixed = mixed + sw_ref[(gi * t_len + t) * t_len + s] * vn_out_ref[s, :, lanes]
            cat_ref[rows, W_MIX + gi * GROUP_W:W_MIX + (gi + 1) * GROUP_W] = (
                u_ref[t, :, lanes] * mixed).astype(BF16)

    for gi in range(N_GROUPS):
        lanes = slice(gi * GROUP_W, (gi + 1) * GROUP_W)
        yc = _dot(d_ref[:, lanes], wpool_ref[gi]) * pscale_ref[:, lanes]
        cat_ref[:, lanes] = yc.astype(BF16)

    o_ref[...] = (x + _dot(cat_ref[...], wout_ref[...])).reshape(t_len, n_seq, D_MODEL)
    sp_ref[...] = p_ext[t_len:t_len + POOL_HIST]


def _odd_sample(x, g, win, wpool, pscale, sguw, sgub, lng, lnb, wout, hist_p, nb, start_pos):
    t_len, n_seq, _ = x.shape
    m = t_len * nb
    n_in = win.shape[1]
    assert t_len <= CHUNK
    sw = sguw[:, :t_len, :t_len].reshape(-1)
    sb = sgub[:, :t_len].reshape(-1)

    grid_spec = pltpu.PrefetchScalarGridSpec(
        num_scalar_prefetch=2,
        grid=(n_seq // nb,),
        in_specs=[_seq_block_spec(t_len, nb, D_MODEL), _const_spec((1, D_MODEL)),
                  _const_spec((D_MODEL, n_in)),
                  _const_spec((N_GROUPS, GROUP_W, GROUP_W)), _const_spec((1, W_MIX)),
                  _const_spec((1, W_MIX)), _const_spec((1, W_MIX)),
                  _const_spec((2 * W_MIX, D_MODEL)),
                  _seq_block_spec(POOL_HIST, nb, W_MIX)],
        out_specs=[_seq_block_spec(t_len, nb, D_MODEL), _seq_block_spec(POOL_HIST, nb, W_MIX),
                   _seq_block_spec(t_len, nb, W_MIX)],
        scratch_shapes=[pltpu.VMEM((m, D_MODEL), BF16),
                        pltpu.VMEM((t_len + POOL_HIST, nb, W_MIX), F32),
                        pltpu.VMEM((t_len, nb, W_MIX), F32),
                        pltpu.VMEM((m, W_MIX), BF16),
                        pltpu.VMEM((m, 2 * W_MIX), BF16)],
    )
    return pl.pallas_call(
        functools.partial(_odd_sample_kernel, t_len=t_len, n_seq=nb, start_pos=start_pos),
        grid_spec=grid_spec,
        out_shape=[jax.ShapeDtypeStruct((t_len, n_seq, D_MODEL), F32),
                   jax.ShapeDtypeStruct((POOL_HIST, n_seq, W_MIX), F32),
                   jax.ShapeDtypeStruct((t_len, n_seq, W_MIX), F32)],
        compiler_params=_params(1),
        name="odd_sample",
    )(sw, sb, x, g.reshape(1, D_MODEL), win, wpool, pscale.reshape(1, W_MIX),
      lng.reshape(1, W_MIX), lnb.reshape(1, W_MIX), wout, hist_p)


def _time_major(a):
    return jnp.swapaxes(a, 0, 1)


def kernel(x_prompt, x_sample, state_conv_a, state_conv_b, state_pool, norm_mix, norm_ffn, ev_w_in,
           ev_conv_a, ev_conv_b, ev_conv_b_bias, ev_ln_g, ev_ln_b, ev_w_out, od_w_in, od_pool_w,
           od_pool_scale, od_sgu_w, od_sgu_b, od_sgu_ln_g, od_sgu_ln_b, od_w_out, ffn_w1, ffn_w2,
           norm_final):
    depth = norm_mix.shape[0]
    batch, seq, _ = x_prompt.shape
    n_seq, t_len, _ = x_sample.shape
    tm_mix = 512
    tm_ffn = 512
    nb = 32

    xp = x_prompt
    xs = _time_major(x_sample)
    sa_p, sa_s, sb_p, sb_s, sc_p, sc_s, vn_s = [], [], [], [], [], [], []
    for l in range(depth):
        i = l // 2
        if l % 2 == 0:
            win, wout = ev_w_in[i].astype(BF16), ev_w_out[i].astype(BF16)
            common = (norm_mix[l], win, ev_conv_a[i], ev_conv_b[i], ev_conv_b_bias[i], ev_ln_g[i],
                      ev_ln_b[i], wout)
            xp, a_p, b_p = _even_prompt(xp, *common, tm=tm_mix)
            xs, a_s, b_s = _even_sample(xs, *common, _time_major(state_conv_a[i]),
                                        _time_major(state_conv_b[i]), nb)
            sa_p.append(a_p)
            sb_p.append(b_p)
            sa_s.append(_time_major(a_s))
            sb_s.append(_time_major(b_s))
        else:
            win, wout = od_w_in[i].astype(BF16), od_w_out[i].astype(BF16)
            common = (norm_mix[l], win, od_pool_w[i].astype(BF16), od_pool_scale[i], od_sgu_w[i],
                      od_sgu_b[i], od_sgu_ln_g[i], od_sgu_ln_b[i], wout)
            xp, c_p = _odd_prompt(xp, *common, tm=tm_mix)
            xs, c_s, v_s = _odd_sample(xs, *common, _time_major(state_pool[i]), nb, PAST_LEN)
            sc_p.append(c_p)
            sc_s.append(_time_major(c_s))
            vn_s.append(_time_major(v_s))
        w1, w2 = ffn_w1[l].astype(BF16), ffn_w2[l].astype(BF16)
        g_final = norm_final if l == depth - 1 else None
        xp = _ffn(xp.reshape(batch * seq, D_MODEL), norm_ffn[l], w1, w2, g_final, tm_ffn
                  ).reshape(batch, seq, D_MODEL)
        xs = _ffn(xs.reshape(t_len * n_seq, D_MODEL), norm_ffn[l], w1, w2, g_final, tm_ffn
                  ).reshape(t_len, n_seq, D_MODEL)

    y_sample = _time_major(xs)
    return (xp, y_sample, jnp.stack(sa_p), jnp.stack(sa_s), jnp.stack(sb_p), jnp.stack(sb_s),
            jnp.stack(sc_p), jnp.stack(sc_s), jnp.stack(vn_s))
```

```python
import functools

import jax
import jax.numpy as jnp
from jax import lax
from jax.experimental import pallas as pl
from jax.experimental.pallas import tpu as pltpu

F32 = jnp.float32
BF16 = jnp.bfloat16

D_MODEL = 1024
W_MIX = D_MODEL // 2
K_A = 3
K_B = 31
POOL_WINDOWS = (2, 4, 8, 16)
POOL_HIST = max(POOL_WINDOWS) - 1
CHUNK = 128
N_GROUPS = 4
GROUP_W = W_MIX // N_GROUPS
D_FF = 4 * D_MODEL
PAST_LEN = 16384
EPS = 1e-6

SUBLANES = 8
HIST_A = 8
HIST_B = 32
HIST_P = 16
ROW_CHUNK = 64
CONV_ROWS = 128
MIX_BLOCK_ROWS = 256
FFN_COL_CHUNK = 1024
VMEM_LIMIT = 56 * 1024 * 1024


def _rms(x, g):
    return x * lax.rsqrt(jnp.mean(x * x, axis=-1, keepdims=True) + EPS) * g


def _ln(x, g, b):
    mu = jnp.mean(x, axis=-1, keepdims=True)
    xc = x - mu
    return xc * lax.rsqrt(jnp.mean(xc * xc, axis=-1, keepdims=True) + EPS) * g + b


def _dot(a, b):
    return jnp.dot(a, b, preferred_element_type=F32)


def _silu(x):
    return x * jax.nn.sigmoid(x)


def _const_spec(shape):
    nd = len(shape)
    return pl.BlockSpec(shape, lambda *_: (0,) * nd, pipeline_mode=pl.Buffered(1))


def _params(n_grid_axes, flags=None):
    return pltpu.CompilerParams(
        dimension_semantics=("arbitrary",) * n_grid_axes,
        vmem_limit_bytes=VMEM_LIMIT,
        flags=flags,
    )


def _ffn_kernel(*refs, final):
    if final:
        x_ref, g_ref, w1_ref, w2_ref, gf_ref, o_ref, hb_ref, act_ref = refs
    else:
        x_ref, g_ref, w1_ref, w2_ref, o_ref, hb_ref, act_ref = refs
    hb_ref[...] = _rms(x_ref[...], g_ref[...]).astype(BF16)
    for c in range(D_FF // FFN_COL_CHUNK):
        cols = slice(c * FFN_COL_CHUNK, (c + 1) * FFN_COL_CHUNK)
        a = _dot(hb_ref[...], w1_ref[:, cols])
        act_ref[:, cols] = jnp.square(jnp.maximum(a, 0.0)).astype(BF16)
    y = x_ref[...] + _dot(act_ref[...], w2_ref[...])
    if final:
        y = _rms(y, gf_ref[...])
    o_ref[...] = y


def _ffn(x2d, g, w1, w2, g_final, tm):
    m = x2d.shape[0]
    final = g_final is not None
    row_spec = pl.BlockSpec((tm, D_MODEL), lambda i: (i, 0))
    in_specs = [row_spec, _const_spec((1, D_MODEL)), _const_spec((D_MODEL, D_FF)),
                _const_spec((D_FF, D_MODEL))]
    args = [x2d, g.reshape(1, D_MODEL), w1, w2]
    if final:
        in_specs.append(_const_spec((1, D_MODEL)))
        args.append(g_final.reshape(1, D_MODEL))
    return pl.pallas_call(
        functools.partial(_ffn_kernel, final=final),
        grid=(m // tm,),
        in_specs=in_specs,
        out_specs=row_spec,
        out_shape=jax.ShapeDtypeStruct((m, D_MODEL), F32),
        scratch_shapes=[pltpu.VMEM((tm, D_MODEL), BF16), pltpu.VMEM((tm, D_FF), BF16)],
        compiler_params=_params(1),
        name="ffn_final" if final else "ffn",
    )(*args)


def _causal_dwconv(win, w_ref, n_taps, hist, rows, lanes):
    out = None
    for r in range(SUBLANES):
        acc = None
        for q in range(hist // SUBLANES):
            k = n_taps - 1 - SUBLANES * q - r
            if k < 0:
                continue
            start = hist - SUBLANES * (q + 1)
            term = w_ref[k:k + 1, lanes] * win[start:start + rows + SUBLANES, lanes]
            acc = term if acc is None else acc + term
        if acc is None:
            continue
        part = acc[SUBLANES - r:SUBLANES - r + rows]
        out = part if out is None else out + part
    return out


def _even_prompt_kernel(x_ref, g_ref, win_ref, ca_ref, cb_ref, cbb_ref, lng_ref, lnb_ref, wout_ref,
                        o_ref, sa_ref, sb_ref,
                        hb_ref, z_ref, ua_ext, ub_ext, cbv_ref, cat_ref, *, tm):
    s = pl.program_id(1)

    @pl.when(s == 0)
    def _():
        ua_ext[0:HIST_A, :] = jnp.zeros((HIST_A, W_MIX), F32)
        ub_ext[0:HIST_B, :] = jnp.zeros((HIST_B, W_MIX), F32)

    @pl.when(s > 0)
    def _():
        ua_ext[0:HIST_A, :] = ua_ext[tm:tm + HIST_A, :]
        ub_ext[0:HIST_B, :] = ub_ext[tm:tm + HIST_B, :]

    def project(rows):
        hb_ref[rows, :] = _rms(x_ref[rows, :], g_ref[...]).astype(BF16)
        z_ref[rows, :] = _dot(hb_ref[rows, :], win_ref[...])

    def zseg(rows, i, lanes=slice(0, W_MIX)):
        return z_ref[rows, i * W_MIX + lanes.start:i * W_MIX + lanes.stop]

    def mix_and_project_out(rows):
        for r0 in range(rows.start, rows.stop, ROW_CHUNK):
            crow = slice(r0, r0 + ROW_CHUNK)
            ua_ext[HIST_A + r0:HIST_A + r0 + ROW_CHUNK, :] = zseg(crow, 1) * zseg(crow, 2)
            ub_ext[HIST_B + r0:HIST_B + r0 + ROW_CHUNK, :] = zseg(crow, 3) * jax.nn.sigmoid(zseg(crow, 4))
        for r0 in range(rows.start, rows.stop, CONV_ROWS):
            crow = slice(r0, r0 + CONV_ROWS)
            win_a = ua_ext.at[r0:r0 + CONV_ROWS + HIST_A, :]
            win_b = ub_ext.at[r0:r0 + CONV_ROWS + HIST_B, :]
            for gi in range(N_GROUPS):
                lanes = slice(gi * GROUP_W, (gi + 1) * GROUP_W)
                ya = zseg(crow, 0, lanes) * _causal_dwconv(win_a, ca_ref, K_A, HIST_A, CONV_ROWS, lanes)
                cat_ref[crow, lanes] = ya.astype(BF16)
                cbv_ref[crow, lanes] = _causal_dwconv(win_b, cb_ref, K_B, HIST_B, CONV_ROWS, lanes)
            yb = _silu(_ln(cbv_ref[crow, :] + cbb_ref[...], lng_ref[...], lnb_ref[...]))
            cat_ref[crow, W_MIX:2 * W_MIX] = yb.astype(BF16)
        o_ref[rows, :] = x_ref[rows, :] + _dot(cat_ref[rows, :], wout_ref[...])

    blocks = [slice(r, r + MIX_BLOCK_ROWS) for r in range(0, tm, MIX_BLOCK_ROWS)]
    for i, rows in enumerate(blocks):
        project(rows)
        if i > 0:
            mix_and_project_out(blocks[i - 1])
    mix_and_project_out(blocks[-1])

    @pl.when(s == pl.num_programs(1) - 1)
    def _():
        sa_ref[...] = ua_ext[HIST_A + tm - (K_A - 1):HIST_A + tm, :]
        sb_ref[...] = ub_ext[HIST_B + tm - (K_B - 1):HIST_B + tm, :]


def _even_prompt(x, g, win, ca, cb, cbb, lng, lnb, wout, tm):
    b, s, _ = x.shape
    tile = pl.BlockSpec((None, tm, D_MODEL), lambda i, j: (i, j, 0))
    n_in = win.shape[1]
    return pl.pallas_call(
        functools.partial(_even_prompt_kernel, tm=tm),
        grid=(b, s // tm),
        in_specs=[tile, _const_spec((1, D_MODEL)), _const_spec((D_MODEL, n_in)),
                  _const_spec((K_A, W_MIX)), _const_spec((K_B, W_MIX)), _const_spec((1, W_MIX)),
                  _const_spec((1, W_MIX)), _const_spec((1, W_MIX)), _const_spec((2 * W_MIX, D_MODEL))],
        out_specs=[tile,
                   pl.BlockSpec((None, K_A - 1, W_MIX), lambda i, j: (i, 0, 0)),
                   pl.BlockSpec((None, K_B - 1, W_MIX), lambda i, j: (i, 0, 0))],
        out_shape=[jax.ShapeDtypeStruct((b, s, D_MODEL), F32),
                   jax.ShapeDtypeStruct((b, K_A - 1, W_MIX), F32),
                   jax.ShapeDtypeStruct((b, K_B - 1, W_MIX), F32)],
        scratch_shapes=[pltpu.VMEM((tm, D_MODEL), BF16),
                        pltpu.VMEM((tm, n_in), F32),
                        pltpu.VMEM((tm + HIST_A, W_MIX), F32),
                        pltpu.VMEM((tm + HIST_B, W_MIX), F32),
                        pltpu.VMEM((tm, W_MIX), F32),
                        pltpu.VMEM((tm, 2 * W_MIX), BF16)],
        compiler_params=_params(2),
        name="even_prompt",
    )(x, g.reshape(1, D_MODEL), win, ca, cb, cbb.reshape(1, W_MIX), lng.reshape(1, W_MIX),
      lnb.reshape(1, W_MIX), wout)


def _tril_bf16(w):
    t = lax.broadcasted_iota(jnp.int32, (CHUNK, CHUNK), 0)
    s = lax.broadcasted_iota(jnp.int32, (CHUNK, CHUNK), 1)
    return jnp.where(s <= t, w, 0.0).astype(BF16)


def _odd_prompt_kernel(x_ref, g_ref, win_ref, wpool_ref, pscale_ref, sguw_ref, sgubt_ref, lng_ref,
                       lnb_ref, wout_ref,
                       o_ref, sp_ref,
                       hb_ref, p_ext, u_ref, vn_ref, d_ref, cat_ref, wm_ref, *, tm):
    s = pl.program_id(1)

    @pl.when(jnp.logical_and(pl.program_id(0) == 0, s == 0))
    def _():
        for gi in range(N_GROUPS):
            wm_ref[gi] = _tril_bf16(sguw_ref[gi])

    @pl.when(s == 0)
    def _():
        p_ext[0:HIST_P, :] = jnp.zeros((HIST_P, W_MIX), F32)

    @pl.when(s > 0)
    def _():
        p_ext[0:HIST_P, :] = p_ext[tm:tm + HIST_P, :]

    hb_ref[...] = _rms(x_ref[...], g_ref[...]).astype(BF16)

    def seg(i):
        return _dot(hb_ref[...], win_ref[:, i * W_MIX:(i + 1) * W_MIX])

    p_ext[HIST_P:HIST_P + tm, :] = seg(0)
    u_ref[...] = seg(1)
    vn_ref[...] = _ln(seg(2), lng_ref[...], lnb_ref[...]).astype(BF16)

    def pool_body(c, carry):
        r0 = pl.multiple_of(c * ROW_CHUNK, ROW_CHUNK)
        win = p_ext.at[pl.ds(r0, ROW_CHUNK + HIST_P), :]
        pos = s * tm + r0 + lax.broadcasted_iota(jnp.int32, (ROW_CHUNK, GROUP_W), 0)
        for gi, w in enumerate(POOL_WINDOWS):
            lanes = slice(gi * GROUP_W, (gi + 1) * GROUP_W)
            tok = win[HIST_P:HIST_P + ROW_CHUNK, lanes]
            wsum = tok
            for j in range(1, w):
                wsum = wsum + win[HIST_P - j:HIST_P - j + ROW_CHUNK, lanes]
            cnt = jnp.minimum(pos + 1, w).astype(F32)
            d_ref[pl.ds(r0, ROW_CHUNK), lanes] = (wsum / cnt - tok).astype(BF16)
        return carry

    lax.fori_loop(0, tm // ROW_CHUNK, pool_body, 0)

    for gi in range(N_GROUPS):
        lanes = slice(gi * GROUP_W, (gi + 1) * GROUP_W)
        yc = _dot(d_ref[:, lanes], wpool_ref[gi]) * pscale_ref[:, lanes]
        cat_ref[:, lanes] = yc.astype(BF16)

    for c in range(tm // CHUNK):
        rows = slice(c * CHUNK, (c + 1) * CHUNK)
        for gi in range(N_GROUPS):
            lanes = slice(gi * GROUP_W, (gi + 1) * GROUP_W)
            mixed = _dot(wm_ref[gi], vn_ref[rows, lanes]) + sgubt_ref[:, gi:gi + 1]
            yd = u_ref[rows, lanes] * mixed
            cat_ref[rows, W_MIX + gi * GROUP_W:W_MIX + (gi + 1) * GROUP_W] = yd.astype(BF16)

    o_ref[...] = x_ref[...] + _dot(cat_ref[...], wout_ref[...])

    @pl.when(s == pl.num_programs(1) - 1)
    def _():
        sp_ref[...] = p_ext[HIST_P + tm - POOL_HIST:HIST_P + tm, :]


def _odd_prompt(x, g, win, wpool, pscale, sguw, sgub, lng, lnb, wout, tm):
    b, s, _ = x.shape
    tile = pl.BlockSpec((None, tm, D_MODEL), lambda i, j: (i, j, 0))
    n_in = win.shape[1]
    return pl.pallas_call(
        functools.partial(_odd_prompt_kernel, tm=tm),
        grid=(b, s // tm),
        in_specs=[tile, _const_spec((1, D_MODEL)), _const_spec((D_MODEL, n_in)),
                  _const_spec((N_GROUPS, GROUP_W, GROUP_W)), _const_spec((1, W_MIX)),
                  _const_spec((N_GROUPS, CHUNK, CHUNK)), _const_spec((CHUNK, N_GROUPS)),
                  _const_spec((1, W_MIX)), _const_spec((1, W_MIX)),
                  _const_spec((2 * W_MIX, D_MODEL))],
        out_specs=[tile, pl.BlockSpec((None, POOL_HIST, W_MIX), lambda i, j: (i, 0, 0))],
        out_shape=[jax.ShapeDtypeStruct((b, s, D_MODEL), F32),
                   jax.ShapeDtypeStruct((b, POOL_HIST, W_MIX), F32)],
        scratch_shapes=[pltpu.VMEM((tm, D_MODEL), BF16),
                        pltpu.VMEM((tm + HIST_P, W_MIX), F32),
                        pltpu.VMEM((tm, W_MIX), F32),
                        pltpu.VMEM((tm, W_MIX), BF16),
                        pltpu.VMEM((tm, W_MIX), BF16),
                        pltpu.VMEM((tm, 2 * W_MIX), BF16),
                        pltpu.VMEM((N_GROUPS, CHUNK, CHUNK), BF16)],
        compiler_params=_params(2),
        name="odd_prompt",
    )(x, g.reshape(1, D_MODEL), win, wpool, pscale.reshape(1, W_MIX), sguw, sgub.T,
      lng.reshape(1, W_MIX), lnb.reshape(1, W_MIX), wout)


def _even_sample_kernel(x_ref, g_ref, win_ref, ca_ref, cb_ref, cbb_ref, lng_ref, lnb_ref, wout_ref,
                        ha_ref, hbst_ref,
                        o_ref, sa_ref, sb_ref,
                        hb_ref, gate_ref, ua_ext, ub_ext, cbv_ref, cat_ref, *, t_len, n_seq):
    x = x_ref[...].reshape(t_len * n_seq, D_MODEL)
    hb_ref[...] = _rms(x, g_ref[...]).astype(BF16)

    def seg(i):
        return _dot(hb_ref[...], win_ref[:, i * W_MIX:(i + 1) * W_MIX])

    gate_ref[...] = seg(0)
    ua_ext[0:K_A - 1] = ha_ref[...]
    ub_ext[0:K_B - 1] = hbst_ref[...]
    ua_ext[K_A - 1:K_A - 1 + t_len] = (seg(1) * seg(2)).reshape(t_len, n_seq, W_MIX)
    ub_ext[K_B - 1:K_B - 1 + t_len] = (seg(3) * jax.nn.sigmoid(seg(4))).reshape(t_len, n_seq, W_MIX)

    def body(t, carry):
        rows = pl.ds(pl.multiple_of(t * n_seq, n_seq), n_seq)
        for gi in range(N_GROUPS):
            lanes = slice(gi * GROUP_W, (gi + 1) * GROUP_W)
            acc = jnp.zeros((n_seq, GROUP_W), F32)
            for k in range(K_A):
                acc = acc + ca_ref[k:k + 1, lanes] * ua_ext[t + k, :, lanes]
            cat_ref[rows, lanes] = (gate_ref[rows, lanes] * acc).astype(BF16)
            acc = jnp.zeros((n_seq, GROUP_W), F32)
            for k in range(K_B):
                acc = acc + cb_ref[k:k + 1, lanes] * ub_ext[t + k, :, lanes]
            cbv_ref[:, lanes] = acc + cbb_ref[:, lanes]
        yb = _silu(_ln(cbv_ref[...], lng_ref[...], lnb_ref[...]))
        cat_ref[rows, W_MIX:2 * W_MIX] = yb.astype(BF16)
        return carry

    lax.fori_loop(0, t_len, body, 0)

    o_ref[...] = (x + _dot(cat_ref[...], wout_ref[...])).reshape(t_len, n_seq, D_MODEL)
    sa_ref[...] = ua_ext[t_len:t_len + K_A - 1]
    sb_ref[...] = ub_ext[t_len:t_len + K_B - 1]


def _seq_block_spec(rows, nb, width):
    return pl.BlockSpec((rows, nb, width), lambda i, *_: (0, i, 0))


def _even_sample(x, g, win, ca, cb, cbb, lng, lnb, wout, hist_a, hist_b, nb):
    t_len, n_seq, _ = x.shape
    m = t_len * nb
    n_in = win.shape[1]
    return pl.pallas_call(
        functools.partial(_even_sample_kernel, t_len=t_len, n_seq=nb),
        grid=(n_seq // nb,),
        in_specs=[_seq_block_spec(t_len, nb, D_MODEL), _const_spec((1, D_MODEL)),
                  _const_spec((D_MODEL, n_in)),
                  _const_spec((K_A, W_MIX)), _const_spec((K_B, W_MIX)), _const_spec((1, W_MIX)),
                  _const_spec((1, W_MIX)), _const_spec((1, W_MIX)), _const_spec((2 * W_MIX, D_MODEL)),
                  _seq_block_spec(K_A - 1, nb, W_MIX), _seq_block_spec(K_B - 1, nb, W_MIX)],
        out_specs=[_seq_block_spec(t_len, nb, D_MODEL), _seq_block_spec(K_A - 1, nb, W_MIX),
                   _seq_block_spec(K_B - 1, nb, W_MIX)],
        out_shape=[jax.ShapeDtypeStruct((t_len, n_seq, D_MODEL), F32),
                   jax.ShapeDtypeStruct((K_A - 1, n_seq, W_MIX), F32),
                   jax.ShapeDtypeStruct((K_B - 1, n_seq, W_MIX), F32)],
        scratch_shapes=[pltpu.VMEM((m, D_MODEL), BF16),
                        pltpu.VMEM((m, W_MIX), F32),
                        pltpu.VMEM((t_len + K_A - 1, nb, W_MIX), F32),
                        pltpu.VMEM((t_len + K_B - 1, nb, W_MIX), F32),
                        pltpu.VMEM((nb, W_MIX), F32),
                        pltpu.VMEM((m, 2 * W_MIX), BF16)],
        compiler_params=_params(1),
        name="even_sample",
    )(x, g.reshape(1, D_MODEL), win, ca, cb, cbb.reshape(1, W_MIX), lng.reshape(1, W_MIX),
      lnb.reshape(1, W_MIX), wout, hist_a, hist_b)


def _odd_sample_kernel(sw_ref, sb_ref_smem,
                       x_ref, g_ref, win_ref, wpool_ref, pscale_ref, lng_ref, lnb_ref, wout_ref,
                       hp_ref,
                       o_ref, sp_ref, vn_out_ref,
                       hb_ref, p_ext, u_ref, d_ref, cat_ref, *, t_len, n_seq, start_pos):
    x = x_ref[...].reshape(t_len * n_seq, D_MODEL)
    hb_ref[...] = _rms(x, g_ref[...]).astype(BF16)

    def seg(i):
        return _dot(hb_ref[...], win_ref[:, i * W_MIX:(i + 1) * W_MIX])

    p_ext[0:POOL_HIST] = hp_ref[...]
    p_ext[POOL_HIST:POOL_HIST + t_len] = seg(0).reshape(t_len, n_seq, W_MIX)
    u_ref[...] = seg(1).reshape(t_len, n_seq, W_MIX)
    vn_out_ref[...] = _ln(seg(2), lng_ref[...], lnb_ref[...]).reshape(t_len, n_seq, W_MIX)

    for t in range(t_len):
        rows = slice(t * n_seq, (t + 1) * n_seq)
        for gi, w in enumerate(POOL_WINDOWS):
            lanes = slice(gi * GROUP_W, (gi + 1) * GROUP_W)
            tok = p_ext[POOL_HIST + t, :, lanes]
            wsum = tok
            for j in range(1, w):
                wsum = wsum + p_ext[POOL_HIST + t - j, :, lanes]
            cnt = float(min(start_pos + t + 1, w))
            d_ref[rows, lanes] = (wsum / cnt - tok).astype(BF16)
            mixed = jnp.zeros((n_seq, GROUP_W), F32) + sb_ref_smem[gi * t_len + t]
            for s in range(t + 1):
                mixed = mixed + sw_ref[(gi * t_len + t) * t_len + s] * vn_out_ref[s, :, lanes]
            cat_ref[rows, W_MIX + gi * GROUP_W:W_MIX + (gi + 1) * GROUP_W] = (
                u_ref[t, :, lanes] * mixed).astype(BF16)

    for gi in range(N_GROUPS):
        lanes = slice(gi * GROUP_W, (gi + 1) * GROUP_W)
        yc = _dot(d_ref[:, lanes], wpool_ref[gi]) * pscale_ref[:, lanes]
        cat_ref[:, lanes] = yc.astype(BF16)

    o_ref[...] = (x + _dot(cat_ref[...], wout_ref[...])).reshape(t_len, n_seq, D_MODEL)
    sp_ref[...] = p_ext[t_len:t_len + POOL_HIST]


def _odd_sample(x, g, win, wpool, pscale, sguw, sgub, lng, lnb, wout, hist_p, nb, start_pos):
    t_len, n_seq, _ = x.shape
    m = t_len * nb
    n_in = win.shape[1]
    assert t_len <= CHUNK
    sw = sguw[:, :t_len, :t_len].reshape(-1)
    sb = sgub[:, :t_len].reshape(-1)

    grid_spec = pltpu.PrefetchScalarGridSpec(
        num_scalar_prefetch=2,
        grid=(n_seq // nb,),
        in_specs=[_seq_block_spec(t_len, nb, D_MODEL), _const_spec((1, D_MODEL)),
                  _const_spec((D_MODEL, n_in)),
                  _const_spec((N_GROUPS, GROUP_W, GROUP_W)), _const_spec((1, W_MIX)),
                  _const_spec((1, W_MIX)), _const_spec((1, W_MIX)),
                  _const_spec((2 * W_MIX, D_MODEL)),
                  _seq_block_spec(POOL_HIST, nb, W_MIX)],
        out_specs=[_seq_block_spec(t_len, nb, D_MODEL), _seq_block_spec(POOL_HIST, nb, W_MIX),
                   _seq_block_spec(t_len, nb, W_MIX)],
        scratch_shapes=[pltpu.VMEM((m, D_MODEL), BF16),
                        pltpu.VMEM((t_len + POOL_HIST, nb, W_MIX), F32),
                        pltpu.VMEM((t_len, nb, W_MIX), F32),
                        pltpu.VMEM((m, W_MIX), BF16),
                        pltpu.VMEM((m, 2 * W_MIX), BF16)],
    )
    return pl.pallas_call(
        functools.partial(_odd_sample_kernel, t_len=t_len, n_seq=nb, start_pos=start_pos),
        grid_spec=grid_spec,
        out_shape=[jax.ShapeDtypeStruct((t_len, n_seq, D_MODEL), F32),
                   jax.ShapeDtypeStruct((POOL_HIST, n_seq, W_MIX), F32),
                   jax.ShapeDtypeStruct((t_len, n_seq, W_MIX), F32)],
        compiler_params=_params(1),
        name="odd_sample",
    )(sw, sb, x, g.reshape(1, D_MODEL), win, wpool, pscale.reshape(1, W_MIX),
      lng.reshape(1, W_MIX), lnb.reshape(1, W_MIX), wout, hist_p)


def _time_major(a):
    return jnp.swapaxes(a, 0, 1)


def kernel(x_prompt, x_sample, state_conv_a, state_conv_b, state_pool, norm_mix, norm_ffn, ev_w_in,
           ev_conv_a, ev_conv_b, ev_conv_b_bias, ev_ln_g, ev_ln_b, ev_w_out, od_w_in, od_pool_w,
           od_pool_scale, od_sgu_w, od_sgu_b, od_sgu_ln_g, od_sgu_ln_b, od_w_out, ffn_w1, ffn_w2,
           norm_final):
    depth = norm_mix.shape[0]
    batch, seq, _ = x_prompt.shape
    n_seq, t_len, _ = x_sample.shape
    tm_mix = 512
    tm_ffn = 512
    nb = 32

    xp = x_prompt
    xs = _time_major(x_sample)
    sa_p, sa_s, sb_p, sb_s, sc_p, sc_s, vn_s = [], [], [], [], [], [], []
    for l in range(depth):
        i = l // 2
        if l % 2 == 0:
            win, wout = ev_w_in[i].astype(BF16), ev_w_out[i].astype(BF16)
            common = (norm_mix[l], win, ev_conv_a[i], ev_conv_b[i], ev_conv_b_bias[i], ev_ln_g[i],
                      ev_ln_b[i], wout)
            xp, a_p, b_p = _even_prompt(xp, *common, tm=tm_mix)
            xs, a_s, b_s = _even_sample(xs, *common, _time_major(state_conv_a[i]),
                                        _time_major(state_conv_b[i]), nb)
            sa_p.append(a_p)
            sb_p.append(b_p)
            sa_s.append(_time_major(a_s))
            sb_s.append(_time_major(b_s))
        else:
            win, wout = od_w_in[i].astype(BF16), od_w_out[i].astype(BF16)
            common = (norm_mix[l], win, od_pool_w[i].astype(BF16), od_pool_scale[i], od_sgu_w[i],
                      od_sgu_b[i], od_sgu_ln_g[i], od_sgu_ln_b[i], wout)
            xp, c_p = _odd_prompt(xp, *common, tm=tm_mix)
            xs, c_s, v_s = _odd_sample(xs, *common, _time_major(state_pool[i]), nb, PAST_LEN)
            sc_p.append(c_p)
            sc_s.append(_time_major(c_s))
            vn_s.append(_time_major(v_s))
        w1, w2 = ffn_w1[l].astype(BF16), ffn_w2[l].astype(BF16)
        g_final = norm_final if l == depth - 1 else None
        xp = _ffn(xp.reshape(batch * seq, D_MODEL), norm_ffn[l], w1, w2, g_final, tm_ffn
                  ).reshape(batch, seq, D_MODEL)
        xs = _ffn(xs.reshape(t_len * n_seq, D_MODEL), norm_ffn[l], w1, w2, g_final, tm_ffn
                  ).reshape(t_len, n_seq, D_MODEL)

    y_sample = _time_major(xs)
    return (xp, y_sample, jnp.stack(sa_p), jnp.stack(sa_s), jnp.stack(sb_p), jnp.stack(sb_s),
            jnp.stack(sc_p), jnp.stack(sc_s), jnp.stack(vn_s))
```

```python
import functools

import jax
import jax.numpy as jnp
from jax import lax
from jax.experimental import pallas as pl
from jax.experimental.pallas import tpu as pltpu

F32 = jnp.float32
BF16 = jnp.bfloat16

D_MODEL = 1024
W_MIX = D_MODEL // 2
K_A = 3
K_B = 31
POOL_WINDOWS = (2, 4, 8, 16)
POOL_HIST = max(POOL_WINDOWS) - 1
CHUNK = 128
N_GROUPS = 4
GROUP_W = W_MIX // N_GROUPS
D_FF = 4 * D_MODEL
PAST_LEN = 16384
EPS = 1e-6

SUBLANES = 8
HIST_A = 8
HIST_B = 32
HIST_P = 16
ROW_CHUNK = 64
CONV_ROWS = 128
MIX_BLOCK_ROWS = 256
FFN_COL_CHUNK = 1024
VMEM_LIMIT = 56 * 1024 * 1024


def _rms(x, g):
    return x * lax.rsqrt(jnp.mean(x * x, axis=-1, keepdims=True) + EPS) * g


def _ln(x, g, b):
    mu = jnp.mean(x, axis=-1, keepdims=True)
    xc = x - mu
    return xc * lax.rsqrt(jnp.mean(xc * xc, axis=-1, keepdims=True) + EPS) * g + b


def _dot(a, b):
    return jnp.dot(a, b, preferred_element_type=F32)


def _silu(x):
    return x * jax.nn.sigmoid(x)


def _const_spec(shape):
    nd = len(shape)
    return pl.BlockSpec(shape, lambda *_: (0,) * nd, pipeline_mode=pl.Buffered(1))


def _params(n_grid_axes, flags=None):
    return pltpu.CompilerParams(
        dimension_semantics=("arbitrary",) * n_grid_axes,
        vmem_limit_bytes=VMEM_LIMIT,
        flags=flags,
    )


def _cast_plumbing(arrays, n_steps, step_of):
    specs, shapes = [], []
    for a in arrays:
        rows, rem = divmod(a.shape[0], n_steps)
        assert rem == 0 and rows % (2 * SUBLANES) == 0, a.shape
        nd = a.ndim
        specs.append(pl.BlockSpec((rows,) + a.shape[1:],
                                  lambda *idx, _nd=nd: (step_of(*idx),) + (0,) * (_nd - 1)))
        shapes.append(jax.ShapeDtypeStruct(a.shape, BF16))
    return specs, shapes


def _cast_rows(src_refs, dst_refs):
    for src, dst in zip(src_refs, dst_refs, strict=True):
        dst[...] = src[...].astype(BF16)


def _ffn_kernel(*refs, final, n_prompt, n_cast):
    n_in = 6 if final else 5
    xp_ref, xs_ref, g_ref, w1_ref, w2_ref = refs[:5]
    cast_in = refs[n_in:n_in + n_cast]
    op_ref, os_ref = refs[n_in + n_cast:n_in + n_cast + 2]
    cast_out = refs[n_in + n_cast + 2:n_in + 2 * n_cast + 2]
    hb_ref, act_ref = refs[n_in + 2 * n_cast + 2:]
    step = pl.program_id(0)

    def rows_block(x_ref, o_ref):
        hb_ref[...] = _rms(x_ref[...], g_ref[...]).astype(BF16)
        for c in range(D_FF // FFN_COL_CHUNK):
            cols = slice(c * FFN_COL_CHUNK, (c + 1) * FFN_COL_CHUNK)
            a = _dot(hb_ref[...], w1_ref[:, cols])
            act_ref[:, cols] = jnp.square(jnp.maximum(a, 0.0)).astype(BF16)
        y = x_ref[...] + _dot(act_ref[...], w2_ref[...])
        if final:
            y = _rms(y, refs[5][...])
        o_ref[...] = y

    @pl.when(step < n_prompt)
    def _():
        rows_block(xp_ref, op_ref)
        _cast_rows(cast_in, cast_out)

    @pl.when(step >= n_prompt)
    def _():
        rows_block(xs_ref, os_ref)


def _ffn(xp2d, xs2d, g, w1, w2, g_final, tm, cast_next=()):
    n_prompt, n_sample = xp2d.shape[0] // tm, xs2d.shape[0] // tm
    final = g_final is not None
    last = n_prompt - 1
    p_spec = pl.BlockSpec((tm, D_MODEL), lambda i: (jnp.minimum(i, last), 0))
    s_spec = pl.BlockSpec((tm, D_MODEL), lambda i: (jnp.maximum(i - n_prompt, 0), 0))
    in_specs = [p_spec, s_spec, _const_spec((1, D_MODEL)), _const_spec((D_MODEL, D_FF)),
                _const_spec((D_FF, D_MODEL))]
    args = [xp2d, xs2d, g.reshape(1, D_MODEL), w1, w2]
    if final:
        in_specs.append(_const_spec((1, D_MODEL)))
        args.append(g_final.reshape(1, D_MODEL))
    c_specs, c_shapes = _cast_plumbing(cast_next, n_prompt, lambda i: jnp.minimum(i, last))
    return pl.pallas_call(
        functools.partial(_ffn_kernel, final=final, n_prompt=n_prompt, n_cast=len(cast_next)),
        grid=(n_prompt + n_sample,),
        in_specs=in_specs + c_specs,
        out_specs=[p_spec, s_spec] + c_specs,
        out_shape=[jax.ShapeDtypeStruct(xp2d.shape, F32), jax.ShapeDtypeStruct(xs2d.shape, F32)] + c_shapes,
        scratch_shapes=[pltpu.VMEM((tm, D_MODEL), BF16), pltpu.VMEM((tm, D_FF), BF16)],
        compiler_params=_params(1),
        name="ffn_final" if final else "ffn",
    )(*args, *cast_next)


def _causal_dwconv(win, w_ref, n_taps, hist, rows, lanes):
    out = None
    for r in range(SUBLANES):
        acc = None
        for q in range(hist // SUBLANES):
            k = n_taps - 1 - SUBLANES * q - r
            if k < 0:
                continue
            start = hist - SUBLANES * (q + 1)
            term = w_ref[k:k + 1, lanes] * win[start:start + rows + SUBLANES, lanes]
            acc = term if acc is None else acc + term
        if acc is None:
            continue
        part = acc[SUBLANES - r:SUBLANES - r + rows]
        out = part if out is None else out + part
    return out


def _even_prompt_kernel(*refs, tm, n_cast):
    x_ref, g_ref, win_ref, ca_ref, cb_ref, cbb_ref, lng_ref, lnb_ref, wout_ref = refs[:9]
    cast_in = refs[9:9 + n_cast]
    o_ref, sa_ref, sb_ref = refs[9 + n_cast:12 + n_cast]
    cast_out = refs[12 + n_cast:12 + 2 * n_cast]
    hb_ref, z_ref, ua_ext, ub_ext, cbv_ref, cat_ref = refs[12 + 2 * n_cast:]
    s = pl.program_id(1)
    _cast_rows(cast_in, cast_out)

    @pl.when(s == 0)
    def _():
        ua_ext[0:HIST_A, :] = jnp.zeros((HIST_A, W_MIX), F32)
        ub_ext[0:HIST_B, :] = jnp.zeros((HIST_B, W_MIX), F32)

    @pl.when(s > 0)
    def _():
        ua_ext[0:HIST_A, :] = ua_ext[tm:tm + HIST_A, :]
        ub_ext[0:HIST_B, :] = ub_ext[tm:tm + HIST_B, :]

    def zseg(rows, i, lanes=slice(0, W_MIX)):
        return z_ref[rows, i * W_MIX + lanes.start:i * W_MIX + lanes.stop]

    def project_pieces(rows):
        def norm():
            hb_ref[rows, :] = _rms(x_ref[rows, :], g_ref[...]).astype(BF16)

        def dot_piece(i):
            cols = slice(i * W_MIX, (i + 1) * W_MIX)
            z_ref[rows, cols] = _dot(hb_ref[rows, :], win_ref[:, cols])

        return [norm] + [functools.partial(dot_piece, i) for i in (3, 4, 1, 2, 0)]

    def mix_pieces(rows):
        def gates(r0):
            crow = slice(r0, r0 + ROW_CHUNK)
            ua_ext[HIST_A + r0:HIST_A + r0 + ROW_CHUNK, :] = zseg(crow, 1) * zseg(crow, 2)
            ub_ext[HIST_B + r0:HIST_B + r0 + ROW_CHUNK, :] = zseg(crow, 3) * jax.nn.sigmoid(zseg(crow, 4))

        def convs(r0, gi):
            crow = slice(r0, r0 + CONV_ROWS)
            lanes = slice(gi * GROUP_W, (gi + 1) * GROUP_W)
            win_a = ua_ext.at[r0:r0 + CONV_ROWS + HIST_A, :]
            win_b = ub_ext.at[r0:r0 + CONV_ROWS + HIST_B, :]
            ya = zseg(crow, 0, lanes) * _causal_dwconv(win_a, ca_ref, K_A, HIST_A, CONV_ROWS, lanes)
            cat_ref[crow, lanes] = ya.astype(BF16)
            cbv_ref[crow, lanes] = _causal_dwconv(win_b, cb_ref, K_B, HIST_B, CONV_ROWS, lanes)

        def norm_act(r0):
            crow = slice(r0, r0 + CONV_ROWS)
            yb = _silu(_ln(cbv_ref[crow, :] + cbb_ref[...], lng_ref[...], lnb_ref[...]))
            cat_ref[crow, W_MIX:2 * W_MIX] = yb.astype(BF16)

        pieces = [functools.partial(gates, r0) for r0 in range(rows.start, rows.stop, ROW_CHUNK)]
        for r0 in range(rows.start, rows.stop, CONV_ROWS):
            pieces += [functools.partial(convs, r0, gi) for gi in range(N_GROUPS)]
            pieces.append(functools.partial(norm_act, r0))
        return pieces

    def out_pieces(rows):
        def piece(j):
            cols = slice(j * W_MIX, (j + 1) * W_MIX)
            o_ref[rows, cols] = x_ref[rows, cols] + _dot(cat_ref[rows, :], wout_ref[:, cols])

        return [functools.partial(piece, j) for j in range(D_MODEL // W_MIX)]

    def emit(mxu_pieces, vpu_pieces):
        n_m, n_v = len(mxu_pieces), len(vpu_pieces)
        i = j = 0
        while i < n_m or j < n_v:
            if j >= n_v or (i < n_m and i * n_v <= j * n_m):
                mxu_pieces[i]()
                i += 1
            else:
                vpu_pieces[j]()
                j += 1

    blocks = [slice(r, r + MIX_BLOCK_ROWS) for r in range(0, tm, MIX_BLOCK_ROWS)]
    emit(project_pieces(blocks[0]), [])
    for i in range(len(blocks)):
        mxu = out_pieces(blocks[i - 1]) if i > 0 else []
        if i + 1 < len(blocks):
            mxu = project_pieces(blocks[i + 1]) + mxu
        emit(mxu, mix_pieces(blocks[i]))
    emit(out_pieces(blocks[-1]), [])

    @pl.when(s == pl.num_programs(1) - 1)
    def _():
        sa_ref[...] = ua_ext[HIST_A + tm - (K_A - 1):HIST_A + tm, :]
        sb_ref[...] = ub_ext[HIST_B + tm - (K_B - 1):HIST_B + tm, :]


def _even_prompt(x, g, win, ca, cb, cbb, lng, lnb, wout, tm, cast_next=()):
    b, s, _ = x.shape
    n_s = s // tm
    tile = pl.BlockSpec((None, tm, D_MODEL), lambda i, j: (i, j, 0))
    n_in = win.shape[1]
    c_specs, c_shapes = _cast_plumbing(cast_next, b * n_s, lambda i, j: i * n_s + j)
    return pl.pallas_call(
        functools.partial(_even_prompt_kernel, tm=tm, n_cast=len(cast_next)),
        grid=(b, n_s),
        in_specs=[tile, _const_spec((1, D_MODEL)), _const_spec((D_MODEL, n_in)),
                  _const_spec((K_A, W_MIX)), _const_spec((K_B, W_MIX)), _const_spec((1, W_MIX)),
                  _const_spec((1, W_MIX)), _const_spec((1, W_MIX)),
                  _const_spec((2 * W_MIX, D_MODEL))] + c_specs,
        out_specs=[tile,
                   pl.BlockSpec((None, K_A - 1, W_MIX), lambda i, j: (i, 0, 0)),
                   pl.BlockSpec((None, K_B - 1, W_MIX), lambda i, j: (i, 0, 0))] + c_specs,
        out_shape=[jax.ShapeDtypeStruct((b, s, D_MODEL), F32),
                   jax.ShapeDtypeStruct((b, K_A - 1, W_MIX), F32),
                   jax.ShapeDtypeStruct((b, K_B - 1, W_MIX), F32)] + c_shapes,
        scratch_shapes=[pltpu.VMEM((tm, D_MODEL), BF16),
                        pltpu.VMEM((tm, n_in), F32),
                        pltpu.VMEM((tm + HIST_A, W_MIX), F32),
                        pltpu.VMEM((tm + HIST_B, W_MIX), F32),
                        pltpu.VMEM((tm, W_MIX), F32),
                        pltpu.VMEM((tm, 2 * W_MIX), BF16)],
        compiler_params=_params(2),
        name="even_prompt",
    )(x, g.reshape(1, D_MODEL), win, ca, cb, cbb.reshape(1, W_MIX), lng.reshape(1, W_MIX),
      lnb.reshape(1, W_MIX), wout, *cast_next)


def _tril_bf16(w):
    t = lax.broadcasted_iota(jnp.int32, (CHUNK, CHUNK), 0)
    s = lax.broadcasted_iota(jnp.int32, (CHUNK, CHUNK), 1)
    return jnp.where(s <= t, w, 0.0).astype(BF16)


def _odd_prompt_kernel(*refs, tm, n_cast):
    (x_ref, g_ref, win_ref, wpool_ref, pscale_ref, sguw_ref, sgubt_ref, lng_ref, lnb_ref,
     wout_ref) = refs[:10]
    cast_in = refs[10:10 + n_cast]
    o_ref, sp_ref = refs[10 + n_cast:12 + n_cast]
    cast_out = refs[12 + n_cast:12 + 2 * n_cast]
    hb_ref, p_ext, u_ref, vn_ref, d_ref, cat_ref, wm_ref = refs[12 + 2 * n_cast:]
    s = pl.program_id(1)
    _cast_rows(cast_in, cast_out)

    @pl.when(jnp.logical_and(pl.program_id(0) == 0, s == 0))
    def _():
        for gi in range(N_GROUPS):
            wm_ref[gi] = _tril_bf16(sguw_ref[gi])

    @pl.when(s == 0)
    def _():
        p_ext[0:HIST_P, :] = jnp.zeros((HIST_P, W_MIX), F32)

    @pl.when(s > 0)
    def _():
        p_ext[0:HIST_P, :] = p_ext[tm:tm + HIST_P, :]

    hb_ref[...] = _rms(x_ref[...], g_ref[...]).astype(BF16)

    def seg(i):
        return _dot(hb_ref[...], win_ref[:, i * W_MIX:(i + 1) * W_MIX])

    p_ext[HIST_P:HIST_P + tm, :] = seg(0)
    u_ref[...] = seg(1)
    vn_ref[...] = _ln(seg(2), lng_ref[...], lnb_ref[...]).astype(BF16)

    def pool_body(c, carry):
        r0 = pl.multiple_of(c * ROW_CHUNK, ROW_CHUNK)
        win = p_ext.at[pl.ds(r0, ROW_CHUNK + HIST_P), :]
        pos = s * tm + r0 + lax.broadcasted_iota(jnp.int32, (ROW_CHUNK, GROUP_W), 0)
        for gi, w in enumerate(POOL_WINDOWS):
            lanes = slice(gi * GROUP_W, (gi + 1) * GROUP_W)
            tok = win[HIST_P:HIST_P + ROW_CHUNK, lanes]
            wsum = tok
            for j in range(1, w):
                wsum = wsum + win[HIST_P - j:HIST_P - j + ROW_CHUNK, lanes]
            cnt = jnp.minimum(pos + 1, w).astype(F32)
            d_ref[pl.ds(r0, ROW_CHUNK), lanes] = (wsum / cnt - tok).astype(BF16)
        return carry

    lax.fori_loop(0, tm // ROW_CHUNK, pool_body, 0)

    for gi in range(N_GROUPS):
        lanes = slice(gi * GROUP_W, (gi + 1) * GROUP_W)
        yc = _dot(d_ref[:, lanes], wpool_ref[gi]) * pscale_ref[:, lanes]
        cat_ref[:, lanes] = yc.astype(BF16)

    for c in range(tm // CHUNK):
        rows = slice(c * CHUNK, (c + 1) * CHUNK)
        for gi in range(N_GROUPS):
            lanes = slice(gi * GROUP_W, (gi + 1) * GROUP_W)
            mixed = _dot(wm_ref[gi], vn_ref[rows, lanes]) + sgubt_ref[:, gi:gi + 1]
            yd = u_ref[rows, lanes] * mixed
            cat_ref[rows, W_MIX + gi * GROUP_W:W_MIX + (gi + 1) * GROUP_W] = yd.astype(BF16)

    o_ref[...] = x_ref[...] + _dot(cat_ref[...], wout_ref[...])

    @pl.when(s == pl.num_programs(1) - 1)
    def _():
        sp_ref[...] = p_ext[HIST_P + tm - POOL_HIST:HIST_P + tm, :]


def _odd_prompt(x, g, win, wpool, pscale, sguw, sgub, lng, lnb, wout, tm, cast_next=()):
    b, s, _ = x.shape
    n_s = s // tm
    tile = pl.BlockSpec((None, tm, D_MODEL), lambda i, j: (i, j, 0))
    n_in = win.shape[1]
    c_specs, c_shapes = _cast_plumbing(cast_next, b * n_s, lambda i, j: i * n_s + j)
    return pl.pallas_call(
        functools.partial(_odd_prompt_kernel, tm=tm, n_cast=len(cast_next)),
        grid=(b, n_s),
        in_specs=[tile, _const_spec((1, D_MODEL)), _const_spec((D_MODEL, n_in)),
                  _const_spec((N_GROUPS, GROUP_W, GROUP_W)), _const_spec((1, W_MIX)),
                  _const_spec((N_GROUPS, CHUNK, CHUNK)), _const_spec((CHUNK, N_GROUPS)),
                  _const_spec((1, W_MIX)), _const_spec((1, W_MIX)),
                  _const_spec((2 * W_MIX, D_MODEL))] + c_specs,
        out_specs=[tile, pl.BlockSpec((None, POOL_HIST, W_MIX), lambda i, j: (i, 0, 0))] + c_specs,
        out_shape=[jax.ShapeDtypeStruct((b, s, D_MODEL), F32),
                   jax.ShapeDtypeStruct((b, POOL_HIST, W_MIX), F32)] + c_shapes,
        scratch_shapes=[pltpu.VMEM((tm, D_MODEL), BF16),
                        pltpu.VMEM((tm + HIST_P, W_MIX), F32),
                        pltpu.VMEM((tm, W_MIX), F32),
                        pltpu.VMEM((tm, W_MIX), BF16),
                        pltpu.VMEM((tm, W_MIX), BF16),
                        pltpu.VMEM((tm, 2 * W_MIX), BF16),
                        pltpu.VMEM((N_GROUPS, CHUNK, CHUNK), BF16)],
        compiler_params=_params(2),
        name="odd_prompt",
    )(x, g.reshape(1, D_MODEL), win, wpool, pscale.reshape(1, W_MIX), sguw, sgub.T,
      lng.reshape(1, W_MIX), lnb.reshape(1, W_MIX), wout, *cast_next)


def _even_sample_kernel(x_ref, g_ref, win_ref, ca_ref, cb_ref, cbb_ref, lng_ref, lnb_ref, wout_ref,
                        ha_ref, hbst_ref,
                        o_ref, sa_ref, sb_ref,
                        hb_ref, gate_ref, ua_ext, ub_ext, cbv_ref, cat_ref, *, t_len, n_seq):
    x = x_ref[...].reshape(t_len * n_seq, D_MODEL)
    hb_ref[...] = _rms(x, g_ref[...]).astype(BF16)

    def seg(i):
        return _dot(hb_ref[...], win_ref[:, i * W_MIX:(i + 1) * W_MIX])

    gate_ref[...] = seg(0)
    ua_ext[0:K_A - 1] = ha_ref[...]
    ub_ext[0:K_B - 1] = hbst_ref[...]
    ua_ext[K_A - 1:K_A - 1 + t_len] = (seg(1) * seg(2)).reshape(t_len, n_seq, W_MIX)
    ub_ext[K_B - 1:K_B - 1 + t_len] = (seg(3) * jax.nn.sigmoid(seg(4))).reshape(t_len, n_seq, W_MIX)

    def body(t, carry):
        rows = pl.ds(pl.multiple_of(t * n_seq, n_seq), n_seq)
        for gi in range(N_GROUPS):
            lanes = slice(gi * GROUP_W, (gi + 1) * GROUP_W)
            acc = jnp.zeros((n_seq, GROUP_W), F32)
            for k in range(K_A):
                acc = acc + ca_ref[k:k + 1, lanes] * ua_ext[t + k, :, lanes]
            cat_ref[rows, lanes] = (gate_ref[rows, lanes] * acc).astype(BF16)
            acc = jnp.zeros((n_seq, GROUP_W), F32)
            for k in range(K_B):
                acc = acc + cb_ref[k:k + 1, lanes] * ub_ext[t + k, :, lanes]
            cbv_ref[:, lanes] = acc + cbb_ref[:, lanes]
        yb = _silu(_ln(cbv_ref[...], lng_ref[...], lnb_ref[...]))
        cat_ref[rows, W_MIX:2 * W_MIX] = yb.astype(BF16)
        return carry

    lax.fori_loop(0, t_len, body, 0)

    o_ref[...] = (x + _dot(cat_ref[...], wout_ref[...])).reshape(t_len, n_seq, D_MODEL)
    sa_ref[...] = ua_ext[t_len:t_len + K_A - 1]
    sb_ref[...] = ub_ext[t_len:t_len + K_B - 1]


def _seq_block_spec(rows, nb, width):
    return pl.BlockSpec((rows, nb, width), lambda i, *_: (0, i, 0))


def _even_sample(x, g, win, ca, cb, cbb, lng, lnb, wout, hist_a, hist_b, nb):
    t_len, n_seq, _ = x.shape
    m = t_len * nb
    n_in = win.shape[1]
    return pl.pallas_call(
        functools.partial(_even_sample_kernel, t_len=t_len, n_seq=nb),
        grid=(n_seq // nb,),
        in_specs=[_seq_block_spec(t_len, nb, D_MODEL), _const_spec((1, D_MODEL)),
                  _const_spec((D_MODEL, n_in)),
                  _const_spec((K_A, W_MIX)), _const_spec((K_B, W_MIX)), _const_spec((1, W_MIX)),
                  _const_spec((1, W_MIX)), _const_spec((1, W_MIX)), _const_spec((2 * W_MIX, D_MODEL)),
                  _seq_block_spec(K_A - 1, nb, W_MIX), _seq_block_spec(K_B - 1, nb, W_MIX)],
        out_specs=[_seq_block_spec(t_len, nb, D_MODEL), _seq_block_spec(K_A - 1, nb, W_MIX),
                   _seq_block_spec(K_B - 1, nb, W_MIX)],
        out_shape=[jax.ShapeDtypeStruct((t_len, n_seq, D_MODEL), F32),
                   jax.ShapeDtypeStruct((K_A - 1, n_seq, W_MIX), F32),
                   jax.ShapeDtypeStruct((K_B - 1, n_seq, W_MIX), F32)],
        scratch_shapes=[pltpu.VMEM((m, D_MODEL), BF16),
                        pltpu.VMEM((m, W_MIX), F32),
                        pltpu.VMEM((t_len + K_A - 1, nb, W_MIX), F32),
                        pltpu.VMEM((t_len + K_B - 1, nb, W_MIX), F32),
                        pltpu.VMEM((nb, W_MIX), F32),
                        pltpu.VMEM((m, 2 * W_MIX), BF16)],
        compiler_params=_params(1),
        name="even_sample",
    )(x, g.reshape(1, D_MODEL), win, ca, cb, cbb.reshape(1, W_MIX), lng.reshape(1, W_MIX),
      lnb.reshape(1, W_MIX), wout, hist_a, hist_b)


def _odd_sample_kernel(sw_ref, sb_ref_smem,
                       x_ref, g_ref, win_ref, wpool_ref, pscale_ref, lng_ref, lnb_ref, wout_ref,
                       hp_ref,
                       o_ref, sp_ref, vn_out_ref,
                       hb_ref, p_ext, u_ref, d_ref, cat_ref, *, t_len, n_seq, start_pos):
    x = x_ref[...].reshape(t_len * n_seq, D_MODEL)
    hb_ref[...] = _rms(x, g_ref[...]).astype(BF16)

    def seg(i):
        return _dot(hb_ref[...], win_ref[:, i * W_MIX:(i + 1) * W_MIX])

    p_ext[0:POOL_HIST] = hp_ref[...]
    p_ext[POOL_HIST:POOL_HIST + t_len] = seg(0).reshape(t_len, n_seq, W_MIX)
    u_ref[...] = seg(1).reshape(t_len, n_seq, W_MIX)
    vn_out_ref[...] = _ln(seg(2), lng_ref[...], lnb_ref[...]).reshape(t_len, n_seq, W_MIX)

    for t in range(t_len):
        rows = slice(t * n_seq, (t + 1) * n_seq)
        for gi, w in enumerate(POOL_WINDOWS):
            lanes = slice(gi * GROUP_W, (gi + 1) * GROUP_W)
            tok = p_ext[POOL_HIST + t, :, lanes]
            wsum = tok
            for j in range(1, w):
                wsum = wsum + p_ext[POOL_HIST + t - j, :, lanes]
            cnt = float(min(start_pos + t + 1, w))
            d_ref[rows, lanes] = (wsum / cnt - tok).astype(BF16)
            mixed = jnp.zeros((n_seq, GROUP_W), F32) + sb_ref_smem[gi * t_len + t]
            for s in range(t + 1):
                mixed = mixed + sw_ref[(gi * t_len + t) * t_len + s] * vn_out_ref[s, :, lanes]
            cat_ref[rows, W_MIX + gi * GROUP_W:W_MIX + (gi + 1) * GROUP_W] = (
                u_ref[t, :, lanes] * mixed).astype(BF16)

    for gi in range(N_GROUPS):
        lanes = slice(gi * GROUP_W, (gi + 1) * GROUP_W)
        yc = _dot(d_ref[:, lanes], wpool_ref[gi]) * pscale_ref[:, lanes]
        cat_ref[:, lanes] = yc.astype(BF16)

    o_ref[...] = (x + _dot(cat_ref[...], wout_ref[...])).reshape(t_len, n_seq, D_MODEL)
    sp_ref[...] = p_ext[t_len:t_len + POOL_HIST]


def _odd_sample(x, g, win, wpool, pscale, sguw, sgub, lng, lnb, wout, hist_p, nb, start_pos):
    t_len, n_seq, _ = x.shape
    m = t_len * nb
    n_in = win.shape[1]
    assert t_len <= CHUNK
    sw = sguw[:, :t_len, :t_len].reshape(-1)
    sb = sgub[:, :t_len].reshape(-1)

    grid_spec = pltpu.PrefetchScalarGridSpec(
        num_scalar_prefetch=2,
        grid=(n_seq // nb,),
        in_specs=[_seq_block_spec(t_len, nb, D_MODEL), _const_spec((1, D_MODEL)),
                  _const_spec((D_MODEL, n_in)),
                  _const_spec((N_GROUPS, GROUP_W, GROUP_W)), _const_spec((1, W_MIX)),
                  _const_spec((1, W_MIX)), _const_spec((1, W_MIX)),
                  _const_spec((2 * W_MIX, D_MODEL)),
                  _seq_block_spec(POOL_HIST, nb, W_MIX)],
        out_specs=[_seq_block_spec(t_len, nb, D_MODEL), _seq_block_spec(POOL_HIST, nb, W_MIX),
                   _seq_block_spec(t_len, nb, W_MIX)],
        scratch_shapes=[pltpu.VMEM((m, D_MODEL), BF16),
                        pltpu.VMEM((t_len + POOL_HIST, nb, W_MIX), F32),
                        pltpu.VMEM((t_len, nb, W_MIX), F32),
                        pltpu.VMEM((m, W_MIX), BF16),
                        pltpu.VMEM((m, 2 * W_MIX), BF16)],
    )
    return pl.pallas_call(
        functools.partial(_odd_sample_kernel, t_len=t_len, n_seq=nb, start_pos=start_pos),
        grid_spec=grid_spec,
        out_shape=[jax.ShapeDtypeStruct((t_len, n_seq, D_MODEL), F32),
                   jax.ShapeDtypeStruct((POOL_HIST, n_seq, W_MIX), F32),
                   jax.ShapeDtypeStruct((t_len, n_seq, W_MIX), F32)],
        compiler_params=_params(1),
        name="odd_sample",
    )(sw, sb, x, g.reshape(1, D_MODEL), win, wpool, pscale.reshape(1, W_MIX),
      lng.reshape(1, W_MIX), lnb.reshape(1, W_MIX), wout, hist_p)


def _time_major(a):
    return jnp.swapaxes(a, 0, 1)


def kernel(x_prompt, x_sample, state_conv_a, state_conv_b, state_pool, norm_mix, norm_ffn, ev_w_in,
           ev_conv_a, ev_conv_b, ev_conv_b_bias, ev_ln_g, ev_ln_b, ev_w_out, od_w_in, od_pool_w,
           od_pool_scale, od_sgu_w, od_sgu_b, od_sgu_ln_g, od_sgu_ln_b, od_w_out, ffn_w1, ffn_w2,
           norm_final):
    depth = norm_mix.shape[0]
    batch, seq, _ = x_prompt.shape
    n_seq, t_len, _ = x_sample.shape
    tm_mix = 512
    tm_ffn = 512
    nb = 32

    def mixer_weights(l):
        i = l // 2
        return (ev_w_in[i], ev_w_out[i]) if l % 2 == 0 else (od_w_in[i], od_w_out[i])

    xp = x_prompt
    xs = _time_major(x_sample)
    sa_p, sa_s, sb_p, sb_s, sc_p, sc_s, vn_s = [], [], [], [], [], [], []
    win, wout = (w.astype(BF16) for w in mixer_weights(0))
    for l in range(depth):
        i = l // 2
        ffn_f32 = (ffn_w1[l], ffn_w2[l])
        if l % 2 == 0:
            common = (norm_mix[l], win, ev_conv_a[i], ev_conv_b[i], ev_conv_b_bias[i], ev_ln_g[i],
                      ev_ln_b[i], wout)
            xp, a_p, b_p, w1, w2 = _even_prompt(xp, *common, tm=tm_mix, cast_next=ffn_f32)
            xs, a_s, b_s = _even_sample(xs, *common, _time_major(state_conv_a[i]),
                                        _time_major(state_conv_b[i]), nb)
            sa_p.append(a_p)
            sb_p.append(b_p)
            sa_s.append(_time_major(a_s))
            sb_s.append(_time_major(b_s))
        else:
            common = (norm_mix[l], win, od_pool_w[i].astype(BF16), od_pool_scale[i], od_sgu_w[i],
                      od_sgu_b[i], od_sgu_ln_g[i], od_sgu_ln_b[i], wout)
            xp, c_p, w1, w2 = _odd_prompt(xp, *common, tm=tm_mix, cast_next=ffn_f32)
            xs, c_s, v_s = _odd_sample(xs, *common, _time_major(state_pool[i]), nb, PAST_LEN)
            sc_p.append(c_p)
            sc_s.append(_time_major(c_s))
            vn_s.append(_time_major(v_s))
        last = l == depth - 1
        xp, xs, *next_w = _ffn(xp.reshape(batch * seq, D_MODEL), xs.reshape(t_len * n_seq, D_MODEL),
                               norm_ffn[l], w1, w2, norm_final if last else None, tm_ffn,
                               cast_next=() if last else mixer_weights(l + 1))
        xp = xp.reshape(batch, seq, D_MODEL)
        xs = xs.reshape(t_len, n_seq, D_MODEL)
        if not last:
            win, wout = next_w

    y_sample = _time_major(xs)
    return (xp, y_sample, jnp.stack(sa_p), jnp.stack(sa_s), jnp.stack(sb_p), jnp.stack(sb_s),
            jnp.stack(sc_p), jnp.stack(sc_s), jnp.stack(vn_s))
```

```python
import functools

import jax
import jax.numpy as jnp
from jax import lax
from jax.experimental import pallas as pl
from jax.experimental.pallas import tpu as pltpu

F32 = jnp.float32
BF16 = jnp.bfloat16

D_MODEL = 1024
W_MIX = D_MODEL // 2
K_A = 3
K_B = 31
POOL_WINDOWS = (2, 4, 8, 16)
POOL_HIST = max(POOL_WINDOWS) - 1
CHUNK = 128
N_GROUPS = 4
GROUP_W = W_MIX // N_GROUPS
D_FF = 4 * D_MODEL
PAST_LEN = 16384
EPS = 1e-6

SUBLANES = 8
HIST_A = 8
HIST_B = 32
HIST_P = 16
ROW_CHUNK = 64
CONV_ROWS = 128
MIX_BLOCK_ROWS = 256
FFN_COL_CHUNK = 1024
VMEM_LIMIT = 56 * 1024 * 1024


def _rms(x, g):
    return x * lax.rsqrt(jnp.mean(x * x, axis=-1, keepdims=True) + EPS) * g


def _ln(x, g, b):
    mu = jnp.mean(x, axis=-1, keepdims=True)
    xc = x - mu
    return xc * lax.rsqrt(jnp.mean(xc * xc, axis=-1, keepdims=True) + EPS) * g + b


def _dot(a, b):
    return jnp.dot(a, b, preferred_element_type=F32)


def _silu(x):
    return x * jax.nn.sigmoid(x)


def _const_spec(shape):
    nd = len(shape)
    return pl.BlockSpec(shape, lambda *_: (0,) * nd, pipeline_mode=pl.Buffered(1))


def _params(n_grid_axes, flags=None):
    return pltpu.CompilerParams(
        dimension_semantics=("arbitrary",) * n_grid_axes,
        vmem_limit_bytes=VMEM_LIMIT,
        flags=flags,
    )


def _cast_plumbing(arrays, n_steps, step_of):
    in_specs, out_specs, shapes = [], [], []
    for stacked, layer in arrays:
        _, n_rows, n_cols = stacked.shape
        rows, rem = divmod(n_rows, n_steps)
        assert rem == 0 and rows % (2 * SUBLANES) == 0, stacked.shape
        in_specs.append(pl.BlockSpec((None, rows, n_cols),
                                     lambda *idx, _l=layer: (_l, step_of(*idx), 0)))
        out_specs.append(pl.BlockSpec((rows, n_cols), lambda *idx: (step_of(*idx), 0)))
        shapes.append(jax.ShapeDtypeStruct((n_rows, n_cols), BF16))
    return in_specs, out_specs, shapes


def _cast_rows(src_refs, dst_refs):
    for src, dst in zip(src_refs, dst_refs, strict=True):
        dst[...] = src[...].astype(BF16)


def _ffn_kernel(*refs, final, n_prompt, n_cast):
    n_in = 6 if final else 5
    xp_ref, xs_ref, g_ref, w1_ref, w2_ref = refs[:5]
    cast_in = refs[n_in:n_in + n_cast]
    op_ref, os_ref = refs[n_in + n_cast:n_in + n_cast + 2]
    cast_out = refs[n_in + n_cast + 2:n_in + 2 * n_cast + 2]
    hb_ref, act_ref = refs[n_in + 2 * n_cast + 2:]
    step = pl.program_id(0)

    def rows_block(x_ref, o_ref):
        hb_ref[...] = _rms(x_ref[...], g_ref[...]).astype(BF16)
        for c in range(D_FF // FFN_COL_CHUNK):
            cols = slice(c * FFN_COL_CHUNK, (c + 1) * FFN_COL_CHUNK)
            a = _dot(hb_ref[...], w1_ref[:, cols])
            act_ref[:, cols] = jnp.square(jnp.maximum(a, 0.0)).astype(BF16)
        y = x_ref[...] + _dot(act_ref[...], w2_ref[...])
        if final:
            y = _rms(y, refs[5][...])
        o_ref[...] = y

    @pl.when(step < n_prompt)
    def _():
        rows_block(xp_ref, op_ref)
        _cast_rows(cast_in, cast_out)

    @pl.when(step >= n_prompt)
    def _():
        rows_block(xs_ref, os_ref)


def _ffn(xp2d, xs2d, g, w1, w2, g_final, tm, cast_next=()):
    n_prompt, n_sample = xp2d.shape[0] // tm, xs2d.shape[0] // tm
    final = g_final is not None
    last = n_prompt - 1
    p_spec = pl.BlockSpec((tm, D_MODEL), lambda i: (jnp.minimum(i, last), 0))
    s_spec = pl.BlockSpec((tm, D_MODEL), lambda i: (jnp.maximum(i - n_prompt, 0), 0))
    in_specs = [p_spec, s_spec, _const_spec((1, D_MODEL)), _const_spec((D_MODEL, D_FF)),
                _const_spec((D_FF, D_MODEL))]
    args = [xp2d, xs2d, g.reshape(1, D_MODEL), w1, w2]
    if final:
        in_specs.append(_const_spec((1, D_MODEL)))
        args.append(g_final.reshape(1, D_MODEL))
    c_in, c_out, c_shapes = _cast_plumbing(cast_next, n_prompt, lambda i: jnp.minimum(i, last))
    return pl.pallas_call(
        functools.partial(_ffn_kernel, final=final, n_prompt=n_prompt, n_cast=len(cast_next)),
        grid=(n_prompt + n_sample,),
        in_specs=in_specs + c_in,
        out_specs=[p_spec, s_spec] + c_out,
        out_shape=[jax.ShapeDtypeStruct(xp2d.shape, F32), jax.ShapeDtypeStruct(xs2d.shape, F32)] + c_shapes,
        scratch_shapes=[pltpu.VMEM((tm, D_MODEL), BF16), pltpu.VMEM((tm, D_FF), BF16)],
        compiler_params=_params(1),
        name="ffn_final" if final else "ffn",
    )(*args, *(a for a, _ in cast_next))


def _causal_dwconv(win, w_ref, n_taps, hist, rows, lanes):
    out = None
    for r in range(SUBLANES):
        acc = None
        for q in range(hist // SUBLANES):
            k = n_taps - 1 - SUBLANES * q - r
            if k < 0:
                continue
            start = hist - SUBLANES * (q + 1)
            term = w_ref[k:k + 1, lanes] * win[start:start + rows + SUBLANES, lanes]
            acc = term if acc is None else acc + term
        if acc is None:
            continue
        part = acc[SUBLANES - r:SUBLANES - r + rows]
        out = part if out is None else out + part
    return out


def _even_prompt_kernel(*refs, tm, n_cast):
    x_ref, g_ref, win_ref, ca_ref, cb_ref, cbb_ref, lng_ref, lnb_ref, wout_ref = refs[:9]
    cast_in = refs[9:9 + n_cast]
    o_ref, sa_ref, sb_ref = refs[9 + n_cast:12 + n_cast]
    cast_out = refs[12 + n_cast:12 + 2 * n_cast]
    hb_ref, z_ref, ua_ext, ub_ext, cbv_ref, cat_ref = refs[12 + 2 * n_cast:]
    s = pl.program_id(1)
    _cast_rows(cast_in, cast_out)

    @pl.when(s == 0)
    def _():
        ua_ext[0:HIST_A, :] = jnp.zeros((HIST_A, W_MIX), F32)
        ub_ext[0:HIST_B, :] = jnp.zeros((HIST_B, W_MIX), F32)

    @pl.when(s > 0)
    def _():
        ua_ext[0:HIST_A, :] = ua_ext[tm:tm + HIST_A, :]
        ub_ext[0:HIST_B, :] = ub_ext[tm:tm + HIST_B, :]

    def zseg(rows, i, lanes=slice(0, W_MIX)):
        return z_ref[rows, i * W_MIX + lanes.start:i * W_MIX + lanes.stop]

    def project_pieces(rows):
        def norm():
            hb_ref[rows, :] = _rms(x_ref[rows, :], g_ref[...]).astype(BF16)

        def dot_piece(i):
            cols = slice(i * W_MIX, (i + 1) * W_MIX)
            z_ref[rows, cols] = _dot(hb_ref[rows, :], win_ref[:, cols])

        return [norm] + [functools.partial(dot_piece, i) for i in (3, 4, 1, 2, 0)]

    def mix_pieces(rows):
        def gates(r0):
            crow = slice(r0, r0 + ROW_CHUNK)
            ua_ext[HIST_A + r0:HIST_A + r0 + ROW_CHUNK, :] = zseg(crow, 1) * zseg(crow, 2)
            ub_ext[HIST_B + r0:HIST_B + r0 + ROW_CHUNK, :] = zseg(crow, 3) * jax.nn.sigmoid(zseg(crow, 4))

        def convs(r0, gi):
            crow = slice(r0, r0 + CONV_ROWS)
            lanes = slice(gi * GROUP_W, (gi + 1) * GROUP_W)
            win_a = ua_ext.at[r0:r0 + CONV_ROWS + HIST_A, :]
            win_b = ub_ext.at[r0:r0 + CONV_ROWS + HIST_B, :]
            ya = zseg(crow, 0, lanes) * _causal_dwconv(win_a, ca_ref, K_A, HIST_A, CONV_ROWS, lanes)
            cat_ref[crow, lanes] = ya.astype(BF16)
            cbv_ref[crow, lanes] = _causal_dwconv(win_b, cb_ref, K_B, HIST_B, CONV_ROWS, lanes)

        def norm_act(r0):
            crow = slice(r0, r0 + CONV_ROWS)
            yb = _silu(_ln(cbv_ref[crow, :] + cbb_ref[...], lng_ref[...], lnb_ref[...]))
            cat_ref[crow, W_MIX:2 * W_MIX] = yb.astype(BF16)

        pieces = [functools.partial(gates, r0) for r0 in range(rows.start, rows.stop, ROW_CHUNK)]
        for r0 in range(rows.start, rows.stop, CONV_ROWS):
            pieces += [functools.partial(convs, r0, gi) for gi in range(N_GROUPS)]
            pieces.append(functools.partial(norm_act, r0))
        return pieces

    def out_pieces(rows):
        def piece(j):
            cols = slice(j * W_MIX, (j + 1) * W_MIX)
            o_ref[rows, cols] = x_ref[rows, cols] + _dot(cat_ref[rows, :], wout_ref[:, cols])

        return [functools.partial(piece, j) for j in range(D_MODEL // W_MIX)]

    def emit(mxu_pieces, vpu_pieces):
        n_m, n_v = len(mxu_pieces), len(vpu_pieces)
        i = j = 0
        while i < n_m or j < n_v:
            if j >= n_v or (i < n_m and i * n_v <= j * n_m):
                mxu_pieces[i]()
                i += 1
            else:
                vpu_pieces[j]()
                j += 1

    blocks = [slice(r, r + MIX_BLOCK_ROWS) for r in range(0, tm, MIX_BLOCK_ROWS)]
    emit(project_pieces(blocks[0]), [])
    for i in range(len(blocks)):
        mxu = out_pieces(blocks[i - 1]) if i > 0 else []
        if i + 1 < len(blocks):
            mxu = project_pieces(blocks[i + 1]) + mxu
        emit(mxu, mix_pieces(blocks[i]))
    emit(out_pieces(blocks[-1]), [])

    @pl.when(s == pl.num_programs(1) - 1)
    def _():
        sa_ref[...] = ua_ext[HIST_A + tm - (K_A - 1):HIST_A + tm, :]
        sb_ref[...] = ub_ext[HIST_B + tm - (K_B - 1):HIST_B + tm, :]


def _even_prompt(x, g, win, ca, cb, cbb, lng, lnb, wout, tm, cast_next=()):
    b, s, _ = x.shape
    n_s = s // tm
    tile = pl.BlockSpec((None, tm, D_MODEL), lambda i, j: (i, j, 0))
    n_in = win.shape[1]
    c_in, c_out, c_shapes = _cast_plumbing(cast_next, b * n_s, lambda i, j: i * n_s + j)
    return pl.pallas_call(
        functools.partial(_even_prompt_kernel, tm=tm, n_cast=len(cast_next)),
        grid=(b, n_s),
        in_specs=[tile, _const_spec((1, D_MODEL)), _const_spec((D_MODEL, n_in)),
                  _const_spec((K_A, W_MIX)), _const_spec((K_B, W_MIX)), _const_spec((1, W_MIX)),
                  _const_spec((1, W_MIX)), _const_spec((1, W_MIX)),
                  _const_spec((2 * W_MIX, D_MODEL))] + c_in,
        out_specs=[tile,
                   pl.BlockSpec((None, K_A - 1, W_MIX), lambda i, j: (i, 0, 0)),
                   pl.BlockSpec((None, K_B - 1, W_MIX), lambda i, j: (i, 0, 0))] + c_out,
        out_shape=[jax.ShapeDtypeStruct((b, s, D_MODEL), F32),
                   jax.ShapeDtypeStruct((b, K_A - 1, W_MIX), F32),
                   jax.ShapeDtypeStruct((b, K_B - 1, W_MIX), F32)] + c_shapes,
        scratch_shapes=[pltpu.VMEM((tm, D_MODEL), BF16),
                        pltpu.VMEM((tm, n_in), F32),
                        pltpu.VMEM((tm + HIST_A, W_MIX), F32),
                        pltpu.VMEM((tm + HIST_B, W_MIX), F32),
                        pltpu.VMEM((tm, W_MIX), F32),
                        pltpu.VMEM((tm, 2 * W_MIX), BF16)],
        compiler_params=_params(2),
        name="even_prompt",
    )(x, g.reshape(1, D_MODEL), win, ca, cb, cbb.reshape(1, W_MIX), lng.reshape(1, W_MIX),
      lnb.reshape(1, W_MIX), wout, *(a for a, _ in cast_next))


def _tril_bf16(w):
    t = lax.broadcasted_iota(jnp.int32, (CHUNK, CHUNK), 0)
    s = lax.broadcasted_iota(jnp.int32, (CHUNK, CHUNK), 1)
    return jnp.where(s <= t, w, 0.0).astype(BF16)


def _odd_prompt_kernel(*refs, tm, n_cast):
    (x_ref, g_ref, win_ref, wpool_ref, pscale_ref, sguw_ref, sgubt_ref, lng_ref, lnb_ref,
     wout_ref) = refs[:10]
    cast_in = refs[10:10 + n_cast]
    o_ref, sp_ref = refs[10 + n_cast:12 + n_cast]
    cast_out = refs[12 + n_cast:12 + 2 * n_cast]
    hb_ref, p_ext, u_ref, vn_ref, d_ref, cat_ref, wm_ref = refs[12 + 2 * n_cast:]
    s = pl.program_id(1)
    _cast_rows(cast_in, cast_out)

    @pl.when(jnp.logical_and(pl.program_id(0) == 0, s == 0))
    def _():
        for gi in range(N_GROUPS):
            wm_ref[gi] = _tril_bf16(sguw_ref[gi])

    @pl.when(s == 0)
    def _():
        p_ext[0:HIST_P, :] = jnp.zeros((HIST_P, W_MIX), F32)

    @pl.when(s > 0)
    def _():
        p_ext[0:HIST_P, :] = p_ext[tm:tm + HIST_P, :]

    hb_ref[...] = _rms(x_ref[...], g_ref[...]).astype(BF16)

    def seg(i):
        return _dot(hb_ref[...], win_ref[:, i * W_MIX:(i + 1) * W_MIX])

    p_ext[HIST_P:HIST_P + tm, :] = seg(0)
    u_ref[...] = seg(1)
    vn_ref[...] = _ln(seg(2), lng_ref[...], lnb_ref[...]).astype(BF16)

    def pool_body(c, carry):
        r0 = pl.multiple_of(c * ROW_CHUNK, ROW_CHUNK)
        win = p_ext.at[pl.ds(r0, ROW_CHUNK + HIST_P), :]
        pos = s * tm + r0 + lax.broadcasted_iota(jnp.int32, (ROW_CHUNK, GROUP_W), 0)
        for gi, w in enumerate(POOL_WINDOWS):
            lanes = slice(gi * GROUP_W, (gi + 1) * GROUP_W)
            tok = win[HIST_P:HIST_P + ROW_CHUNK, lanes]
            wsum = tok
            for j in range(1, w):
                wsum = wsum + win[HIST_P - j:HIST_P - j + ROW_CHUNK, lanes]
            cnt = jnp.minimum(pos + 1, w).astype(F32)
            d_ref[pl.ds(r0, ROW_CHUNK), lanes] = (wsum / cnt - tok).astype(BF16)
        return carry

    lax.fori_loop(0, tm // ROW_CHUNK, pool_body, 0)

    for gi in range(N_GROUPS):
        lanes = slice(gi * GROUP_W, (gi + 1) * GROUP_W)
        yc = _dot(d_ref[:, lanes], wpool_ref[gi]) * pscale_ref[:, lanes]
        cat_ref[:, lanes] = yc.astype(BF16)

    for c in range(tm // CHUNK):
        rows = slice(c * CHUNK, (c + 1) * CHUNK)
        for gi in range(N_GROUPS):
            lanes = slice(gi * GROUP_W, (gi + 1) * GROUP_W)
            mixed = _dot(wm_ref[gi], vn_ref[rows, lanes]) + sgubt_ref[:, gi:gi + 1]
            yd = u_ref[rows, lanes] * mixed
            cat_ref[rows, W_MIX + gi * GROUP_W:W_MIX + (gi + 1) * GROUP_W] = yd.astype(BF16)

    o_ref[...] = x_ref[...] + _dot(cat_ref[...], wout_ref[...])

    @pl.when(s == pl.num_programs(1) - 1)
    def _():
        sp_ref[...] = p_ext[HIST_P + tm - POOL_HIST:HIST_P + tm, :]


def _odd_prompt(x, g, win, wpool, pscale, sguw, sgub, lng, lnb, wout, tm, cast_next=()):
    b, s, _ = x.shape
    n_s = s // tm
    tile = pl.BlockSpec((None, tm, D_MODEL), lambda i, j: (i, j, 0))
    n_in = win.shape[1]
    c_in, c_out, c_shapes = _cast_plumbing(cast_next, b * n_s, lambda i, j: i * n_s + j)
    return pl.pallas_call(
        functools.partial(_odd_prompt_kernel, tm=tm, n_cast=len(cast_next)),
        grid=(b, n_s),
        in_specs=[tile, _const_spec((1, D_MODEL)), _const_spec((D_MODEL, n_in)),
                  _const_spec((N_GROUPS, GROUP_W, GROUP_W)), _const_spec((1, W_MIX)),
                  _const_spec((N_GROUPS, CHUNK, CHUNK)), _const_spec((CHUNK, N_GROUPS)),
                  _const_spec((1, W_MIX)), _const_spec((1, W_MIX)),
                  _const_spec((2 * W_MIX, D_MODEL))] + c_in,
        out_specs=[tile, pl.BlockSpec((None, POOL_HIST, W_MIX), lambda i, j: (i, 0, 0))] + c_out,
        out_shape=[jax.ShapeDtypeStruct((b, s, D_MODEL), F32),
                   jax.ShapeDtypeStruct((b, POOL_HIST, W_MIX), F32)] + c_shapes,
        scratch_shapes=[pltpu.VMEM((tm, D_MODEL), BF16),
                        pltpu.VMEM((tm + HIST_P, W_MIX), F32),
                        pltpu.VMEM((tm, W_MIX), F32),
                        pltpu.VMEM((tm, W_MIX), BF16),
                        pltpu.VMEM((tm, W_MIX), BF16),
                        pltpu.VMEM((tm, 2 * W_MIX), BF16),
                        pltpu.VMEM((N_GROUPS, CHUNK, CHUNK), BF16)],
        compiler_params=_params(2),
        name="odd_prompt",
    )(x, g.reshape(1, D_MODEL), win, wpool, pscale.reshape(1, W_MIX), sguw, sgub.T,
      lng.reshape(1, W_MIX), lnb.reshape(1, W_MIX), wout, *(a for a, _ in cast_next))


def _even_sample_kernel(x_ref, g_ref, win_ref, ca_ref, cb_ref, cbb_ref, lng_ref, lnb_ref, wout_ref,
                        ha_ref, hbst_ref,
                        o_ref, sa_ref, sb_ref,
                        hb_ref, gate_ref, ua_ext, ub_ext, cbv_ref, cat_ref, *, t_len, n_seq,
                        x_seq_major):
    if x_seq_major:
        x = jnp.concatenate([x_ref[:, t, :] for t in range(t_len)], axis=0)
    else:
        x = x_ref[...].reshape(t_len * n_seq, D_MODEL)
    hb_ref[...] = _rms(x, g_ref[...]).astype(BF16)

    def seg(i):
        return _dot(hb_ref[...], win_ref[:, i * W_MIX:(i + 1) * W_MIX])

    gate_ref[...] = seg(0)
    for j in range(K_A - 1):
        ua_ext[j] = ha_ref[:, j, :]
    for j in range(K_B - 1):
        ub_ext[j] = hbst_ref[:, j, :]
    ua_ext[K_A - 1:K_A - 1 + t_len] = (seg(1) * seg(2)).reshape(t_len, n_seq, W_MIX)
    ub_ext[K_B - 1:K_B - 1 + t_len] = (seg(3) * jax.nn.sigmoid(seg(4))).reshape(t_len, n_seq, W_MIX)

    def body(t, carry):
        rows = pl.ds(pl.multiple_of(t * n_seq, n_seq), n_seq)
        for gi in range(N_GROUPS):
            lanes = slice(gi * GROUP_W, (gi + 1) * GROUP_W)
            acc = jnp.zeros((n_seq, GROUP_W), F32)
            for k in range(K_A):
                acc = acc + ca_ref[k:k + 1, lanes] * ua_ext[t + k, :, lanes]
            cat_ref[rows, lanes] = (gate_ref[rows, lanes] * acc).astype(BF16)
            acc = jnp.zeros((n_seq, GROUP_W), F32)
            for k in range(K_B):
                acc = acc + cb_ref[k:k + 1, lanes] * ub_ext[t + k, :, lanes]
            cbv_ref[:, lanes] = acc + cbb_ref[:, lanes]
        yb = _silu(_ln(cbv_ref[...], lng_ref[...], lnb_ref[...]))
        cat_ref[rows, W_MIX:2 * W_MIX] = yb.astype(BF16)
        return carry

    lax.fori_loop(0, t_len, body, 0)

    o_ref[...] = (x + _dot(cat_ref[...], wout_ref[...])).reshape(t_len, n_seq, D_MODEL)
    for j in range(K_A - 1):
        sa_ref[:, j, :] = ua_ext[t_len + j]
    for j in range(K_B - 1):
        sb_ref[:, j, :] = ub_ext[t_len + j]


def _seq_block_spec(rows, nb, width):
    return pl.BlockSpec((rows, nb, width), lambda i, *_: (0, i, 0))


def _seq_major_spec(rows, nb, width):
    return pl.BlockSpec((nb, rows, width), lambda i, *_: (i, 0, 0))


def _even_sample(x, g, win, ca, cb, cbb, lng, lnb, wout, hist_a, hist_b, nb, x_seq_major):
    n_seq, t_len = hist_a.shape[0], x.shape[0] * x.shape[1] // hist_a.shape[0]
    m = t_len * nb
    n_in = win.shape[1]
    x_spec = (_seq_major_spec if x_seq_major else _seq_block_spec)(t_len, nb, D_MODEL)
    return pl.pallas_call(
        functools.partial(_even_sample_kernel, t_len=t_len, n_seq=nb, x_seq_major=x_seq_major),
        grid=(n_seq // nb,),
        in_specs=[x_spec, _const_spec((1, D_MODEL)),
                  _const_spec((D_MODEL, n_in)),
                  _const_spec((K_A, W_MIX)), _const_spec((K_B, W_MIX)), _const_spec((1, W_MIX)),
                  _const_spec((1, W_MIX)), _const_spec((1, W_MIX)), _const_spec((2 * W_MIX, D_MODEL)),
                  _seq_major_spec(K_A - 1, nb, W_MIX), _seq_major_spec(K_B - 1, nb, W_MIX)],
        out_specs=[_seq_block_spec(t_len, nb, D_MODEL), _seq_major_spec(K_A - 1, nb, W_MIX),
                   _seq_major_spec(K_B - 1, nb, W_MIX)],
        out_shape=[jax.ShapeDtypeStruct((t_len, n_seq, D_MODEL), F32),
                   jax.ShapeDtypeStruct((n_seq, K_A - 1, W_MIX), F32),
                   jax.ShapeDtypeStruct((n_seq, K_B - 1, W_MIX), F32)],
        scratch_shapes=[pltpu.VMEM((m, D_MODEL), BF16),
                        pltpu.VMEM((m, W_MIX), F32),
                        pltpu.VMEM((t_len + K_A - 1, nb, W_MIX), F32),
                        pltpu.VMEM((t_len + K_B - 1, nb, W_MIX), F32),
                        pltpu.VMEM((nb, W_MIX), F32),
                        pltpu.VMEM((m, 2 * W_MIX), BF16)],
        compiler_params=_params(1),
        name="even_sample",
    )(x, g.reshape(1, D_MODEL), win, ca, cb, cbb.reshape(1, W_MIX), lng.reshape(1, W_MIX),
      lnb.reshape(1, W_MIX), wout, hist_a, hist_b)


def _odd_sample_kernel(sw_ref, sb_ref_smem,
                       x_ref, g_ref, win_ref, wpool_ref, pscale_ref, lng_ref, lnb_ref, wout_ref,
                       hp_ref,
                       o_ref, sp_ref, vn_out_ref,
                       hb_ref, p_ext, u_ref, vn_ref, d_ref, cat_ref, *, t_len, n_seq, start_pos):
    x = x_ref[...].reshape(t_len * n_seq, D_MODEL)
    hb_ref[...] = _rms(x, g_ref[...]).astype(BF16)

    def seg(i):
        return _dot(hb_ref[...], win_ref[:, i * W_MIX:(i + 1) * W_MIX])

    for j in range(POOL_HIST):
        p_ext[j] = hp_ref[:, j, :]
    p_ext[POOL_HIST:POOL_HIST + t_len] = seg(0).reshape(t_len, n_seq, W_MIX)
    u_ref[...] = seg(1).reshape(t_len, n_seq, W_MIX)
    vn_ref[...] = _ln(seg(2), lng_ref[...], lnb_ref[...]).reshape(t_len, n_seq, W_MIX)

    for t in range(t_len):
        rows = slice(t * n_seq, (t + 1) * n_seq)
        for gi, w in enumerate(POOL_WINDOWS):
            lanes = slice(gi * GROUP_W, (gi + 1) * GROUP_W)
            tok = p_ext[POOL_HIST + t, :, lanes]
            wsum = tok
            for j in range(1, w):
                wsum = wsum + p_ext[POOL_HIST + t - j, :, lanes]
            cnt = float(min(start_pos + t + 1, w))
            d_ref[rows, lanes] = (wsum / cnt - tok).astype(BF16)
            mixed = jnp.zeros((n_seq, GROUP_W), F32) + sb_ref_smem[gi * t_len + t]
            for s in range(t + 1):
                mixed = mixed + sw_ref[(gi * t_len + t) * t_len + s] * vn_ref[s, :, lanes]
            cat_ref[rows, W_MIX + gi * GROUP_W:W_MIX + (gi + 1) * GROUP_W] = (
                u_ref[t, :, lanes] * mixed).astype(BF16)

    for gi in range(N_GROUPS):
        lanes = slice(gi * GROUP_W, (gi + 1) * GROUP_W)
        yc = _dot(d_ref[:, lanes], wpool_ref[gi]) * pscale_ref[:, lanes]
        cat_ref[:, lanes] = yc.astype(BF16)

    o_ref[...] = (x + _dot(cat_ref[...], wout_ref[...])).reshape(t_len, n_seq, D_MODEL)
    for j in range(POOL_HIST):
        sp_ref[:, j, :] = p_ext[t_len + j]
    for t in range(t_len):
        vn_out_ref[:, t, :] = vn_ref[t]


def _odd_sample(x, g, win, wpool, pscale, sguw, sgub, lng, lnb, wout, hist_p, nb, start_pos):
    t_len, n_seq, _ = x.shape
    m = t_len * nb
    n_in = win.shape[1]
    assert t_len <= CHUNK
    sw = sguw[:, :t_len, :t_len].reshape(-1)
    sb = sgub[:, :t_len].reshape(-1)

    grid_spec = pltpu.PrefetchScalarGridSpec(
        num_scalar_prefetch=2,
        grid=(n_seq // nb,),
        in_specs=[_seq_block_spec(t_len, nb, D_MODEL), _const_spec((1, D_MODEL)),
                  _const_spec((D_MODEL, n_in)),
                  _const_spec((N_GROUPS, GROUP_W, GROUP_W)), _const_spec((1, W_MIX)),
                  _const_spec((1, W_MIX)), _const_spec((1, W_MIX)),
                  _const_spec((2 * W_MIX, D_MODEL)),
                  _seq_major_spec(POOL_HIST, nb, W_MIX)],
        out_specs=[_seq_block_spec(t_len, nb, D_MODEL), _seq_major_spec(POOL_HIST, nb, W_MIX),
                   _seq_major_spec(t_len, nb, W_MIX)],
        scratch_shapes=[pltpu.VMEM((m, D_MODEL), BF16),
                        pltpu.VMEM((t_len + POOL_HIST, nb, W_MIX), F32),
                        pltpu.VMEM((t_len, nb, W_MIX), F32),
                        pltpu.VMEM((t_len, nb, W_MIX), F32),
                        pltpu.VMEM((m, W_MIX), BF16),
                        pltpu.VMEM((m, 2 * W_MIX), BF16)],
    )
    return pl.pallas_call(
        functools.partial(_odd_sample_kernel, t_len=t_len, n_seq=nb, start_pos=start_pos),
        grid_spec=grid_spec,
        out_shape=[jax.ShapeDtypeStruct((t_len, n_seq, D_MODEL), F32),
                   jax.ShapeDtypeStruct((n_seq, POOL_HIST, W_MIX), F32),
                   jax.ShapeDtypeStruct((n_seq, t_len, W_MIX), F32)],
        compiler_params=_params(1),
        name="odd_sample",
    )(sw, sb, x, g.reshape(1, D_MODEL), win, wpool, pscale.reshape(1, W_MIX),
      lng.reshape(1, W_MIX), lnb.reshape(1, W_MIX), wout, hist_p)


def _time_major(a):
    return jnp.swapaxes(a, 0, 1)


def kernel(x_prompt, x_sample, state_conv_a, state_conv_b, state_pool, norm_mix, norm_ffn, ev_w_in,
           ev_conv_a, ev_conv_b, ev_conv_b_bias, ev_ln_g, ev_ln_b, ev_w_out, od_w_in, od_pool_w,
           od_pool_scale, od_sgu_w, od_sgu_b, od_sgu_ln_g, od_sgu_ln_b, od_w_out, ffn_w1, ffn_w2,
           norm_final):
    depth = norm_mix.shape[0]
    batch, seq, _ = x_prompt.shape
    n_seq, t_len, _ = x_sample.shape
    tm_mix = 512
    tm_ffn = 512
    nb = 32

    def mixer_weights(l):
        i = l // 2
        return ((ev_w_in, i), (ev_w_out, i)) if l % 2 == 0 else ((od_w_in, i), (od_w_out, i))

    xp = x_prompt
    xs = x_sample
    sa_p, sa_s, sb_p, sb_s, sc_p, sc_s, vn_s = [], [], [], [], [], [], []
    win, wout = (w[i].astype(BF16) for w, i in mixer_weights(0))
    for l in range(depth):
        i = l // 2
        ffn_f32 = ((ffn_w1, l), (ffn_w2, l))
        if l % 2 == 0:
            common = (norm_mix[l], win, ev_conv_a[i], ev_conv_b[i], ev_conv_b_bias[i], ev_ln_g[i],
                      ev_ln_b[i], wout)
            xp, a_p, b_p, w1, w2 = _even_prompt(xp, *common, tm=tm_mix, cast_next=ffn_f32)
            xs, a_s, b_s = _even_sample(xs, *common, state_conv_a[i], state_conv_b[i], nb,
                                        x_seq_major=(l == 0))
            sa_p.append(a_p)
            sb_p.append(b_p)
            sa_s.append(a_s)
            sb_s.append(b_s)
        else:
            common = (norm_mix[l], win, od_pool_w[i].astype(BF16), od_pool_scale[i], od_sgu_w[i],
                      od_sgu_b[i], od_sgu_ln_g[i], od_sgu_ln_b[i], wout)
            xp, c_p, w1, w2 = _odd_prompt(xp, *common, tm=tm_mix, cast_next=ffn_f32)
            xs, c_s, v_s = _odd_sample(xs, *common, state_pool[i], nb, PAST_LEN)
            sc_p.append(c_p)
            sc_s.append(c_s)
            vn_s.append(v_s)
        last = l == depth - 1
        xp, xs, *next_w = _ffn(xp.reshape(batch * seq, D_MODEL), xs.reshape(t_len * n_seq, D_MODEL),
                               norm_ffn[l], w1, w2, norm_final if last else None, tm_ffn,
                               cast_next=() if last else mixer_weights(l + 1))
        xp = xp.reshape(batch, seq, D_MODEL)
        xs = xs.reshape(t_len, n_seq, D_MODEL)
        if not last:
            win, wout = next_w

    y_sample = _time_major(xs)
    return (xp, y_sample, jnp.stack(sa_p), jnp.stack(sa_s), jnp.stack(sb_p), jnp.stack(sb_s),
            jnp.stack(sc_p), jnp.stack(sc_s), jnp.stack(vn_s))
```

```python
import functools

import jax
import jax.numpy as jnp
from jax import lax
from jax.experimental import pallas as pl
from jax.experimental.pallas import tpu as pltpu

F32 = jnp.float32
BF16 = jnp.bfloat16

D_MODEL = 1024
W_MIX = D_MODEL // 2
K_A = 3
K_B = 31
POOL_WINDOWS = (2, 4, 8, 16)
POOL_HIST = max(POOL_WINDOWS) - 1
CHUNK = 128
N_GROUPS = 4
GROUP_W = W_MIX // N_GROUPS
D_FF = 4 * D_MODEL
PAST_LEN = 16384
EPS = 1e-6

SUBLANES = 8
HIST_A = 8
HIST_B = 32
HIST_P = 16
ROW_CHUNK = 64
CONV_ROWS = 128
MIX_BLOCK_ROWS = 256
FFN_COL_CHUNK = 1024
VMEM_LIMIT = 56 * 1024 * 1024


def _rms(x, g):
    return x * lax.rsqrt(jnp.mean(x * x, axis=-1, keepdims=True) + EPS) * g


def _ln(x, g, b):
    mu = jnp.mean(x, axis=-1, keepdims=True)
    xc = x - mu
    return xc * lax.rsqrt(jnp.mean(xc * xc, axis=-1, keepdims=True) + EPS) * g + b


def _dot(a, b):
    return jnp.dot(a, b, preferred_element_type=F32)


def _silu(x):
    return x * jax.nn.sigmoid(x)


def _const_spec(shape):
    nd = len(shape)
    return pl.BlockSpec(shape, lambda *_: (0,) * nd, pipeline_mode=pl.Buffered(1))


def _params(n_grid_axes, flags=None):
    return pltpu.CompilerParams(
        dimension_semantics=("arbitrary",) * n_grid_axes,
        vmem_limit_bytes=VMEM_LIMIT,
        flags=flags,
    )


def _cast_plumbing(arrays, n_steps, step_of):
    in_specs, out_specs, shapes = [], [], []
    for stacked, layer in arrays:
        _, n_rows, n_cols = stacked.shape
        rows, rem = divmod(n_rows, n_steps)
        assert rem == 0 and rows % (2 * SUBLANES) == 0, stacked.shape
        in_specs.append(pl.BlockSpec((None, rows, n_cols),
                                     lambda *idx, _l=layer: (_l, step_of(*idx), 0)))
        out_specs.append(pl.BlockSpec((rows, n_cols), lambda *idx: (step_of(*idx), 0)))
        shapes.append(jax.ShapeDtypeStruct((n_rows, n_cols), BF16))
    return in_specs, out_specs, shapes


def _cast_rows(src_refs, dst_refs):
    for src, dst in zip(src_refs, dst_refs, strict=True):
        dst[...] = src[...].astype(BF16)


def _ffn_kernel(*refs, final, n_prompt, n_cast, t_len):
    n_in = 6 if final else 5
    xp_ref, xs_ref, g_ref, w1_ref, w2_ref = refs[:5]
    cast_in = refs[n_in:n_in + n_cast]
    op_ref, os_ref = refs[n_in + n_cast:n_in + n_cast + 2]
    cast_out = refs[n_in + n_cast + 2:n_in + 2 * n_cast + 2]
    hb_ref, act_ref = refs[n_in + 2 * n_cast + 2:]
    step = pl.program_id(0)

    def rows_block(x_ref):
        hb_ref[...] = _rms(x_ref[...], g_ref[...]).astype(BF16)
        for c in range(D_FF // FFN_COL_CHUNK):
            cols = slice(c * FFN_COL_CHUNK, (c + 1) * FFN_COL_CHUNK)
            a = _dot(hb_ref[...], w1_ref[:, cols])
            act_ref[:, cols] = jnp.square(jnp.maximum(a, 0.0)).astype(BF16)
        y = x_ref[...] + _dot(act_ref[...], w2_ref[...])
        return _rms(y, refs[5][...]) if final else y

    @pl.when(step < n_prompt)
    def _():
        op_ref[...] = rows_block(xp_ref)
        _cast_rows(cast_in, cast_out)

    @pl.when(step >= n_prompt)
    def _():
        y = rows_block(xs_ref)
        if final:
            nb = y.shape[0] // t_len
            for t in range(t_len):
                os_ref[:, t, :] = y[t * nb:(t + 1) * nb]
        else:
            os_ref[...] = y


def _ffn(xp2d, xs2d, g, w1, w2, g_final, tm, t_len, cast_next=()):
    n_prompt, n_sample = xp2d.shape[0] // tm, xs2d.shape[0] // tm
    final = g_final is not None
    last = n_prompt - 1
    p_spec = pl.BlockSpec((tm, D_MODEL), lambda i: (jnp.minimum(i, last), 0))
    s_spec = pl.BlockSpec((tm, D_MODEL), lambda i: (jnp.maximum(i - n_prompt, 0), 0))
    in_specs = [p_spec, s_spec, _const_spec((1, D_MODEL)), _const_spec((D_MODEL, D_FF)),
                _const_spec((D_FF, D_MODEL))]
    args = [xp2d, xs2d, g.reshape(1, D_MODEL), w1, w2]
    s_out_spec, s_out_shape = s_spec, jax.ShapeDtypeStruct(xs2d.shape, F32)
    if final:
        in_specs.append(_const_spec((1, D_MODEL)))
        args.append(g_final.reshape(1, D_MODEL))
        nb = tm // t_len
        s_out_spec = pl.BlockSpec((nb, t_len, D_MODEL), lambda i: (jnp.maximum(i - n_prompt, 0), 0, 0))
        s_out_shape = jax.ShapeDtypeStruct((xs2d.shape[0] // t_len, t_len, D_MODEL), F32)
    c_in, c_out, c_shapes = _cast_plumbing(cast_next, n_prompt, lambda i: jnp.minimum(i, last))
    return pl.pallas_call(
        functools.partial(_ffn_kernel, final=final, n_prompt=n_prompt, n_cast=len(cast_next),
                          t_len=t_len),
        grid=(n_prompt + n_sample,),
        in_specs=in_specs + c_in,
        out_specs=[p_spec, s_out_spec] + c_out,
        out_shape=[jax.ShapeDtypeStruct(xp2d.shape, F32), s_out_shape] + c_shapes,
        scratch_shapes=[pltpu.VMEM((tm, D_MODEL), BF16), pltpu.VMEM((tm, D_FF), BF16)],
        compiler_params=_params(1),
        name="ffn_final" if final else "ffn",
    )(*args, *(a for a, _ in cast_next))


def _causal_dwconv(win, w_ref, n_taps, hist, rows, lanes):
    out = None
    for r in range(SUBLANES):
        acc = None
        for q in range(hist // SUBLANES):
            k = n_taps - 1 - SUBLANES * q - r
            if k < 0:
                continue
            start = hist - SUBLANES * (q + 1)
            term = w_ref[k:k + 1, lanes] * win[start:start + rows + SUBLANES, lanes]
            acc = term if acc is None else acc + term
        if acc is None:
            continue
        part = acc[SUBLANES - r:SUBLANES - r + rows]
        out = part if out is None else out + part
    return out


def _even_prompt_kernel(*refs, tm, n_cast):
    x_ref, g_ref, win_ref, ca_ref, cb_ref, cbb_ref, lng_ref, lnb_ref, wout_ref = refs[:9]
    cast_in = refs[9:9 + n_cast]
    o_ref, sa_ref, sb_ref = refs[9 + n_cast:12 + n_cast]
    cast_out = refs[12 + n_cast:12 + 2 * n_cast]
    hb_ref, z_ref, ua_ext, ub_ext, cbv_ref, cat_ref = refs[12 + 2 * n_cast:]
    s = pl.program_id(1)
    _cast_rows(cast_in, cast_out)

    @pl.when(s == 0)
    def _():
        ua_ext[0:HIST_A, :] = jnp.zeros((HIST_A, W_MIX), F32)
        ub_ext[0:HIST_B, :] = jnp.zeros((HIST_B, W_MIX), F32)

    @pl.when(s > 0)
    def _():
        ua_ext[0:HIST_A, :] = ua_ext[tm:tm + HIST_A, :]
        ub_ext[0:HIST_B, :] = ub_ext[tm:tm + HIST_B, :]

    def zseg(rows, i, lanes=slice(0, W_MIX)):
        return z_ref[rows, i * W_MIX + lanes.start:i * W_MIX + lanes.stop]

    def project_pieces(rows):
        def norm():
            hb_ref[rows, :] = _rms(x_ref[rows, :], g_ref[...]).astype(BF16)

        def dot_piece(i):
            cols = slice(i * W_MIX, (i + 1) * W_MIX)
            z_ref[rows, cols] = _dot(hb_ref[rows, :], win_ref[:, cols])

        return [norm] + [functools.partial(dot_piece, i) for i in (3, 4, 1, 2, 0)]

    def mix_pieces(rows):
        def gates(r0):
            crow = slice(r0, r0 + ROW_CHUNK)
            ua_ext[HIST_A + r0:HIST_A + r0 + ROW_CHUNK, :] = zseg(crow, 1) * zseg(crow, 2)
            ub_ext[HIST_B + r0:HIST_B + r0 + ROW_CHUNK, :] = zseg(crow, 3) * jax.nn.sigmoid(zseg(crow, 4))

        def convs(r0, gi):
            crow = slice(r0, r0 + CONV_ROWS)
            lanes = slice(gi * GROUP_W, (gi + 1) * GROUP_W)
            win_a = ua_ext.at[r0:r0 + CONV_ROWS + HIST_A, :]
            win_b = ub_ext.at[r0:r0 + CONV_ROWS + HIST_B, :]
            ya = zseg(crow, 0, lanes) * _causal_dwconv(win_a, ca_ref, K_A, HIST_A, CONV_ROWS, lanes)
            cat_ref[crow, lanes] = ya.astype(BF16)
            cbv_ref[crow, lanes] = _causal_dwconv(win_b, cb_ref, K_B, HIST_B, CONV_ROWS, lanes)

        def norm_act(r0):
            crow = slice(r0, r0 + CONV_ROWS)
            yb = _silu(_ln(cbv_ref[crow, :] + cbb_ref[...], lng_ref[...], lnb_ref[...]))
            cat_ref[crow, W_MIX:2 * W_MIX] = yb.astype(BF16)

        pieces = [functools.partial(gates, r0) for r0 in range(rows.start, rows.stop, ROW_CHUNK)]
        for r0 in range(rows.start, rows.stop, CONV_ROWS):
            pieces += [functools.partial(convs, r0, gi) for gi in range(N_GROUPS)]
            pieces.append(functools.partial(norm_act, r0))
        return pieces

    def out_pieces(rows):
        def piece(j):
            cols = slice(j * W_MIX, (j + 1) * W_MIX)
            o_ref[rows, cols] = x_ref[rows, cols] + _dot(cat_ref[rows, :], wout_ref[:, cols])

        return [functools.partial(piece, j) for j in range(D_MODEL // W_MIX)]

    def emit(mxu_pieces, vpu_pieces):
        n_m, n_v = len(mxu_pieces), len(vpu_pieces)
        i = j = 0
        while i < n_m or j < n_v:
            if j >= n_v or (i < n_m and i * n_v <= j * n_m):
                mxu_pieces[i]()
                i += 1
            else:
                vpu_pieces[j]()
                j += 1

    blocks = [slice(r, r + MIX_BLOCK_ROWS) for r in range(0, tm, MIX_BLOCK_ROWS)]
    emit(project_pieces(blocks[0]), [])
    for i in range(len(blocks)):
        mxu = out_pieces(blocks[i - 1]) if i > 0 else []
        if i + 1 < len(blocks):
            mxu = project_pieces(blocks[i + 1]) + mxu
        emit(mxu, mix_pieces(blocks[i]))
    emit(out_pieces(blocks[-1]), [])

    @pl.when(s == pl.num_programs(1) - 1)
    def _():
        sa_ref[...] = ua_ext[HIST_A + tm - (K_A - 1):HIST_A + tm, :]
        sb_ref[...] = ub_ext[HIST_B + tm - (K_B - 1):HIST_B + tm, :]


def _even_prompt(x, g, win, ca, cb, cbb, lng, lnb, wout, tm, cast_next=()):
    b, s, _ = x.shape
    n_s = s // tm
    tile = pl.BlockSpec((None, tm, D_MODEL), lambda i, j: (i, j, 0))
    n_in = win.shape[1]
    c_in, c_out, c_shapes = _cast_plumbing(cast_next, b * n_s, lambda i, j: i * n_s + j)
    return pl.pallas_call(
        functools.partial(_even_prompt_kernel, tm=tm, n_cast=len(cast_next)),
        grid=(b, n_s),
        in_specs=[tile, _const_spec((1, D_MODEL)), _const_spec((D_MODEL, n_in)),
                  _const_spec((K_A, W_MIX)), _const_spec((K_B, W_MIX)), _const_spec((1, W_MIX)),
                  _const_spec((1, W_MIX)), _const_spec((1, W_MIX)),
                  _const_spec((2 * W_MIX, D_MODEL))] + c_in,
        out_specs=[tile,
                   pl.BlockSpec((None, K_A - 1, W_MIX), lambda i, j: (i, 0, 0)),
                   pl.BlockSpec((None, K_B - 1, W_MIX), lambda i, j: (i, 0, 0))] + c_out,
        out_shape=[jax.ShapeDtypeStruct((b, s, D_MODEL), F32),
                   jax.ShapeDtypeStruct((b, K_A - 1, W_MIX), F32),
                   jax.ShapeDtypeStruct((b, K_B - 1, W_MIX), F32)] + c_shapes,
        scratch_shapes=[pltpu.VMEM((tm, D_MODEL), BF16),
                        pltpu.VMEM((tm, n_in), F32),
                        pltpu.VMEM((tm + HIST_A, W_MIX), F32),
                        pltpu.VMEM((tm + HIST_B, W_MIX), F32),
                        pltpu.VMEM((tm, W_MIX), F32),
                        pltpu.VMEM((tm, 2 * W_MIX), BF16)],
        compiler_params=_params(2),
        name="even_prompt",
    )(x, g.reshape(1, D_MODEL), win, ca, cb, cbb.reshape(1, W_MIX), lng.reshape(1, W_MIX),
      lnb.reshape(1, W_MIX), wout, *(a for a, _ in cast_next))


def _tril_bf16(w):
    t = lax.broadcasted_iota(jnp.int32, (CHUNK, CHUNK), 0)
    s = lax.broadcasted_iota(jnp.int32, (CHUNK, CHUNK), 1)
    return jnp.where(s <= t, w, 0.0).astype(BF16)


def _odd_prompt_kernel(*refs, tm, n_cast):
    (x_ref, g_ref, win_ref, wpool_ref, pscale_ref, sguw_ref, sgubt_ref, lng_ref, lnb_ref,
     wout_ref) = refs[:10]
    cast_in = refs[10:10 + n_cast]
    o_ref, sp_ref = refs[10 + n_cast:12 + n_cast]
    cast_out = refs[12 + n_cast:12 + 2 * n_cast]
    hb_ref, p_ext, u_ref, vn_ref, d_ref, cat_ref, wm_ref = refs[12 + 2 * n_cast:]
    s = pl.program_id(1)
    _cast_rows(cast_in, cast_out)

    @pl.when(jnp.logical_and(pl.program_id(0) == 0, s == 0))
    def _():
        for gi in range(N_GROUPS):
            wm_ref[gi] = _tril_bf16(sguw_ref[gi])

    @pl.when(s == 0)
    def _():
        p_ext[0:HIST_P, :] = jnp.zeros((HIST_P, W_MIX), F32)

    @pl.when(s > 0)
    def _():
        p_ext[0:HIST_P, :] = p_ext[tm:tm + HIST_P, :]

    hb_ref[...] = _rms(x_ref[...], g_ref[...]).astype(BF16)

    def seg(i):
        return _dot(hb_ref[...], win_ref[:, i * W_MIX:(i + 1) * W_MIX])

    p_ext[HIST_P:HIST_P + tm, :] = seg(0)
    u_ref[...] = seg(1)
    vn_ref[...] = _ln(seg(2), lng_ref[...], lnb_ref[...]).astype(BF16)

    def pool_body(c, carry):
        r0 = pl.multiple_of(c * ROW_CHUNK, ROW_CHUNK)
        win = p_ext.at[pl.ds(r0, ROW_CHUNK + HIST_P), :]
        pos = s * tm + r0 + lax.broadcasted_iota(jnp.int32, (ROW_CHUNK, GROUP_W), 0)
        for gi, w in enumerate(POOL_WINDOWS):
            lanes = slice(gi * GROUP_W, (gi + 1) * GROUP_W)
            tok = win[HIST_P:HIST_P + ROW_CHUNK, lanes]
            wsum = tok
            for j in range(1, w):
                wsum = wsum + win[HIST_P - j:HIST_P - j + ROW_CHUNK, lanes]
            cnt = jnp.minimum(pos + 1, w).astype(F32)
            d_ref[pl.ds(r0, ROW_CHUNK), lanes] = (wsum / cnt - tok).astype(BF16)
        return carry

    lax.fori_loop(0, tm // ROW_CHUNK, pool_body, 0)

    for gi in range(N_GROUPS):
        lanes = slice(gi * GROUP_W, (gi + 1) * GROUP_W)
        yc = _dot(d_ref[:, lanes], wpool_ref[gi]) * pscale_ref[:, lanes]
        cat_ref[:, lanes] = yc.astype(BF16)

    for c in range(tm // CHUNK):
        rows = slice(c * CHUNK, (c + 1) * CHUNK)
        for gi in range(N_GROUPS):
            lanes = slice(gi * GROUP_W, (gi + 1) * GROUP_W)
            mixed = _dot(wm_ref[gi], vn_ref[rows, lanes]) + sgubt_ref[:, gi:gi + 1]
            yd = u_ref[rows, lanes] * mixed
            cat_ref[rows, W_MIX + gi * GROUP_W:W_MIX + (gi + 1) * GROUP_W] = yd.astype(BF16)

    o_ref[...] = x_ref[...] + _dot(cat_ref[...], wout_ref[...])

    @pl.when(s == pl.num_programs(1) - 1)
    def _():
        sp_ref[...] = p_ext[HIST_P + tm - POOL_HIST:HIST_P + tm, :]


def _odd_prompt(x, g, win, wpool, pscale, sguw, sgub, lng, lnb, wout, tm, cast_next=()):
    b, s, _ = x.shape
    n_s = s // tm
    tile = pl.BlockSpec((None, tm, D_MODEL), lambda i, j: (i, j, 0))
    n_in = win.shape[1]
    c_in, c_out, c_shapes = _cast_plumbing(cast_next, b * n_s, lambda i, j: i * n_s + j)
    return pl.pallas_call(
        functools.partial(_odd_prompt_kernel, tm=tm, n_cast=len(cast_next)),
        grid=(b, n_s),
        in_specs=[tile, _const_spec((1, D_MODEL)), _const_spec((D_MODEL, n_in)),
                  _const_spec((N_GROUPS, GROUP_W, GROUP_W)), _const_spec((1, W_MIX)),
                  _const_spec((N_GROUPS, CHUNK, CHUNK)), _const_spec((CHUNK, N_GROUPS)),
                  _const_spec((1, W_MIX)), _const_spec((1, W_MIX)),
                  _const_spec((2 * W_MIX, D_MODEL))] + c_in,
        out_specs=[tile, pl.BlockSpec((None, POOL_HIST, W_MIX), lambda i, j: (i, 0, 0))] + c_out,
        out_shape=[jax.ShapeDtypeStruct((b, s, D_MODEL), F32),
                   jax.ShapeDtypeStruct((b, POOL_HIST, W_MIX), F32)] + c_shapes,
        scratch_shapes=[pltpu.VMEM((tm, D_MODEL), BF16),
                        pltpu.VMEM((tm + HIST_P, W_MIX), F32),
                        pltpu.VMEM((tm, W_MIX), F32),
                        pltpu.VMEM((tm, W_MIX), BF16),
                        pltpu.VMEM((tm, W_MIX), BF16),
                        pltpu.VMEM((tm, 2 * W_MIX), BF16),
                        pltpu.VMEM((N_GROUPS, CHUNK, CHUNK), BF16)],
        compiler_params=_params(2),
        name="odd_prompt",
    )(x, g.reshape(1, D_MODEL), win, wpool, pscale.reshape(1, W_MIX), sguw, sgub.T,
      lng.reshape(1, W_MIX), lnb.reshape(1, W_MIX), wout, *(a for a, _ in cast_next))


def _even_sample_kernel(x_ref, g_ref, win_ref, ca_ref, cb_ref, cbb_ref, lng_ref, lnb_ref, wout_ref,
                        ha_ref, hbst_ref,
                        o_ref, sa_ref, sb_ref,
                        hb_ref, gate_ref, ua_ext, ub_ext, cbv_ref, cat_ref, *, t_len, n_seq,
                        x_seq_major):
    if x_seq_major:
        x = jnp.concatenate([x_ref[:, t, :] for t in range(t_len)], axis=0)
    else:
        x = x_ref[...].reshape(t_len * n_seq, D_MODEL)
    hb_ref[...] = _rms(x, g_ref[...]).astype(BF16)

    def seg(i):
        return _dot(hb_ref[...], win_ref[:, i * W_MIX:(i + 1) * W_MIX])

    gate_ref[...] = seg(0)
    for j in range(K_A - 1):
        ua_ext[j] = ha_ref[:, j, :]
    ub_ext[0:K_B - 1] = hbst_ref[...]
    ua_ext[K_A - 1:K_A - 1 + t_len] = (seg(1) * seg(2)).reshape(t_len, n_seq, W_MIX)
    ub_ext[K_B - 1:K_B - 1 + t_len] = (seg(3) * jax.nn.sigmoid(seg(4))).reshape(t_len, n_seq, W_MIX)

    def body(t, carry):
        rows = pl.ds(pl.multiple_of(t * n_seq, n_seq), n_seq)
        for gi in range(N_GROUPS):
            lanes = slice(gi * GROUP_W, (gi + 1) * GROUP_W)
            acc = jnp.zeros((n_seq, GROUP_W), F32)
            for k in range(K_A):
                acc = acc + ca_ref[k:k + 1, lanes] * ua_ext[t + k, :, lanes]
            cat_ref[rows, lanes] = (gate_ref[rows, lanes] * acc).astype(BF16)
            acc = jnp.zeros((n_seq, GROUP_W), F32)
            for k in range(K_B):
                acc = acc + cb_ref[k:k + 1, lanes] * ub_ext[t + k, :, lanes]
            cbv_ref[:, lanes] = acc + cbb_ref[:, lanes]
        yb = _silu(_ln(cbv_ref[...], lng_ref[...], lnb_ref[...]))
        cat_ref[rows, W_MIX:2 * W_MIX] = yb.astype(BF16)
        return carry

    lax.fori_loop(0, t_len, body, 0)

    o_ref[...] = (x + _dot(cat_ref[...], wout_ref[...])).reshape(t_len, n_seq, D_MODEL)
    for j in range(K_A - 1):
        sa_ref[:, j, :] = ua_ext[t_len + j]
    sb_ref[...] = ub_ext[t_len:t_len + K_B - 1]


def _seq_block_spec(rows, nb, width):
    return pl.BlockSpec((rows, nb, width), lambda i, *_: (0, i, 0))


def _seq_major_spec(rows, nb, width):
    return pl.BlockSpec((nb, rows, width), lambda i, *_: (i, 0, 0))


def _xs_block_spec(t_len, nb):
    return pl.BlockSpec((None, t_len, nb, D_MODEL), lambda i, *_: (i, 0, 0, 0))


def _even_sample(x, g, win, ca, cb, cbb, lng, lnb, wout, hist_a, hist_b, nb, x_seq_major):
    n_seq, t_len = hist_a.shape[0], x.size // (hist_a.shape[0] * D_MODEL)
    m = t_len * nb
    n_in = win.shape[1]
    x_spec = _seq_major_spec(t_len, nb, D_MODEL) if x_seq_major else _xs_block_spec(t_len, nb)
    return pl.pallas_call(
        functools.partial(_even_sample_kernel, t_len=t_len, n_seq=nb, x_seq_major=x_seq_major),
        grid=(n_seq // nb,),
        in_specs=[x_spec, _const_spec((1, D_MODEL)),
                  _const_spec((D_MODEL, n_in)),
                  _const_spec((K_A, W_MIX)), _const_spec((K_B, W_MIX)), _const_spec((1, W_MIX)),
                  _const_spec((1, W_MIX)), _const_spec((1, W_MIX)), _const_spec((2 * W_MIX, D_MODEL)),
                  _seq_major_spec(K_A - 1, nb, W_MIX), _seq_block_spec(K_B - 1, nb, W_MIX)],
        out_specs=[_xs_block_spec(t_len, nb), _seq_major_spec(K_A - 1, nb, W_MIX),
                   _seq_block_spec(K_B - 1, nb, W_MIX)],
        out_shape=[jax.ShapeDtypeStruct((n_seq // nb, t_len, nb, D_MODEL), F32),
                   jax.ShapeDtypeStruct((n_seq, K_A - 1, W_MIX), F32),
                   jax.ShapeDtypeStruct((K_B - 1, n_seq, W_MIX), F32)],
        scratch_shapes=[pltpu.VMEM((m, D_MODEL), BF16),
                        pltpu.VMEM((m, W_MIX), F32),
                        pltpu.VMEM((t_len + K_A - 1, nb, W_MIX), F32),
                        pltpu.VMEM((t_len + K_B - 1, nb, W_MIX), F32),
                        pltpu.VMEM((nb, W_MIX), F32),
                        pltpu.VMEM((m, 2 * W_MIX), BF16)],
        compiler_params=_params(1),
        name="even_sample",
    )(x, g.reshape(1, D_MODEL), win, ca, cb, cbb.reshape(1, W_MIX), lng.reshape(1, W_MIX),
      lnb.reshape(1, W_MIX), wout, hist_a, hist_b)


def _odd_sample_kernel(sw_ref, sb_ref_smem,
                       x_ref, g_ref, win_ref, wpool_ref, pscale_ref, lng_ref, lnb_ref, wout_ref,
                       hp_ref,
                       o_ref, sp_ref, vn_out_ref,
                       hb_ref, p_ext, u_ref, vn_ref, d_ref, cat_ref, *, t_len, n_seq, start_pos):
    x = x_ref[...].reshape(t_len * n_seq, D_MODEL)
    hb_ref[...] = _rms(x, g_ref[...]).astype(BF16)

    def seg(i):
        return _dot(hb_ref[...], win_ref[:, i * W_MIX:(i + 1) * W_MIX])

    p_ext[0:POOL_HIST] = hp_ref[...]
    p_ext[POOL_HIST:POOL_HIST + t_len] = seg(0).reshape(t_len, n_seq, W_MIX)
    u_ref[...] = seg(1).reshape(t_len, n_seq, W_MIX)
    vn_ref[...] = _ln(seg(2), lng_ref[...], lnb_ref[...]).reshape(t_len, n_seq, W_MIX)

    for t in range(t_len):
        rows = slice(t * n_seq, (t + 1) * n_seq)
        for gi, w in enumerate(POOL_WINDOWS):
            lanes = slice(gi * GROUP_W, (gi + 1) * GROUP_W)
            tok = p_ext[POOL_HIST + t, :, lanes]
            wsum = tok
            for j in range(1, w):
                wsum = wsum + p_ext[POOL_HIST + t - j, :, lanes]
            cnt = float(min(start_pos + t + 1, w))
            d_ref[rows, lanes] = (wsum / cnt - tok).astype(BF16)
            mixed = jnp.zeros((n_seq, GROUP_W), F32) + sb_ref_smem[gi * t_len + t]
            for s in range(t + 1):
                mixed = mixed + sw_ref[(gi * t_len + t) * t_len + s] * vn_ref[s, :, lanes]
            cat_ref[rows, W_MIX + gi * GROUP_W:W_MIX + (gi + 1) * GROUP_W] = (
                u_ref[t, :, lanes] * mixed).astype(BF16)

    for gi in range(N_GROUPS):
        lanes = slice(gi * GROUP_W, (gi + 1) * GROUP_W)
        yc = _dot(d_ref[:, lanes], wpool_ref[gi]) * pscale_ref[:, lanes]
        cat_ref[:, lanes] = yc.astype(BF16)

    o_ref[...] = (x + _dot(cat_ref[...], wout_ref[...])).reshape(t_len, n_seq, D_MODEL)
    sp_ref[...] = p_ext[t_len:t_len + POOL_HIST]
    for t in range(t_len):
        vn_out_ref[:, t, :] = vn_ref[t]


def _odd_sample(x, g, win, wpool, pscale, sguw, sgub, lng, lnb, wout, hist_p, nb, start_pos):
    n_blocks, t_len, _, _ = x.shape
    n_seq = n_blocks * nb
    m = t_len * nb
    n_in = win.shape[1]
    assert t_len <= CHUNK
    sw = sguw[:, :t_len, :t_len].reshape(-1)
    sb = sgub[:, :t_len].reshape(-1)

    grid_spec = pltpu.PrefetchScalarGridSpec(
        num_scalar_prefetch=2,
        grid=(n_seq // nb,),
        in_specs=[_xs_block_spec(t_len, nb), _const_spec((1, D_MODEL)),
                  _const_spec((D_MODEL, n_in)),
                  _const_spec((N_GROUPS, GROUP_W, GROUP_W)), _const_spec((1, W_MIX)),
                  _const_spec((1, W_MIX)), _const_spec((1, W_MIX)),
                  _const_spec((2 * W_MIX, D_MODEL)),
                  _seq_block_spec(POOL_HIST, nb, W_MIX)],
        out_specs=[_xs_block_spec(t_len, nb), _seq_block_spec(POOL_HIST, nb, W_MIX),
                   _seq_major_spec(t_len, nb, W_MIX)],
        scratch_shapes=[pltpu.VMEM((m, D_MODEL), BF16),
                        pltpu.VMEM((t_len + POOL_HIST, nb, W_MIX), F32),
                        pltpu.VMEM((t_len, nb, W_MIX), F32),
                        pltpu.VMEM((t_len, nb, W_MIX), F32),
                        pltpu.VMEM((m, W_MIX), BF16),
                        pltpu.VMEM((m, 2 * W_MIX), BF16)],
    )
    return pl.pallas_call(
        functools.partial(_odd_sample_kernel, t_len=t_len, n_seq=nb, start_pos=start_pos),
        grid_spec=grid_spec,
        out_shape=[jax.ShapeDtypeStruct((n_blocks, t_len, nb, D_MODEL), F32),
                   jax.ShapeDtypeStruct((POOL_HIST, n_seq, W_MIX), F32),
                   jax.ShapeDtypeStruct((n_seq, t_len, W_MIX), F32)],
        compiler_params=_params(1),
        name="odd_sample",
    )(sw, sb, x, g.reshape(1, D_MODEL), win, wpool, pscale.reshape(1, W_MIX),
      lng.reshape(1, W_MIX), lnb.reshape(1, W_MIX), wout, hist_p)


def _time_major(a):
    return jnp.swapaxes(a, 0, 1)


def kernel(x_prompt, x_sample, state_conv_a, state_conv_b, state_pool, norm_mix, norm_ffn, ev_w_in,
           ev_conv_a, ev_conv_b, ev_conv_b_bias, ev_ln_g, ev_ln_b, ev_w_out, od_w_in, od_pool_w,
           od_pool_scale, od_sgu_w, od_sgu_b, od_sgu_ln_g, od_sgu_ln_b, od_w_out, ffn_w1, ffn_w2,
           norm_final):
    depth = norm_mix.shape[0]
    batch, seq, _ = x_prompt.shape
    n_seq, t_len, _ = x_sample.shape
    tm_mix = 512
    tm_ffn = 512
    nb = tm_ffn // t_len

    def mixer_weights(l):
        i = l // 2
        return ((ev_w_in, i), (ev_w_out, i)) if l % 2 == 0 else ((od_w_in, i), (od_w_out, i))

    xp = x_prompt
    xs = x_sample
    sa_p, sa_s, sb_p, sb_s, sc_p, sc_s, vn_s = [], [], [], [], [], [], []
    win, wout = (w[i].astype(BF16) for w, i in mixer_weights(0))
    for l in range(depth):
        i = l // 2
        ffn_f32 = ((ffn_w1, l), (ffn_w2, l))
        if l % 2 == 0:
            common = (norm_mix[l], win, ev_conv_a[i], ev_conv_b[i], ev_conv_b_bias[i], ev_ln_g[i],
                      ev_ln_b[i], wout)
            xp, a_p, b_p, w1, w2 = _even_prompt(xp, *common, tm=tm_mix, cast_next=ffn_f32)
            xs, a_s, b_s = _even_sample(xs, *common, state_conv_a[i], _time_major(state_conv_b[i]),
                                        nb, x_seq_major=(l == 0))
            sa_p.append(a_p)
            sb_p.append(b_p)
            sa_s.append(a_s)
            sb_s.append(_time_major(b_s))
        else:
            common = (norm_mix[l], win, od_pool_w[i].astype(BF16), od_pool_scale[i], od_sgu_w[i],
                      od_sgu_b[i], od_sgu_ln_g[i], od_sgu_ln_b[i], wout)
            xp, c_p, w1, w2 = _odd_prompt(xp, *common, tm=tm_mix, cast_next=ffn_f32)
            xs, c_s, v_s = _odd_sample(xs, *common, _time_major(state_pool[i]), nb, PAST_LEN)
            sc_p.append(c_p)
            sc_s.append(_time_major(c_s))
            vn_s.append(v_s)
        last = l == depth - 1
        xp, xs, *next_w = _ffn(xp.reshape(batch * seq, D_MODEL), xs.reshape(t_len * n_seq, D_MODEL),
                               norm_ffn[l], w1, w2, norm_final if last else None, tm_ffn, t_len,
                               cast_next=() if last else mixer_weights(l + 1))
        xp = xp.reshape(batch, seq, D_MODEL)
        if not last:
            xs = xs.reshape(n_seq // nb, t_len, nb, D_MODEL)
            win, wout = next_w

    return (xp, xs, jnp.stack(sa_p), jnp.stack(sa_s), jnp.stack(sb_p), jnp.stack(sb_s),
            jnp.stack(sc_p), jnp.stack(sc_s), jnp.stack(vn_s))
```

```python
import functools

import jax
import jax.numpy as jnp
from jax import lax
from jax.experimental import pallas as pl
from jax.experimental.pallas import tpu as pltpu

F32 = jnp.float32
BF16 = jnp.bfloat16

D_MODEL = 1024
W_MIX = D_MODEL // 2
K_A = 3
K_B = 31
POOL_WINDOWS = (2, 4, 8, 16)
POOL_HIST = max(POOL_WINDOWS) - 1
CHUNK = 128
N_GROUPS = 4
GROUP_W = W_MIX // N_GROUPS
D_FF = 4 * D_MODEL
PAST_LEN = 16384
EPS = 1e-6

SUBLANES = 8
HIST_A = 8
HIST_B = 32
HIST_P = SUBLANES * len(POOL_WINDOWS)
ROW_CHUNK = 64
CONV_ROWS = 128
MIX_BLOCK_ROWS = 256
FFN_COL_CHUNK = 1024
VMEM_LIMIT = 56 * 1024 * 1024


def _rms(x, g):
    return x * lax.rsqrt(jnp.mean(x * x, axis=-1, keepdims=True) + EPS) * g


def _ln(x, g, b):
    mu = jnp.mean(x, axis=-1, keepdims=True)
    xc = x - mu
    return xc * lax.rsqrt(jnp.mean(xc * xc, axis=-1, keepdims=True) + EPS) * g + b


def _dot(a, b):
    return jnp.dot(a, b, preferred_element_type=F32)


def _silu(x):
    return x * jax.nn.sigmoid(x)


def _const_spec(shape):
    nd = len(shape)
    return pl.BlockSpec(shape, lambda *_: (0,) * nd, pipeline_mode=pl.Buffered(1))


def _params(n_grid_axes):
    return pltpu.CompilerParams(
        dimension_semantics=("arbitrary",) * n_grid_axes,
        vmem_limit_bytes=VMEM_LIMIT,
    )


def _cast_plumbing(arrays, n_steps, step_of):
    in_specs, out_specs, shapes = [], [], []
    for stacked, layer in arrays:
        _, n_rows, n_cols = stacked.shape
        rows, rem = divmod(n_rows, n_steps)
        assert rem == 0 and rows % (2 * SUBLANES) == 0, stacked.shape
        in_specs.append(pl.BlockSpec((None, rows, n_cols),
                                     lambda *idx, _l=layer: (_l, step_of(*idx), 0)))
        out_specs.append(pl.BlockSpec((rows, n_cols), lambda *idx: (step_of(*idx), 0)))
        shapes.append(jax.ShapeDtypeStruct((n_rows, n_cols), BF16))
    return in_specs, out_specs, shapes


def _cast_rows(src_refs, dst_refs):
    for src, dst in zip(src_refs, dst_refs, strict=True):
        dst[...] = src[...].astype(BF16)


def _ffn_kernel(*refs, final, n_prompt, n_cast, t_len):
    n_in = 6 if final else 5
    xp_ref, xs_ref, g_ref, w1_ref, w2_ref = refs[:5]
    cast_in = refs[n_in:n_in + n_cast]
    op_ref, os_ref = refs[n_in + n_cast:n_in + n_cast + 2]
    cast_out = refs[n_in + n_cast + 2:n_in + 2 * n_cast + 2]
    hb_ref, act_ref = refs[n_in + 2 * n_cast + 2:]
    step = pl.program_id(0)

    def rows_block(x_ref):
        hb_ref[...] = _rms(x_ref[...], g_ref[...]).astype(BF16)
        for c in range(D_FF // FFN_COL_CHUNK):
            cols = slice(c * FFN_COL_CHUNK, (c + 1) * FFN_COL_CHUNK)
            a = _dot(hb_ref[...], w1_ref[:, cols])
            act_ref[:, cols] = jnp.square(jnp.maximum(a, 0.0)).astype(BF16)
        y = x_ref[...] + _dot(act_ref[...], w2_ref[...])
        return _rms(y, refs[5][...]) if final else y

    @pl.when(step < n_prompt)
    def _():
        op_ref[...] = rows_block(xp_ref)
        _cast_rows(cast_in, cast_out)

    @pl.when(step >= n_prompt)
    def _():
        y = rows_block(xs_ref)
        if final:
            nb = y.shape[0] // t_len
            for t in range(t_len):
                os_ref[:, t, :] = y[t * nb:(t + 1) * nb]
        else:
            os_ref[...] = y


def _ffn(xp2d, xs2d, g, w1, w2, g_final, tm, t_len, cast_next=()):
    n_prompt, n_sample = xp2d.shape[0] // tm, xs2d.shape[0] // tm
    final = g_final is not None
    last = n_prompt - 1
    p_spec = pl.BlockSpec((tm, D_MODEL), lambda i: (jnp.minimum(i, last), 0))
    s_spec = pl.BlockSpec((tm, D_MODEL), lambda i: (jnp.maximum(i - n_prompt, 0), 0))
    in_specs = [p_spec, s_spec, _const_spec((1, D_MODEL)), _const_spec((D_MODEL, D_FF)),
                _const_spec((D_FF, D_MODEL))]
    args = [xp2d, xs2d, g.reshape(1, D_MODEL), w1, w2]
    s_out_spec, s_out_shape = s_spec, jax.ShapeDtypeStruct(xs2d.shape, F32)
    if final:
        in_specs.append(_const_spec((1, D_MODEL)))
        args.append(g_final.reshape(1, D_MODEL))
        nb = tm // t_len
        s_out_spec = pl.BlockSpec((nb, t_len, D_MODEL), lambda i: (jnp.maximum(i - n_prompt, 0), 0, 0))
        s_out_shape = jax.ShapeDtypeStruct((xs2d.shape[0] // t_len, t_len, D_MODEL), F32)
    c_in, c_out, c_shapes = _cast_plumbing(cast_next, n_prompt, lambda i: jnp.minimum(i, last))
    return pl.pallas_call(
        functools.partial(_ffn_kernel, final=final, n_prompt=n_prompt, n_cast=len(cast_next),
                          t_len=t_len),
        grid=(n_prompt + n_sample,),
        in_specs=in_specs + c_in,
        out_specs=[p_spec, s_out_spec] + c_out,
        out_shape=[jax.ShapeDtypeStruct(xp2d.shape, F32), s_out_shape] + c_shapes,
        scratch_shapes=[pltpu.VMEM((tm, D_MODEL), BF16), pltpu.VMEM((tm, D_FF), BF16)],
        compiler_params=_params(1),
        name="ffn_final" if final else "ffn",
    )(*args, *(a for a, _ in cast_next))


def _causal_dwconv(win, w_ref, n_taps, hist, rows, lanes):
    out = None
    for r in range(SUBLANES):
        acc = None
        for q in range(hist // SUBLANES):
            k = n_taps - 1 - SUBLANES * q - r
            if k < 0:
                continue
            start = hist - SUBLANES * (q + 1)
            term = w_ref[k:k + 1, lanes] * win[start:start + rows + SUBLANES, lanes]
            acc = term if acc is None else acc + term
        if acc is None:
            continue
        part = acc[SUBLANES - r:SUBLANES - r + rows]
        out = part if out is None else out + part
    return out


def _even_prompt_kernel(*refs, tm, n_cast):
    x_ref, g_ref, win_ref, ca_ref, cb_ref, cbb_ref, lng_ref, lnb_ref, wout_ref = refs[:9]
    cast_in = refs[9:9 + n_cast]
    o_ref, sa_ref, sb_ref = refs[9 + n_cast:12 + n_cast]
    cast_out = refs[12 + n_cast:12 + 2 * n_cast]
    hb_ref, z_ref, ua_ext, ub_ext, cbv_ref, cat_ref = refs[12 + 2 * n_cast:]
    s = pl.program_id(1)
    _cast_rows(cast_in, cast_out)

    @pl.when(s == 0)
    def _():
        ua_ext[0:HIST_A, :] = jnp.zeros((HIST_A, W_MIX), F32)
        ub_ext[0:HIST_B, :] = jnp.zeros((HIST_B, W_MIX), F32)

    @pl.when(s > 0)
    def _():
        ua_ext[0:HIST_A, :] = ua_ext[tm:tm + HIST_A, :]
        ub_ext[0:HIST_B, :] = ub_ext[tm:tm + HIST_B, :]

    def zseg(rows, i, lanes=slice(0, W_MIX)):
        return z_ref[rows, i * W_MIX + lanes.start:i * W_MIX + lanes.stop]

    def project_pieces(rows):
        def norm():
            hb_ref[rows, :] = _rms(x_ref[rows, :], g_ref[...]).astype(BF16)

        def dot_piece(i):
            cols = slice(i * W_MIX, (i + 1) * W_MIX)
            z_ref[rows, cols] = _dot(hb_ref[rows, :], win_ref[:, cols])

        return [norm] + [functools.partial(dot_piece, i) for i in (3, 4, 1, 2, 0)]

    def mix_pieces(rows):
        def gates(r0):
            crow = slice(r0, r0 + ROW_CHUNK)
            ua_ext[HIST_A + r0:HIST_A + r0 + ROW_CHUNK, :] = zseg(crow, 1) * zseg(crow, 2)
            ub_ext[HIST_B + r0:HIST_B + r0 + ROW_CHUNK, :] = zseg(crow, 3) * jax.nn.sigmoid(zseg(crow, 4))

        def convs(r0, gi):
            crow = slice(r0, r0 + CONV_ROWS)
            lanes = slice(gi * GROUP_W, (gi + 1) * GROUP_W)
            win_a = ua_ext.at[r0:r0 + CONV_ROWS + HIST_A, :]
            win_b = ub_ext.at[r0:r0 + CONV_ROWS + HIST_B, :]
            ya = zseg(crow, 0, lanes) * _causal_dwconv(win_a, ca_ref, K_A, HIST_A, CONV_ROWS, lanes)
            cat_ref[crow, lanes] = ya.astype(BF16)
            cbv_ref[crow, lanes] = _causal_dwconv(win_b, cb_ref, K_B, HIST_B, CONV_ROWS, lanes)

        def norm_act(r0):
            crow = slice(r0, r0 + CONV_ROWS)
            yb = _silu(_ln(cbv_ref[crow, :] + cbb_ref[...], lng_ref[...], lnb_ref[...]))
            cat_ref[crow, W_MIX:2 * W_MIX] = yb.astype(BF16)

        pieces = [functools.partial(gates, r0) for r0 in range(rows.start, rows.stop, ROW_CHUNK)]
        for r0 in range(rows.start, rows.stop, CONV_ROWS):
            pieces += [functools.partial(convs, r0, gi) for gi in range(N_GROUPS)]
            pieces.append(functools.partial(norm_act, r0))
        return pieces

    def out_pieces(rows):
        def piece(j):
            cols = slice(j * W_MIX, (j + 1) * W_MIX)
            o_ref[rows, cols] = x_ref[rows, cols] + _dot(cat_ref[rows, :], wout_ref[:, cols])

        return [functools.partial(piece, j) for j in range(D_MODEL // W_MIX)]

    def emit(mxu_pieces, vpu_pieces):
        n_m, n_v = len(mxu_pieces), len(vpu_pieces)
        i = j = 0
        while i < n_m or j < n_v:
            if j >= n_v or (i < n_m and i * n_v <= j * n_m):
                mxu_pieces[i]()
                i += 1
            else:
                vpu_pieces[j]()
                j += 1

    blocks = [slice(r, r + MIX_BLOCK_ROWS) for r in range(0, tm, MIX_BLOCK_ROWS)]
    emit(project_pieces(blocks[0]), [])
    for i in range(len(blocks)):
        mxu = out_pieces(blocks[i - 1]) if i > 0 else []
        if i + 1 < len(blocks):
            mxu = project_pieces(blocks[i + 1]) + mxu
        emit(mxu, mix_pieces(blocks[i]))
    emit(out_pieces(blocks[-1]), [])

    @pl.when(s == pl.num_programs(1) - 1)
    def _():
        sa_ref[...] = ua_ext[HIST_A + tm - (K_A - 1):HIST_A + tm, :]
        sb_ref[...] = ub_ext[HIST_B + tm - (K_B - 1):HIST_B + tm, :]


def _even_prompt(x, g, win, ca, cb, cbb, lng, lnb, wout, tm, cast_next=()):
    b, s, _ = x.shape
    n_s = s // tm
    tile = pl.BlockSpec((None, tm, D_MODEL), lambda i, j: (i, j, 0))
    n_in = win.shape[1]
    c_in, c_out, c_shapes = _cast_plumbing(cast_next, b * n_s, lambda i, j: i * n_s + j)
    return pl.pallas_call(
        functools.partial(_even_prompt_kernel, tm=tm, n_cast=len(cast_next)),
        grid=(b, n_s),
        in_specs=[tile, _const_spec((1, D_MODEL)), _const_spec((D_MODEL, n_in)),
                  _const_spec((K_A, W_MIX)), _const_spec((K_B, W_MIX)), _const_spec((1, W_MIX)),
                  _const_spec((1, W_MIX)), _const_spec((1, W_MIX)),
                  _const_spec((2 * W_MIX, D_MODEL))] + c_in,
        out_specs=[tile,
                   pl.BlockSpec((None, K_A - 1, W_MIX), lambda i, j: (i, 0, 0)),
                   pl.BlockSpec((None, K_B - 1, W_MIX), lambda i, j: (i, 0, 0))] + c_out,
        out_shape=[jax.ShapeDtypeStruct((b, s, D_MODEL), F32),
                   jax.ShapeDtypeStruct((b, K_A - 1, W_MIX), F32),
                   jax.ShapeDtypeStruct((b, K_B - 1, W_MIX), F32)] + c_shapes,
        scratch_shapes=[pltpu.VMEM((tm, D_MODEL), BF16),
                        pltpu.VMEM((tm, n_in), F32),
                        pltpu.VMEM((tm + HIST_A, W_MIX), F32),
                        pltpu.VMEM((tm + HIST_B, W_MIX), F32),
                        pltpu.VMEM((tm, W_MIX), F32),
                        pltpu.VMEM((tm, 2 * W_MIX), BF16)],
        compiler_params=_params(2),
        name="even_prompt",
    )(x, g.reshape(1, D_MODEL), win, ca, cb, cbb.reshape(1, W_MIX), lng.reshape(1, W_MIX),
      lnb.reshape(1, W_MIX), wout, *(a for a, _ in cast_next))


def _tril_bf16(w):
    t = lax.broadcasted_iota(jnp.int32, (CHUNK, CHUNK), 0)
    s = lax.broadcasted_iota(jnp.int32, (CHUNK, CHUNK), 1)
    return jnp.where(s <= t, w, 0.0).astype(BF16)


def _odd_prompt_kernel(*refs, tm, n_cast):
    (x_ref, g_ref, win_ref, wpool_ref, pscale_ref, sguw_ref, sgubt_ref, lng_ref, lnb_ref,
     wout_ref) = refs[:10]
    cast_in = refs[10:10 + n_cast]
    o_ref, sp_ref = refs[10 + n_cast:12 + n_cast]
    cast_out = refs[12 + n_cast:12 + 2 * n_cast]
    hb_ref, p_ext, ps_ref, u_ref, vn_ref, d_ref, cat_ref, wm_ref = refs[12 + 2 * n_cast:]
    s = pl.program_id(1)
    _cast_rows(cast_in, cast_out)

    @pl.when(jnp.logical_and(pl.program_id(0) == 0, s == 0))
    def _():
        for gi in range(N_GROUPS):
            wm_ref[gi] = _tril_bf16(sguw_ref[gi])

    @pl.when(s == 0)
    def _():
        p_ext[0:HIST_P, :] = jnp.zeros((HIST_P, W_MIX), F32)

    @pl.when(s > 0)
    def _():
        p_ext[0:HIST_P, :] = p_ext[tm:tm + HIST_P, :]

    hb_ref[...] = _rms(x_ref[...], g_ref[...]).astype(BF16)

    def seg(i):
        return _dot(hb_ref[...], win_ref[:, i * W_MIX:(i + 1) * W_MIX])

    p_ext[HIST_P:HIST_P + tm, :] = seg(0)
    u_ref[...] = seg(1)
    vn_ref[...] = _ln(seg(2), lng_ref[...], lnb_ref[...]).astype(BF16)

    def pool_body(c, carry):
        r0 = pl.multiple_of(c * ROW_CHUNK, ROW_CHUNK)
        n = ROW_CHUNK + HIST_P
        win = p_ext.at[pl.ds(r0, n), :]
        pos = s * tm + r0 + lax.broadcasted_iota(jnp.int32, (ROW_CHUNK, GROUP_W), 0)
        prev, step = win, 1
        for level, w in enumerate(POOL_WINDOWS):
            lo = SUBLANES * (level + 1)
            first = level * GROUP_W
            sums = prev[lo:n, first:] + prev[lo - step:n - step, first:]
            lanes = slice(first, first + GROUP_W)
            tok = win[HIST_P:n, lanes]
            cnt = jnp.minimum(pos + 1, w).astype(F32)
            d_ref[pl.ds(r0, ROW_CHUNK), lanes] = (sums[HIST_P - lo:, 0:GROUP_W] / cnt - tok).astype(BF16)
            if level + 1 < len(POOL_WINDOWS):
                ps_ref[level % 2, lo:n, first + GROUP_W:] = sums[:, GROUP_W:]
                prev, step = ps_ref.at[level % 2], w
        return carry

    lax.fori_loop(0, tm // ROW_CHUNK, pool_body, 0)

    for gi in range(N_GROUPS):
        lanes = slice(gi * GROUP_W, (gi + 1) * GROUP_W)
        yc = _dot(d_ref[:, lanes], wpool_ref[gi]) * pscale_ref[:, lanes]
        cat_ref[:, lanes] = yc.astype(BF16)

    for c in range(tm // CHUNK):
        rows = slice(c * CHUNK, (c + 1) * CHUNK)
        for gi in range(N_GROUPS):
            lanes = slice(gi * GROUP_W, (gi + 1) * GROUP_W)
            mixed = _dot(wm_ref[gi], vn_ref[rows, lanes]) + sgubt_ref[:, gi:gi + 1]
            yd = u_ref[rows, lanes] * mixed
            cat_ref[rows, W_MIX + gi * GROUP_W:W_MIX + (gi + 1) * GROUP_W] = yd.astype(BF16)

    o_ref[...] = x_ref[...] + _dot(cat_ref[...], wout_ref[...])

    @pl.when(s == pl.num_programs(1) - 1)
    def _():
        sp_ref[...] = p_ext[HIST_P + tm - POOL_HIST:HIST_P + tm, :]


def _odd_prompt(x, g, win, wpool, pscale, sguw, sgub, lng, lnb, wout, tm, cast_next=()):
    b, s, _ = x.shape
    n_s = s // tm
    tile = pl.BlockSpec((None, tm, D_MODEL), lambda i, j: (i, j, 0))
    n_in = win.shape[1]
    c_in, c_out, c_shapes = _cast_plumbing(cast_next, b * n_s, lambda i, j: i * n_s + j)
    return pl.pallas_call(
        functools.partial(_odd_prompt_kernel, tm=tm, n_cast=len(cast_next)),
        grid=(b, n_s),
        in_specs=[tile, _const_spec((1, D_MODEL)), _const_spec((D_MODEL, n_in)),
                  _const_spec((N_GROUPS, GROUP_W, GROUP_W)), _const_spec((1, W_MIX)),
                  _const_spec((N_GROUPS, CHUNK, CHUNK)), _const_spec((CHUNK, N_GROUPS)),
                  _const_spec((1, W_MIX)), _const_spec((1, W_MIX)),
                  _const_spec((2 * W_MIX, D_MODEL))] + c_in,
        out_specs=[tile, pl.BlockSpec((None, POOL_HIST, W_MIX), lambda i, j: (i, 0, 0))] + c_out,
        out_shape=[jax.ShapeDtypeStruct((b, s, D_MODEL), F32),
                   jax.ShapeDtypeStruct((b, POOL_HIST, W_MIX), F32)] + c_shapes,
        scratch_shapes=[pltpu.VMEM((tm, D_MODEL), BF16),
                        pltpu.VMEM((tm + HIST_P, W_MIX), F32),
                        pltpu.VMEM((2, ROW_CHUNK + HIST_P, W_MIX), F32),
                        pltpu.VMEM((tm, W_MIX), F32),
                        pltpu.VMEM((tm, W_MIX), BF16),
                        pltpu.VMEM((tm, W_MIX), BF16),
                        pltpu.VMEM((tm, 2 * W_MIX), BF16),
                        pltpu.VMEM((N_GROUPS, CHUNK, CHUNK), BF16)],
        compiler_params=_params(2),
        name="odd_prompt",
    )(x, g.reshape(1, D_MODEL), win, wpool, pscale.reshape(1, W_MIX), sguw, sgub.T,
      lng.reshape(1, W_MIX), lnb.reshape(1, W_MIX), wout, *(a for a, _ in cast_next))


def _even_sample_kernel(x_ref, g_ref, win_ref, ca_ref, cb_ref, cbb_ref, lng_ref, lnb_ref, wout_ref,
                        ha_ref, hbst_ref,
                        o_ref, sa_ref, sb_ref,
                        hb_ref, gate_ref, ua_ext, ub_ext, cbv_ref, cat_ref, *, t_len, n_seq,
                        x_seq_major):
    if x_seq_major:
        x = jnp.concatenate([x_ref[:, t, :] for t in range(t_len)], axis=0)
    else:
        x = x_ref[...].reshape(t_len * n_seq, D_MODEL)
    hb_ref[...] = _rms(x, g_ref[...]).astype(BF16)

    def seg(i):
        return _dot(hb_ref[...], win_ref[:, i * W_MIX:(i + 1) * W_MIX])

    gate_ref[...] = seg(0)
    for j in range(K_A - 1):
        ua_ext[j] = ha_ref[:, j, :]
    ub_ext[0:K_B - 1] = hbst_ref[...]
    ua_ext[K_A - 1:K_A - 1 + t_len] = (seg(1) * seg(2)).reshape(t_len, n_seq, W_MIX)
    ub_ext[K_B - 1:K_B - 1 + t_len] = (seg(3) * jax.nn.sigmoid(seg(4))).reshape(t_len, n_seq, W_MIX)

    def body(t, carry):
        rows = pl.ds(pl.multiple_of(t * n_seq, n_seq), n_seq)
        for gi in range(N_GROUPS):
            lanes = slice(gi * GROUP_W, (gi + 1) * GROUP_W)
            acc = jnp.zeros((n_seq, GROUP_W), F32)
            for k in range(K_A):
                acc = acc + ca_ref[k:k + 1, lanes] * ua_ext[t + k, :, lanes]
            cat_ref[rows, lanes] = (gate_ref[rows, lanes] * acc).astype(BF16)
            acc = jnp.zeros((n_seq, GROUP_W), F32)
            for k in range(K_B):
                acc = acc + cb_ref[k:k + 1, lanes] * ub_ext[t + k, :, lanes]
            cbv_ref[:, lanes] = acc + cbb_ref[:, lanes]
        yb = _silu(_ln(cbv_ref[...], lng_ref[...], lnb_ref[...]))
        cat_ref[rows, W_MIX:2 * W_MIX] = yb.astype(BF16)
        return carry

    lax.fori_loop(0, t_len, body, 0)

    o_ref[...] = (x + _dot(cat_ref[...], wout_ref[...])).reshape(t_len, n_seq, D_MODEL)
    for j in range(K_A - 1):
        sa_ref[:, j, :] = ua_ext[t_len + j]
    sb_ref[...] = ub_ext[t_len:t_len + K_B - 1]


def _seq_block_spec(rows, nb, width):
    return pl.BlockSpec((rows, nb, width), lambda i, *_: (0, i, 0))


def _seq_major_spec(rows, nb, width):
    return pl.BlockSpec((nb, rows, width), lambda i, *_: (i, 0, 0))


def _xs_block_spec(t_len, nb):
    return pl.BlockSpec((None, t_len, nb, D_MODEL), lambda i, *_: (i, 0, 0, 0))


def _even_sample(x, g, win, ca, cb, cbb, lng, lnb, wout, hist_a, hist_b, nb, x_seq_major):
    n_seq, t_len = hist_a.shape[0], x.size // (hist_a.shape[0] * D_MODEL)
    m = t_len * nb
    n_in = win.shape[1]
    x_spec = _seq_major_spec(t_len, nb, D_MODEL) if x_seq_major else _xs_block_spec(t_len, nb)
    return pl.pallas_call(
        functools.partial(_even_sample_kernel, t_len=t_len, n_seq=nb, x_seq_major=x_seq_major),
        grid=(n_seq // nb,),
        in_specs=[x_spec, _const_spec((1, D_MODEL)),
                  _const_spec((D_MODEL, n_in)),
                  _const_spec((K_A, W_MIX)), _const_spec((K_B, W_MIX)), _const_spec((1, W_MIX)),
                  _const_spec((1, W_MIX)), _const_spec((1, W_MIX)), _const_spec((2 * W_MIX, D_MODEL)),
                  _seq_major_spec(K_A - 1, nb, W_MIX), _seq_block_spec(K_B - 1, nb, W_MIX)],
        out_specs=[_xs_block_spec(t_len, nb), _seq_major_spec(K_A - 1, nb, W_MIX),
                   _seq_block_spec(K_B - 1, nb, W_MIX)],
        out_shape=[jax.ShapeDtypeStruct((n_seq // nb, t_len, nb, D_MODEL), F32),
                   jax.ShapeDtypeStruct((n_seq, K_A - 1, W_MIX), F32),
                   jax.ShapeDtypeStruct((K_B - 1, n_seq, W_MIX), F32)],
        scratch_shapes=[pltpu.VMEM((m, D_MODEL), BF16),
                        pltpu.VMEM((m, W_MIX), F32),
                        pltpu.VMEM((t_len + K_A - 1, nb, W_MIX), F32),
                        pltpu.VMEM((t_len + K_B - 1, nb, W_MIX), F32),
                        pltpu.VMEM((nb, W_MIX), F32),
                        pltpu.VMEM((m, 2 * W_MIX), BF16)],
        compiler_params=_params(1),
        name="even_sample",
    )(x, g.reshape(1, D_MODEL), win, ca, cb, cbb.reshape(1, W_MIX), lng.reshape(1, W_MIX),
      lnb.reshape(1, W_MIX), wout, hist_a, hist_b)


def _odd_sample_kernel(sw_ref, sb_ref_smem,
                       x_ref, g_ref, win_ref, wpool_ref, pscale_ref, lng_ref, lnb_ref, wout_ref,
                       hp_ref,
                       o_ref, sp_ref, vn_out_ref,
                       hb_ref, p_ext, u_ref, vn_ref, d_ref, cat_ref, *, t_len, n_seq, start_pos):
    x = x_ref[...].reshape(t_len * n_seq, D_MODEL)
    hb_ref[...] = _rms(x, g_ref[...]).astype(BF16)

    def seg(i):
        return _dot(hb_ref[...], win_ref[:, i * W_MIX:(i + 1) * W_MIX])

    p_ext[0:POOL_HIST] = hp_ref[...]
    p_ext[POOL_HIST:POOL_HIST + t_len] = seg(0).reshape(t_len, n_seq, W_MIX)
    u_ref[...] = seg(1).reshape(t_len, n_seq, W_MIX)
    vn_ref[...] = _ln(seg(2), lng_ref[...], lnb_ref[...]).reshape(t_len, n_seq, W_MIX)

    for t in range(t_len):
        rows = slice(t * n_seq, (t + 1) * n_seq)
        for gi, w in enumerate(POOL_WINDOWS):
            lanes = slice(gi * GROUP_W, (gi + 1) * GROUP_W)
            tok = p_ext[POOL_HIST + t, :, lanes]
            wsum = tok
            for j in range(1, w):
                wsum = wsum + p_ext[POOL_HIST + t - j, :, lanes]
            cnt = float(min(start_pos + t + 1, w))
            d_ref[rows, lanes] = (wsum / cnt - tok).astype(BF16)
            mixed = jnp.zeros((n_seq, GROUP_W), F32) + sb_ref_smem[gi * t_len + t]
            for s in range(t + 1):
                mixed = mixed + sw_ref[(gi * t_len + t) * t_len + s] * vn_ref[s, :, lanes]
            cat_ref[rows, W_MIX + gi * GROUP_W:W_MIX + (gi + 1) * GROUP_W] = (
                u_ref[t, :, lanes] * mixed).astype(BF16)

    for gi in range(N_GROUPS):
        lanes = slice(gi * GROUP_W, (gi + 1) * GROUP_W)
        yc = _dot(d_ref[:, lanes], wpool_ref[gi]) * pscale_ref[:, lanes]
        cat_ref[:, lanes] = yc.astype(BF16)

    o_ref[...] = (x + _dot(cat_ref[...], wout_ref[...])).reshape(t_len, n_seq, D_MODEL)
    sp_ref[...] = p_ext[t_len:t_len + POOL_HIST]
    for t in range(t_len):
        vn_out_ref[:, t, :] = vn_ref[t]


def _odd_sample(x, g, win, wpool, pscale, sguw, sgub, lng, lnb, wout, hist_p, nb, start_pos):
    n_blocks, t_len, _, _ = x.shape
    n_seq = n_blocks * nb
    m = t_len * nb
    n_in = win.shape[1]
    assert t_len <= CHUNK
    sw = sguw[:, :t_len, :t_len].reshape(-1)
    sb = sgub[:, :t_len].reshape(-1)

    grid_spec = pltpu.PrefetchScalarGridSpec(
        num_scalar_prefetch=2,
        grid=(n_seq // nb,),
        in_specs=[_xs_block_spec(t_len, nb), _const_spec((1, D_MODEL)),
                  _const_spec((D_MODEL, n_in)),
                  _const_spec((N_GROUPS, GROUP_W, GROUP_W)), _const_spec((1, W_MIX)),
                  _const_spec((1, W_MIX)), _const_spec((1, W_MIX)),
                  _const_spec((2 * W_MIX, D_MODEL)),
                  _seq_block_spec(POOL_HIST, nb, W_MIX)],
        out_specs=[_xs_block_spec(t_len, nb), _seq_block_spec(POOL_HIST, nb, W_MIX),
                   _seq_major_spec(t_len, nb, W_MIX)],
        scratch_shapes=[pltpu.VMEM((m, D_MODEL), BF16),
                        pltpu.VMEM((t_len + POOL_HIST, nb, W_MIX), F32),
                        pltpu.VMEM((t_len, nb, W_MIX), F32),
                        pltpu.VMEM((t_len, nb, W_MIX), F32),
                        pltpu.VMEM((m, W_MIX), BF16),
                        pltpu.VMEM((m, 2 * W_MIX), BF16)],
    )
    return pl.pallas_call(
        functools.partial(_odd_sample_kernel, t_len=t_len, n_seq=nb, start_pos=start_pos),
        grid_spec=grid_spec,
        out_shape=[jax.ShapeDtypeStruct((n_blocks, t_len, nb, D_MODEL), F32),
                   jax.ShapeDtypeStruct((POOL_HIST, n_seq, W_MIX), F32),
                   jax.ShapeDtypeStruct((n_seq, t_len, W_MIX), F32)],
        compiler_params=_params(1),
        name="odd_sample",
    )(sw, sb, x, g.reshape(1, D_MODEL), win, wpool, pscale.reshape(1, W_MIX),
      lng.reshape(1, W_MIX), lnb.reshape(1, W_MIX), wout, hist_p)


def _time_major(a):
    return jnp.swapaxes(a, 0, 1)


def kernel(x_prompt, x_sample, state_conv_a, state_conv_b, state_pool, norm_mix, norm_ffn, ev_w_in,
           ev_conv_a, ev_conv_b, ev_conv_b_bias, ev_ln_g, ev_ln_b, ev_w_out, od_w_in, od_pool_w,
           od_pool_scale, od_sgu_w, od_sgu_b, od_sgu_ln_g, od_sgu_ln_b, od_w_out, ffn_w1, ffn_w2,
           norm_final):
    depth = norm_mix.shape[0]
    batch, seq, _ = x_prompt.shape
    n_seq, t_len, _ = x_sample.shape
    tm_mix = 512
    tm_ffn = 512
    nb = tm_ffn // t_len

    def mixer_weights(l):
        i = l // 2
        return ((ev_w_in, i), (ev_w_out, i)) if l % 2 == 0 else ((od_w_in, i), (od_w_out, i))

    xp = x_prompt
    xs = x_sample
    sa_p, sa_s, sb_p, sb_s, sc_p, sc_s, vn_s = [], [], [], [], [], [], []
    win, wout = (w[i].astype(BF16) for w, i in mixer_weights(0))
    for l in range(depth):
        i = l // 2
        ffn_f32 = ((ffn_w1, l), (ffn_w2, l))
        if l % 2 == 0:
            common = (norm_mix[l], win, ev_conv_a[i], ev_conv_b[i], ev_conv_b_bias[i], ev_ln_g[i],
                      ev_ln_b[i], wout)
            xp, a_p, b_p, w1, w2 = _even_prompt(xp, *common, tm=tm_mix, cast_next=ffn_f32)
            xs, a_s, b_s = _even_sample(xs, *common, state_conv_a[i], _time_major(state_conv_b[i]),
                                        nb, x_seq_major=(l == 0))
            sa_p.append(a_p)
            sb_p.append(b_p)
            sa_s.append(a_s)
            sb_s.append(_time_major(b_s))
        else:
            common = (norm_mix[l], win, od_pool_w[i].astype(BF16), od_pool_scale[i], od_sgu_w[i],
                      od_sgu_b[i], od_sgu_ln_g[i], od_sgu_ln_b[i], wout)
            xp, c_p, w1, w2 = _odd_prompt(xp, *common, tm=tm_mix, cast_next=ffn_f32)
            xs, c_s, v_s = _odd_sample(xs, *common, _time_major(state_pool[i]), nb, PAST_LEN)
            sc_p.append(c_p)
            sc_s.append(_time_major(c_s))
            vn_s.append(v_s)
        last = l == depth - 1
        xp, xs, *next_w = _ffn(xp.reshape(batch * seq, D_MODEL), xs.reshape(t_len * n_seq, D_MODEL),
                               norm_ffn[l], w1, w2, norm_final if last else None, tm_ffn, t_len,
                               cast_next=() if last else mixer_weights(l + 1))
        xp = xp.reshape(batch, seq, D_MODEL)
        if not last:
            xs = xs.reshape(n_seq // nb, t_len, nb, D_MODEL)
            win, wout = next_w

    return (xp, xs, jnp.stack(sa_p), jnp.stack(sa_s), jnp.stack(sb_p), jnp.stack(sb_s),
            jnp.stack(sc_p), jnp.stack(sc_s), jnp.stack(vn_s))
```

```python
import functools

import jax
import jax.numpy as jnp
from jax import lax
from jax.experimental import pallas as pl
from jax.experimental.pallas import tpu as pltpu

F32 = jnp.float32
BF16 = jnp.bfloat16

D_MODEL = 1024
W_MIX = D_MODEL // 2
K_A = 3
K_B = 31
POOL_WINDOWS = (2, 4, 8, 16)
POOL_HIST = max(POOL_WINDOWS) - 1
CHUNK = 128
N_GROUPS = 4
GROUP_W = W_MIX // N_GROUPS
D_FF = 4 * D_MODEL
PAST_LEN = 16384
EPS = 1e-6

SUBLANES = 8
HIST_A = 8
HIST_B = 32
HIST_P = SUBLANES * len(POOL_WINDOWS)
ROW_CHUNK = 64
CONV_ROWS = 128
MIX_BLOCK_ROWS = 256
FFN_COL_CHUNK = 1024
VMEM_LIMIT = 56 * 1024 * 1024


def _rms(x, g):
    return x * lax.rsqrt(jnp.mean(x * x, axis=-1, keepdims=True) + EPS) * g


def _ln(x, g, b):
    mu = jnp.mean(x, axis=-1, keepdims=True)
    xc = x - mu
    return xc * lax.rsqrt(jnp.mean(xc * xc, axis=-1, keepdims=True) + EPS) * g + b


def _dot(a, b):
    return jnp.dot(a, b, preferred_element_type=F32)


def _silu(x):
    return x * jax.nn.sigmoid(x)


def _const_spec(shape):
    nd = len(shape)
    return pl.BlockSpec(shape, lambda *_: (0,) * nd, pipeline_mode=pl.Buffered(1))


def _params(n_grid_axes):
    return pltpu.CompilerParams(
        dimension_semantics=("arbitrary",) * n_grid_axes,
        vmem_limit_bytes=VMEM_LIMIT,
    )


def _cast_plumbing(arrays, n_steps, step_of):
    in_specs, out_specs, shapes = [], [], []
    for stacked, layer in arrays:
        _, n_rows, n_cols = stacked.shape
        rows, rem = divmod(n_rows, n_steps)
        assert rem == 0 and rows % (2 * SUBLANES) == 0, stacked.shape
        in_specs.append(pl.BlockSpec((None, rows, n_cols),
                                     lambda *idx, _l=layer: (_l, step_of(*idx), 0)))
        out_specs.append(pl.BlockSpec((rows, n_cols), lambda *idx: (step_of(*idx), 0)))
        shapes.append(jax.ShapeDtypeStruct((n_rows, n_cols), BF16))
    return in_specs, out_specs, shapes


def _cast_rows(src_refs, dst_refs):
    for src, dst in zip(src_refs, dst_refs, strict=True):
        dst[...] = src[...].astype(BF16)


def _ffn_kernel(*refs, final, n_prompt, n_cast, t_len):
    n_in = 6 if final else 5
    xp_ref, xs_ref, g_ref, w1_ref, w2_ref = refs[:5]
    cast_in = refs[n_in:n_in + n_cast]
    op_ref, os_ref = refs[n_in + n_cast:n_in + n_cast + 2]
    cast_out = refs[n_in + n_cast + 2:n_in + 2 * n_cast + 2]
    hb_ref, act_ref = refs[n_in + 2 * n_cast + 2:]
    step = pl.program_id(0)

    def rows_block(x_ref):
        hb_ref[...] = _rms(x_ref[...], g_ref[...]).astype(BF16)
        for c in range(D_FF // FFN_COL_CHUNK):
            cols = slice(c * FFN_COL_CHUNK, (c + 1) * FFN_COL_CHUNK)
            a = _dot(hb_ref[...], w1_ref[:, cols])
            act_ref[:, cols] = jnp.square(jnp.maximum(a, 0.0)).astype(BF16)
        y = x_ref[...] + _dot(act_ref[...], w2_ref[...])
        return _rms(y, refs[5][...]) if final else y

    @pl.when(step < n_prompt)
    def _():
        op_ref[...] = rows_block(xp_ref)
        _cast_rows(cast_in, cast_out)

    @pl.when(step >= n_prompt)
    def _():
        y = rows_block(xs_ref)
        if final:
            nb = y.shape[0] // t_len
            for t in range(t_len):
                os_ref[:, t, :] = y[t * nb:(t + 1) * nb]
        else:
            os_ref[...] = y


def _ffn(xp2d, xs2d, g, w1, w2, g_final, tm, t_len, cast_next=()):
    n_prompt, n_sample = xp2d.shape[0] // tm, xs2d.shape[0] // tm
    final = g_final is not None
    last = n_prompt - 1
    p_spec = pl.BlockSpec((tm, D_MODEL), lambda i: (jnp.minimum(i, last), 0))
    s_spec = pl.BlockSpec((tm, D_MODEL), lambda i: (jnp.maximum(i - n_prompt, 0), 0))
    in_specs = [p_spec, s_spec, _const_spec((1, D_MODEL)), _const_spec((D_MODEL, D_FF)),
                _const_spec((D_FF, D_MODEL))]
    args = [xp2d, xs2d, g.reshape(1, D_MODEL), w1, w2]
    s_out_spec, s_out_shape = s_spec, jax.ShapeDtypeStruct(xs2d.shape, F32)
    if final:
        in_specs.append(_const_spec((1, D_MODEL)))
        args.append(g_final.reshape(1, D_MODEL))
        nb = tm // t_len
        s_out_spec = pl.BlockSpec((nb, t_len, D_MODEL), lambda i: (jnp.maximum(i - n_prompt, 0), 0, 0))
        s_out_shape = jax.ShapeDtypeStruct((xs2d.shape[0] // t_len, t_len, D_MODEL), F32)
    c_in, c_out, c_shapes = _cast_plumbing(cast_next, n_prompt, lambda i: jnp.minimum(i, last))
    return pl.pallas_call(
        functools.partial(_ffn_kernel, final=final, n_prompt=n_prompt, n_cast=len(cast_next),
                          t_len=t_len),
        grid=(n_prompt + n_sample,),
        in_specs=in_specs + c_in,
        out_specs=[p_spec, s_out_spec] + c_out,
        out_shape=[jax.ShapeDtypeStruct(xp2d.shape, F32), s_out_shape] + c_shapes,
        scratch_shapes=[pltpu.VMEM((tm, D_MODEL), BF16), pltpu.VMEM((tm, D_FF), BF16)],
        compiler_params=_params(1),
        name="ffn_final" if final else "ffn",
    )(*args, *(a for a, _ in cast_next))


def _causal_dwconv(win, w_ref, n_taps, hist, rows, lanes):
    out = None
    for r in range(SUBLANES):
        acc = None
        for q in range(hist // SUBLANES):
            k = n_taps - 1 - SUBLANES * q - r
            if k < 0:
                continue
            start = hist - SUBLANES * (q + 1)
            term = w_ref[k:k + 1, lanes] * win[start:start + rows + SUBLANES, lanes]
            acc = term if acc is None else acc + term
        if acc is None:
            continue
        part = acc[SUBLANES - r:SUBLANES - r + rows]
        out = part if out is None else out + part
    return out


def _even_prompt_kernel(*refs, tm, n_cast):
    x_ref, g_ref, win_ref, ca_ref, cb_ref, cbb_ref, lng_ref, lnb_ref, wout_ref = refs[:9]
    cast_in = refs[9:9 + n_cast]
    o_ref, sa_ref, sb_ref = refs[9 + n_cast:12 + n_cast]
    cast_out = refs[12 + n_cast:12 + 2 * n_cast]
    hb_ref, z_ref, ua_ext, ub_ext, cbv_ref, cat_ref = refs[12 + 2 * n_cast:]
    s = pl.program_id(1)
    _cast_rows(cast_in, cast_out)

    @pl.when(s == 0)
    def _():
        ua_ext[0:HIST_A, :] = jnp.zeros((HIST_A, W_MIX), F32)
        ub_ext[0:HIST_B, :] = jnp.zeros((HIST_B, W_MIX), F32)

    @pl.when(s > 0)
    def _():
        ua_ext[0:HIST_A, :] = ua_ext[tm:tm + HIST_A, :]
        ub_ext[0:HIST_B, :] = ub_ext[tm:tm + HIST_B, :]

    def zseg(rows, i, lanes=slice(0, W_MIX)):
        return z_ref[rows, i * W_MIX + lanes.start:i * W_MIX + lanes.stop]

    def project_pieces(rows):
        def norm():
            hb_ref[rows, :] = _rms(x_ref[rows, :], g_ref[...]).astype(BF16)

        def dot_piece(i):
            cols = slice(i * W_MIX, (i + 1) * W_MIX)
            z_ref[rows, cols] = _dot(hb_ref[rows, :], win_ref[:, cols])

        return [norm] + [functools.partial(dot_piece, i) for i in (3, 4, 1, 2, 0)]

    def mix_pieces(rows):
        def gates(r0):
            crow = slice(r0, r0 + ROW_CHUNK)
            ua_ext[HIST_A + r0:HIST_A + r0 + ROW_CHUNK, :] = zseg(crow, 1) * zseg(crow, 2)
            ub_ext[HIST_B + r0:HIST_B + r0 + ROW_CHUNK, :] = zseg(crow, 3) * jax.nn.sigmoid(zseg(crow, 4))

        def convs(r0, gi):
            crow = slice(r0, r0 + CONV_ROWS)
            lanes = slice(gi * GROUP_W, (gi + 1) * GROUP_W)
            win_a = ua_ext.at[r0:r0 + CONV_ROWS + HIST_A, :]
            win_b = ub_ext.at[r0:r0 + CONV_ROWS + HIST_B, :]
            ya = zseg(crow, 0, lanes) * _causal_dwconv(win_a, ca_ref, K_A, HIST_A, CONV_ROWS, lanes)
            cat_ref[crow, lanes] = ya.astype(BF16)
            cbv_ref[crow, lanes] = _causal_dwconv(win_b, cb_ref, K_B, HIST_B, CONV_ROWS, lanes)

        def norm_act(r0):
            crow = slice(r0, r0 + CONV_ROWS)
            yb = _silu(_ln(cbv_ref[crow, :] + cbb_ref[...], lng_ref[...], lnb_ref[...]))
            cat_ref[crow, W_MIX:2 * W_MIX] = yb.astype(BF16)

        pieces = [functools.partial(gates, r0) for r0 in range(rows.start, rows.stop, ROW_CHUNK)]
        for r0 in range(rows.start, rows.stop, CONV_ROWS):
            pieces += [functools.partial(convs, r0, gi) for gi in range(N_GROUPS)]
            pieces.append(functools.partial(norm_act, r0))
        return pieces

    def out_pieces(rows):
        def piece(j):
            cols = slice(j * W_MIX, (j + 1) * W_MIX)
            o_ref[rows, cols] = x_ref[rows, cols] + _dot(cat_ref[rows, :], wout_ref[:, cols])

        return [functools.partial(piece, j) for j in range(D_MODEL // W_MIX)]

    def emit(mxu_pieces, vpu_pieces):
        n_m, n_v = len(mxu_pieces), len(vpu_pieces)
        i = j = 0
        while i < n_m or j < n_v:
            if j >= n_v or (i < n_m and i * n_v <= j * n_m):
                mxu_pieces[i]()
                i += 1
            else:
                vpu_pieces[j]()
                j += 1

    blocks = [slice(r, r + MIX_BLOCK_ROWS) for r in range(0, tm, MIX_BLOCK_ROWS)]
    emit(project_pieces(blocks[0]), [])
    for i in range(len(blocks)):
        mxu = out_pieces(blocks[i - 1]) if i > 0 else []
        if i + 1 < len(blocks):
            mxu = project_pieces(blocks[i + 1]) + mxu
        emit(mxu, mix_pieces(blocks[i]))
    emit(out_pieces(blocks[-1]), [])

    @pl.when(s == pl.num_programs(1) - 1)
    def _():
        sa_ref[...] = ua_ext[HIST_A + tm - (K_A - 1):HIST_A + tm, :]
        sb_ref[...] = ub_ext[HIST_B + tm - (K_B - 1):HIST_B + tm, :]


def _even_prompt(x, g, win, ca, cb, cbb, lng, lnb, wout, tm, cast_next=()):
    b, s, _ = x.shape
    n_s = s // tm
    tile = pl.BlockSpec((None, tm, D_MODEL), lambda i, j: (i, j, 0))
    n_in = win.shape[1]
    c_in, c_out, c_shapes = _cast_plumbing(cast_next, b * n_s, lambda i, j: i * n_s + j)
    return pl.pallas_call(
        functools.partial(_even_prompt_kernel, tm=tm, n_cast=len(cast_next)),
        grid=(b, n_s),
        in_specs=[tile, _const_spec((1, D_MODEL)), _const_spec((D_MODEL, n_in)),
                  _const_spec((K_A, W_MIX)), _const_spec((K_B, W_MIX)), _const_spec((1, W_MIX)),
                  _const_spec((1, W_MIX)), _const_spec((1, W_MIX)),
                  _const_spec((2 * W_MIX, D_MODEL))] + c_in,
        out_specs=[tile,
                   pl.BlockSpec((None, K_A - 1, W_MIX), lambda i, j: (i, 0, 0)),
                   pl.BlockSpec((None, K_B - 1, W_MIX), lambda i, j: (i, 0, 0))] + c_out,
        out_shape=[jax.ShapeDtypeStruct((b, s, D_MODEL), F32),
                   jax.ShapeDtypeStruct((b, K_A - 1, W_MIX), F32),
                   jax.ShapeDtypeStruct((b, K_B - 1, W_MIX), F32)] + c_shapes,
        scratch_shapes=[pltpu.VMEM((tm, D_MODEL), BF16),
                        pltpu.VMEM((tm, n_in), F32),
                        pltpu.VMEM((tm + HIST_A, W_MIX), F32),
                        pltpu.VMEM((tm + HIST_B, W_MIX), F32),
                        pltpu.VMEM((tm, W_MIX), F32),
                        pltpu.VMEM((tm, 2 * W_MIX), BF16)],
        compiler_params=_params(2),
        name="even_prompt",
    )(x, g.reshape(1, D_MODEL), win, ca, cb, cbb.reshape(1, W_MIX), lng.reshape(1, W_MIX),
      lnb.reshape(1, W_MIX), wout, *(a for a, _ in cast_next))


def _tril_bf16(w):
    t = lax.broadcasted_iota(jnp.int32, (CHUNK, CHUNK), 0)
    s = lax.broadcasted_iota(jnp.int32, (CHUNK, CHUNK), 1)
    return jnp.where(s <= t, w, 0.0).astype(BF16)


def _odd_prompt_kernel(*refs, tm, n_cast):
    (x_ref, g_ref, win_ref, wpool_ref, pscale_ref, sguw_ref, sgubt_ref, lng_ref, lnb_ref,
     wout_ref) = refs[:10]
    cast_in = refs[10:10 + n_cast]
    o_ref, sp_ref = refs[10 + n_cast:12 + n_cast]
    cast_out = refs[12 + n_cast:12 + 2 * n_cast]
    hb_ref, p_ext, ps_ref, inv_ref, u_ref, vn_ref, d_ref, cat_ref, wm_ref = refs[12 + 2 * n_cast:]
    s = pl.program_id(1)
    _cast_rows(cast_in, cast_out)

    @pl.when(jnp.logical_and(pl.program_id(0) == 0, s == 0))
    def _():
        for gi in range(N_GROUPS):
            wm_ref[gi] = _tril_bf16(sguw_ref[gi])

    @pl.when(s == 0)
    def _():
        p_ext[0:HIST_P, :] = jnp.zeros((HIST_P, W_MIX), F32)

    @pl.when(s > 0)
    def _():
        p_ext[0:HIST_P, :] = p_ext[tm:tm + HIST_P, :]

    hb_ref[...] = _rms(x_ref[...], g_ref[...]).astype(BF16)

    def seg(i):
        return _dot(hb_ref[...], win_ref[:, i * W_MIX:(i + 1) * W_MIX])

    p_ext[HIST_P:HIST_P + tm, :] = seg(0)
    u_ref[...] = seg(1)
    vn_ref[...] = _ln(seg(2), lng_ref[...], lnb_ref[...]).astype(BF16)

    assert ROW_CHUNK >= max(POOL_WINDOWS)
    pos0 = s * tm + lax.broadcasted_iota(jnp.int32, (ROW_CHUNK, GROUP_W), 0)
    for level, w in enumerate(POOL_WINDOWS):
        inv_ref[level] = 1.0 / jnp.minimum(pos0 + 1, w).astype(F32)

    def pool_body(c, carry):
        r0 = pl.multiple_of(c * ROW_CHUNK, ROW_CHUNK)
        n = ROW_CHUNK + HIST_P
        win = p_ext.at[pl.ds(r0, n), :]
        prev, step = win, 1
        for level, w in enumerate(POOL_WINDOWS):
            lo = SUBLANES * (level + 1)
            first = level * GROUP_W
            sums = prev[lo:n, first:] + prev[lo - step:n - step, first:]
            lanes = slice(first, first + GROUP_W)
            tok = win[HIST_P:n, lanes]
            inv = jnp.where(c == 0, inv_ref[level], 1.0 / w)
            d_ref[pl.ds(r0, ROW_CHUNK), lanes] = (sums[HIST_P - lo:, 0:GROUP_W] * inv - tok).astype(BF16)
            if level + 1 < len(POOL_WINDOWS):
                ps_ref[level % 2, lo:n, first + GROUP_W:] = sums[:, GROUP_W:]
                prev, step = ps_ref.at[level % 2], w
        return carry

    lax.fori_loop(0, tm // ROW_CHUNK, pool_body, 0)

    for gi in range(N_GROUPS):
        lanes = slice(gi * GROUP_W, (gi + 1) * GROUP_W)
        yc = _dot(d_ref[:, lanes], wpool_ref[gi]) * pscale_ref[:, lanes]
        cat_ref[:, lanes] = yc.astype(BF16)

    for c in range(tm // CHUNK):
        rows = slice(c * CHUNK, (c + 1) * CHUNK)
        for gi in range(N_GROUPS):
            lanes = slice(gi * GROUP_W, (gi + 1) * GROUP_W)
            mixed = _dot(wm_ref[gi], vn_ref[rows, lanes]) + sgubt_ref[:, gi:gi + 1]
            yd = u_ref[rows, lanes] * mixed
            cat_ref[rows, W_MIX + gi * GROUP_W:W_MIX + (gi + 1) * GROUP_W] = yd.astype(BF16)

    o_ref[...] = x_ref[...] + _dot(cat_ref[...], wout_ref[...])

    @pl.when(s == pl.num_programs(1) - 1)
    def _():
        sp_ref[...] = p_ext[HIST_P + tm - POOL_HIST:HIST_P + tm, :]


def _odd_prompt(x, g, win, wpool, pscale, sguw, sgub, lng, lnb, wout, tm, cast_next=()):
    b, s, _ = x.shape
    n_s = s // tm
    tile = pl.BlockSpec((None, tm, D_MODEL), lambda i, j: (i, j, 0))
    n_in = win.shape[1]
    c_in, c_out, c_shapes = _cast_plumbing(cast_next, b * n_s, lambda i, j: i * n_s + j)
    return pl.pallas_call(
        functools.partial(_odd_prompt_kernel, tm=tm, n_cast=len(cast_next)),
        grid=(b, n_s),
        in_specs=[tile, _const_spec((1, D_MODEL)), _const_spec((D_MODEL, n_in)),
                  _const_spec((N_GROUPS, GROUP_W, GROUP_W)), _const_spec((1, W_MIX)),
                  _const_spec((N_GROUPS, CHUNK, CHUNK)), _const_spec((CHUNK, N_GROUPS)),
                  _const_spec((1, W_MIX)), _const_spec((1, W_MIX)),
                  _const_spec((2 * W_MIX, D_MODEL))] + c_in,
        out_specs=[tile, pl.BlockSpec((None, POOL_HIST, W_MIX), lambda i, j: (i, 0, 0))] + c_out,
        out_shape=[jax.ShapeDtypeStruct((b, s, D_MODEL), F32),
                   jax.ShapeDtypeStruct((b, POOL_HIST, W_MIX), F32)] + c_shapes,
        scratch_shapes=[pltpu.VMEM((tm, D_MODEL), BF16),
                        pltpu.VMEM((tm + HIST_P, W_MIX), F32),
                        pltpu.VMEM((2, ROW_CHUNK + HIST_P, W_MIX), F32),
                        pltpu.VMEM((len(POOL_WINDOWS), ROW_CHUNK, GROUP_W), F32),
                        pltpu.VMEM((tm, W_MIX), F32),
                        pltpu.VMEM((tm, W_MIX), BF16),
                        pltpu.VMEM((tm, W_MIX), BF16),
                        pltpu.VMEM((tm, 2 * W_MIX), BF16),
                        pltpu.VMEM((N_GROUPS, CHUNK, CHUNK), BF16)],
        compiler_params=_params(2),
        name="odd_prompt",
    )(x, g.reshape(1, D_MODEL), win, wpool, pscale.reshape(1, W_MIX), sguw, sgub.T,
      lng.reshape(1, W_MIX), lnb.reshape(1, W_MIX), wout, *(a for a, _ in cast_next))


def _even_sample_kernel(x_ref, g_ref, win_ref, ca_ref, cb_ref, cbb_ref, lng_ref, lnb_ref, wout_ref,
                        ha_ref, hbst_ref,
                        o_ref, sa_ref, sb_ref,
                        hb_ref, gate_ref, ua_ext, ub_ext, cbv_ref, cat_ref, *, t_len, n_seq,
                        x_seq_major):
    if x_seq_major:
        x = jnp.concatenate([x_ref[:, t, :] for t in range(t_len)], axis=0)
    else:
        x = x_ref[...].reshape(t_len * n_seq, D_MODEL)
    hb_ref[...] = _rms(x, g_ref[...]).astype(BF16)

    def seg(i):
        return _dot(hb_ref[...], win_ref[:, i * W_MIX:(i + 1) * W_MIX])

    gate_ref[...] = seg(0)
    for j in range(K_A - 1):
        ua_ext[j] = ha_ref[:, j, :]
    ub_ext[0:K_B - 1] = hbst_ref[...]
    ua_ext[K_A - 1:K_A - 1 + t_len] = (seg(1) * seg(2)).reshape(t_len, n_seq, W_MIX)
    ub_ext[K_B - 1:K_B - 1 + t_len] = (seg(3) * jax.nn.sigmoid(seg(4))).reshape(t_len, n_seq, W_MIX)

    def body(t, carry):
        rows = pl.ds(pl.multiple_of(t * n_seq, n_seq), n_seq)
        for gi in range(N_GROUPS):
            lanes = slice(gi * GROUP_W, (gi + 1) * GROUP_W)
            acc = jnp.zeros((n_seq, GROUP_W), F32)
            for k in range(K_A):
                acc = acc + ca_ref[k:k + 1, lanes] * ua_ext[t + k, :, lanes]
            cat_ref[rows, lanes] = (gate_ref[rows, lanes] * acc).astype(BF16)
            acc = jnp.zeros((n_seq, GROUP_W), F32)
            for k in range(K_B):
                acc = acc + cb_ref[k:k + 1, lanes] * ub_ext[t + k, :, lanes]
            cbv_ref[:, lanes] = acc + cbb_ref[:, lanes]
        yb = _silu(_ln(cbv_ref[...], lng_ref[...], lnb_ref[...]))
        cat_ref[rows, W_MIX:2 * W_MIX] = yb.astype(BF16)
        return carry

    lax.fori_loop(0, t_len, body, 0)

    o_ref[...] = (x + _dot(cat_ref[...], wout_ref[...])).reshape(t_len, n_seq, D_MODEL)
    for j in range(K_A - 1):
        sa_ref[:, j, :] = ua_ext[t_len + j]
    sb_ref[...] = ub_ext[t_len:t_len + K_B - 1]


def _seq_block_spec(rows, nb, width):
    return pl.BlockSpec((rows, nb, width), lambda i, *_: (0, i, 0))


def _seq_major_spec(rows, nb, width):
    return pl.BlockSpec((nb, rows, width), lambda i, *_: (i, 0, 0))


def _xs_block_spec(t_len, nb):
    return pl.BlockSpec((None, t_len, nb, D_MODEL), lambda i, *_: (i, 0, 0, 0))


def _even_sample(x, g, win, ca, cb, cbb, lng, lnb, wout, hist_a, hist_b, nb, x_seq_major):
    n_seq, t_len = hist_a.shape[0], x.size // (hist_a.shape[0] * D_MODEL)
    m = t_len * nb
    n_in = win.shape[1]
    x_spec = _seq_major_spec(t_len, nb, D_MODEL) if x_seq_major else _xs_block_spec(t_len, nb)
    return pl.pallas_call(
        functools.partial(_even_sample_kernel, t_len=t_len, n_seq=nb, x_seq_major=x_seq_major),
        grid=(n_seq // nb,),
        in_specs=[x_spec, _const_spec((1, D_MODEL)),
                  _const_spec((D_MODEL, n_in)),
                  _const_spec((K_A, W_MIX)), _const_spec((K_B, W_MIX)), _const_spec((1, W_MIX)),
                  _const_spec((1, W_MIX)), _const_spec((1, W_MIX)), _const_spec((2 * W_MIX, D_MODEL)),
                  _seq_major_spec(K_A - 1, nb, W_MIX), _seq_block_spec(K_B - 1, nb, W_MIX)],
        out_specs=[_xs_block_spec(t_len, nb), _seq_major_spec(K_A - 1, nb, W_MIX),
                   _seq_block_spec(K_B - 1, nb, W_MIX)],
        out_shape=[jax.ShapeDtypeStruct((n_seq // nb, t_len, nb, D_MODEL), F32),
                   jax.ShapeDtypeStruct((n_seq, K_A - 1, W_MIX), F32),
                   jax.ShapeDtypeStruct((K_B - 1, n_seq, W_MIX), F32)],
        scratch_shapes=[pltpu.VMEM((m, D_MODEL), BF16),
                        pltpu.VMEM((m, W_MIX), F32),
                        pltpu.VMEM((t_len + K_A - 1, nb, W_MIX), F32),
                        pltpu.VMEM((t_len + K_B - 1, nb, W_MIX), F32),
                        pltpu.VMEM((nb, W_MIX), F32),
                        pltpu.VMEM((m, 2 * W_MIX), BF16)],
        compiler_params=_params(1),
        name="even_sample",
    )(x, g.reshape(1, D_MODEL), win, ca, cb, cbb.reshape(1, W_MIX), lng.reshape(1, W_MIX),
      lnb.reshape(1, W_MIX), wout, hist_a, hist_b)


def _odd_sample_kernel(sw_ref, sb_ref_smem,
                       x_ref, g_ref, win_ref, wpool_ref, pscale_ref, lng_ref, lnb_ref, wout_ref,
                       hp_ref,
                       o_ref, sp_ref, vn_out_ref,
                       hb_ref, p_ext, u_ref, vn_ref, d_ref, cat_ref, *, t_len, n_seq, start_pos):
    x = x_ref[...].reshape(t_len * n_seq, D_MODEL)
    hb_ref[...] = _rms(x, g_ref[...]).astype(BF16)

    def seg(i):
        return _dot(hb_ref[...], win_ref[:, i * W_MIX:(i + 1) * W_MIX])

    p_ext[0:POOL_HIST] = hp_ref[...]
    p_ext[POOL_HIST:POOL_HIST + t_len] = seg(0).reshape(t_len, n_seq, W_MIX)
    u_ref[...] = seg(1).reshape(t_len, n_seq, W_MIX)
    vn_ref[...] = _ln(seg(2), lng_ref[...], lnb_ref[...]).reshape(t_len, n_seq, W_MIX)

    for t in range(t_len):
        rows = slice(t * n_seq, (t + 1) * n_seq)
        for gi, w in enumerate(POOL_WINDOWS):
            lanes = slice(gi * GROUP_W, (gi + 1) * GROUP_W)
            tok = p_ext[POOL_HIST + t, :, lanes]
            wsum = tok
            for j in range(1, w):
                wsum = wsum + p_ext[POOL_HIST + t - j, :, lanes]
            cnt = float(min(start_pos + t + 1, w))
            d_ref[rows, lanes] = (wsum / cnt - tok).astype(BF16)
            mixed = jnp.zeros((n_seq, GROUP_W), F32) + sb_ref_smem[gi * t_len + t]
            for s in range(t + 1):
                mixed = mixed + sw_ref[(gi * t_len + t) * t_len + s] * vn_ref[s, :, lanes]
            cat_ref[rows, W_MIX + gi * GROUP_W:W_MIX + (gi + 1) * GROUP_W] = (
                u_ref[t, :, lanes] * mixed).astype(BF16)

    for gi in range(N_GROUPS):
        lanes = slice(gi * GROUP_W, (gi + 1) * GROUP_W)
        yc = _dot(d_ref[:, lanes], wpool_ref[gi]) * pscale_ref[:, lanes]
        cat_ref[:, lanes] = yc.astype(BF16)

    o_ref[...] = (x + _dot(cat_ref[...], wout_ref[...])).reshape(t_len, n_seq, D_MODEL)
    sp_ref[...] = p_ext[t_len:t_len + POOL_HIST]
    for t in range(t_len):
        vn_out_ref[:, t, :] = vn_ref[t]


def _odd_sample(x, g, win, wpool, pscale, sguw, sgub, lng, lnb, wout, hist_p, nb, start_pos):
    n_blocks, t_len, _, _ = x.shape
    n_seq = n_blocks * nb
    m = t_len * nb
    n_in = win.shape[1]
    assert t_len <= CHUNK
    sw = sguw[:, :t_len, :t_len].reshape(-1)
    sb = sgub[:, :t_len].reshape(-1)

    grid_spec = pltpu.PrefetchScalarGridSpec(
        num_scalar_prefetch=2,
        grid=(n_seq // nb,),
        in_specs=[_xs_block_spec(t_len, nb), _const_spec((1, D_MODEL)),
                  _const_spec((D_MODEL, n_in)),
                  _const_spec((N_GROUPS, GROUP_W, GROUP_W)), _const_spec((1, W_MIX)),
                  _const_spec((1, W_MIX)), _const_spec((1, W_MIX)),
                  _const_spec((2 * W_MIX, D_MODEL)),
                  _seq_block_spec(POOL_HIST, nb, W_MIX)],
        out_specs=[_xs_block_spec(t_len, nb), _seq_block_spec(POOL_HIST, nb, W_MIX),
                   _seq_major_spec(t_len, nb, W_MIX)],
        scratch_shapes=[pltpu.VMEM((m, D_MODEL), BF16),
                        pltpu.VMEM((t_len + POOL_HIST, nb, W_MIX), F32),
                        pltpu.VMEM((t_len, nb, W_MIX), F32),
                        pltpu.VMEM((t_len, nb, W_MIX), F32),
                        pltpu.VMEM((m, W_MIX), BF16),
                        pltpu.VMEM((m, 2 * W_MIX), BF16)],
    )
    return pl.pallas_call(
        functools.partial(_odd_sample_kernel, t_len=t_len, n_seq=nb, start_pos=start_pos),
        grid_spec=grid_spec,
        out_shape=[jax.ShapeDtypeStruct((n_blocks, t_len, nb, D_MODEL), F32),
                   jax.ShapeDtypeStruct((POOL_HIST, n_seq, W_MIX), F32),
                   jax.ShapeDtypeStruct((n_seq, t_len, W_MIX), F32)],
        compiler_params=_params(1),
        name="odd_sample",
    )(sw, sb, x, g.reshape(1, D_MODEL), win, wpool, pscale.reshape(1, W_MIX),
      lng.reshape(1, W_MIX), lnb.reshape(1, W_MIX), wout, hist_p)


def _time_major(a):
    return jnp.swapaxes(a, 0, 1)


def kernel(x_prompt, x_sample, state_conv_a, state_conv_b, state_pool, norm_mix, norm_ffn, ev_w_in,
           ev_conv_a, ev_conv_b, ev_conv_b_bias, ev_ln_g, ev_ln_b, ev_w_out, od_w_in, od_pool_w,
           od_pool_scale, od_sgu_w, od_sgu_b, od_sgu_ln_g, od_sgu_ln_b, od_w_out, ffn_w1, ffn_w2,
           norm_final):
    depth = norm_mix.shape[0]
    batch, seq, _ = x_prompt.shape
    n_seq, t_len, _ = x_sample.shape
    tm_mix = 1024
    tm_ffn = 512
    nb = tm_ffn // t_len

    def mixer_weights(l):
        i = l // 2
        return ((ev_w_in, i), (ev_w_out, i)) if l % 2 == 0 else ((od_w_in, i), (od_w_out, i))

    xp = x_prompt
    xs = x_sample
    sa_p, sa_s, sb_p, sb_s, sc_p, sc_s, vn_s = [], [], [], [], [], [], []
    win, wout = (w[i].astype(BF16) for w, i in mixer_weights(0))
    for l in range(depth):
        i = l // 2
        ffn_f32 = ((ffn_w1, l), (ffn_w2, l))
        if l % 2 == 0:
            common = (norm_mix[l], win, ev_conv_a[i], ev_conv_b[i], ev_conv_b_bias[i], ev_ln_g[i],
                      ev_ln_b[i], wout)
            xp, a_p, b_p, w1, w2 = _even_prompt(xp, *common, tm=tm_mix, cast_next=ffn_f32)
            xs, a_s, b_s = _even_sample(xs, *common, state_conv_a[i], _time_major(state_conv_b[i]),
                                        nb, x_seq_major=(l == 0))
            sa_p.append(a_p)
            sb_p.append(b_p)
            sa_s.append(a_s)
            sb_s.append(_time_major(b_s))
        else:
            common = (norm_mix[l], win, od_pool_w[i].astype(BF16), od_pool_scale[i], od_sgu_w[i],
                      od_sgu_b[i], od_sgu_ln_g[i], od_sgu_ln_b[i], wout)
            xp, c_p, w1, w2 = _odd_prompt(xp, *common, tm=tm_mix, cast_next=ffn_f32)
            xs, c_s, v_s = _odd_sample(xs, *common, _time_major(state_pool[i]), nb, PAST_LEN)
            sc_p.append(c_p)
            sc_s.append(_time_major(c_s))
            vn_s.append(v_s)
        last = l == depth - 1
        xp, xs, *next_w = _ffn(xp.reshape(batch * seq, D_MODEL), xs.reshape(t_len * n_seq, D_MODEL),
                               norm_ffn[l], w1, w2, norm_final if last else None, tm_ffn, t_len,
                               cast_next=() if last else mixer_weights(l + 1))
        xp = xp.reshape(batch, seq, D_MODEL)
        if not last:
            xs = xs.reshape(n_seq // nb, t_len, nb, D_MODEL)
            win, wout = next_w

    return (xp, xs, jnp.stack(sa_p), jnp.stack(sa_s), jnp.stack(sb_p), jnp.stack(sb_s),
            jnp.stack(sc_p), jnp.stack(sc_s), jnp.stack(vn_s))
```

```python
import functools

import jax
import jax.numpy as jnp
from jax import lax
from jax.experimental import pallas as pl
from jax.experimental.pallas import tpu as pltpu

F32 = jnp.float32
BF16 = jnp.bfloat16

D_MODEL = 1024
W_MIX = D_MODEL // 2
K_A = 3
K_B = 31
POOL_WINDOWS = (2, 4, 8, 16)
POOL_HIST = max(POOL_WINDOWS) - 1
CHUNK = 128
N_GROUPS = 4
GROUP_W = W_MIX // N_GROUPS
D_FF = 4 * D_MODEL
PAST_LEN = 16384
EPS = 1e-6

SUBLANES = 8
HIST_A = 8
HIST_B = 32
HIST_P = SUBLANES * len(POOL_WINDOWS)
ROW_CHUNK = 64
CONV_ROWS = 64
MIX_BLOCK_ROWS = 256
FFN_COL_CHUNK = 1024
VMEM_LIMIT = 56 * 1024 * 1024


def _rms(x, g):
    return x * lax.rsqrt(jnp.mean(x * x, axis=-1, keepdims=True) + EPS) * g


def _ln(x, g, b):
    mu = jnp.mean(x, axis=-1, keepdims=True)
    xc = x - mu
    return xc * lax.rsqrt(jnp.mean(xc * xc, axis=-1, keepdims=True) + EPS) * g + b


def _dot(a, b):
    return jnp.dot(a, b, preferred_element_type=F32)


def _silu(x):
    return x * jax.nn.sigmoid(x)


def _const_spec(shape):
    nd = len(shape)
    return pl.BlockSpec(shape, lambda *_: (0,) * nd, pipeline_mode=pl.Buffered(1))


def _params(n_grid_axes):
    return pltpu.CompilerParams(
        dimension_semantics=("arbitrary",) * n_grid_axes,
        vmem_limit_bytes=VMEM_LIMIT,
    )


def _cast_plumbing(arrays, n_steps, step_of):
    in_specs, out_specs, shapes = [], [], []
    for stacked, layer in arrays:
        _, n_rows, n_cols = stacked.shape
        rows, rem = divmod(n_rows, n_steps)
        assert rem == 0 and rows % (2 * SUBLANES) == 0, stacked.shape
        in_specs.append(pl.BlockSpec((None, rows, n_cols),
                                     lambda *idx, _l=layer: (_l, step_of(*idx), 0)))
        out_specs.append(pl.BlockSpec((rows, n_cols), lambda *idx: (step_of(*idx), 0)))
        shapes.append(jax.ShapeDtypeStruct((n_rows, n_cols), BF16))
    return in_specs, out_specs, shapes


def _cast_rows(src_refs, dst_refs):
    for src, dst in zip(src_refs, dst_refs, strict=True):
        dst[...] = src[...].astype(BF16)


def _ffn_kernel(*refs, final, n_prompt, n_cast, t_len):
    n_in = 6 if final else 5
    xp_ref, xs_ref, g_ref, w1_ref, w2_ref = refs[:5]
    cast_in = refs[n_in:n_in + n_cast]
    op_ref, os_ref = refs[n_in + n_cast:n_in + n_cast + 2]
    cast_out = refs[n_in + n_cast + 2:n_in + 2 * n_cast + 2]
    hb_ref, act_ref = refs[n_in + 2 * n_cast + 2:]
    step = pl.program_id(0)

    def rows_block(x_ref):
        hb_ref[...] = _rms(x_ref[...], g_ref[...]).astype(BF16)
        for c in range(D_FF // FFN_COL_CHUNK):
            cols = slice(c * FFN_COL_CHUNK, (c + 1) * FFN_COL_CHUNK)
            a = _dot(hb_ref[...], w1_ref[:, cols])
            act_ref[:, cols] = jnp.square(jnp.maximum(a, 0.0)).astype(BF16)
        y = x_ref[...] + _dot(act_ref[...], w2_ref[...])
        return _rms(y, refs[5][...]) if final else y

    @pl.when(step < n_prompt)
    def _():
        op_ref[...] = rows_block(xp_ref)
        _cast_rows(cast_in, cast_out)

    @pl.when(step >= n_prompt)
    def _():
        y = rows_block(xs_ref)
        if final:
            nb = y.shape[0] // t_len
            for t in range(t_len):
                os_ref[:, t, :] = y[t * nb:(t + 1) * nb]
        else:
            os_ref[...] = y


def _ffn(xp2d, xs2d, g, w1, w2, g_final, tm, t_len, cast_next=()):
    n_prompt, n_sample = xp2d.shape[0] // tm, xs2d.shape[0] // tm
    final = g_final is not None
    last = n_prompt - 1
    p_spec = pl.BlockSpec((tm, D_MODEL), lambda i: (jnp.minimum(i, last), 0))
    s_spec = pl.BlockSpec((tm, D_MODEL), lambda i: (jnp.maximum(i - n_prompt, 0), 0))
    in_specs = [p_spec, s_spec, _const_spec((1, D_MODEL)), _const_spec((D_MODEL, D_FF)),
                _const_spec((D_FF, D_MODEL))]
    args = [xp2d, xs2d, g.reshape(1, D_MODEL), w1, w2]
    s_out_spec, s_out_shape = s_spec, jax.ShapeDtypeStruct(xs2d.shape, F32)
    if final:
        in_specs.append(_const_spec((1, D_MODEL)))
        args.append(g_final.reshape(1, D_MODEL))
        nb = tm // t_len
        s_out_spec = pl.BlockSpec((nb, t_len, D_MODEL), lambda i: (jnp.maximum(i - n_prompt, 0), 0, 0))
        s_out_shape = jax.ShapeDtypeStruct((xs2d.shape[0] // t_len, t_len, D_MODEL), F32)
    c_in, c_out, c_shapes = _cast_plumbing(cast_next, n_prompt, lambda i: jnp.minimum(i, last))
    return pl.pallas_call(
        functools.partial(_ffn_kernel, final=final, n_prompt=n_prompt, n_cast=len(cast_next),
                          t_len=t_len),
        grid=(n_prompt + n_sample,),
        in_specs=in_specs + c_in,
        out_specs=[p_spec, s_out_spec] + c_out,
        out_shape=[jax.ShapeDtypeStruct(xp2d.shape, F32), s_out_shape] + c_shapes,
        scratch_shapes=[pltpu.VMEM((tm, D_MODEL), BF16), pltpu.VMEM((tm, D_FF), BF16)],
        compiler_params=_params(1),
        name="ffn_final" if final else "ffn",
    )(*args, *(a for a, _ in cast_next))


def _causal_dwconv(ext_ref, gi, w_ref, n_taps, hist, r0, rows):
    lanes = slice(gi * GROUP_W, (gi + 1) * GROUP_W)
    acc = None
    for k in range(n_taps):
        lo = hist + r0 - (n_taps - 1) + k
        term = w_ref[k:k + 1, lanes] * ext_ref[gi, lo:lo + rows, :]
        acc = term if acc is None else acc + term
    return acc


def _even_prompt_kernel(*refs, tm, n_cast):
    x_ref, g_ref, win_ref, ca_ref, cb_ref, cbb_ref, lng_ref, lnb_ref, wout_ref = refs[:9]
    cast_in = refs[9:9 + n_cast]
    o_ref, sa_ref, sb_ref = refs[9 + n_cast:12 + n_cast]
    cast_out = refs[12 + n_cast:12 + 2 * n_cast]
    hb_ref, z_ref, ua_ext, ub_ext, cbv_ref, cat_ref = refs[12 + 2 * n_cast:]
    s = pl.program_id(1)
    _cast_rows(cast_in, cast_out)

    @pl.when(s == 0)
    def _():
        ua_ext[:, 0:HIST_A, :] = jnp.zeros((N_GROUPS, HIST_A, GROUP_W), F32)
        ub_ext[:, 0:HIST_B, :] = jnp.zeros((N_GROUPS, HIST_B, GROUP_W), F32)

    @pl.when(s > 0)
    def _():
        ua_ext[:, 0:HIST_A, :] = ua_ext[:, tm:tm + HIST_A, :]
        ub_ext[:, 0:HIST_B, :] = ub_ext[:, tm:tm + HIST_B, :]

    def zseg(rows, i, lanes=slice(0, W_MIX)):
        return z_ref[rows, i * W_MIX + lanes.start:i * W_MIX + lanes.stop]

    def project_pieces(rows):
        def norm():
            hb_ref[rows, :] = _rms(x_ref[rows, :], g_ref[...]).astype(BF16)

        def dot_piece(i):
            cols = slice(i * W_MIX, (i + 1) * W_MIX)
            z_ref[rows, cols] = _dot(hb_ref[rows, :], win_ref[:, cols])

        return [norm] + [functools.partial(dot_piece, i) for i in (3, 4, 1, 2, 0)]

    def mix_pieces(rows):
        def gates(r0):
            crow = slice(r0, r0 + ROW_CHUNK)
            for gi in range(N_GROUPS):
                lanes = slice(gi * GROUP_W, (gi + 1) * GROUP_W)
                ua_ext[gi, HIST_A + r0:HIST_A + r0 + ROW_CHUNK, :] = (
                    zseg(crow, 1, lanes) * zseg(crow, 2, lanes))
                ub_ext[gi, HIST_B + r0:HIST_B + r0 + ROW_CHUNK, :] = (
                    zseg(crow, 3, lanes) * jax.nn.sigmoid(zseg(crow, 4, lanes)))

        def convs(r0, gi):
            crow = slice(r0, r0 + CONV_ROWS)
            lanes = slice(gi * GROUP_W, (gi + 1) * GROUP_W)
            ya = zseg(crow, 0, lanes) * _causal_dwconv(ua_ext, gi, ca_ref, K_A, HIST_A, r0, CONV_ROWS)
            cat_ref[crow, lanes] = ya.astype(BF16)
            cbv_ref[crow, lanes] = _causal_dwconv(ub_ext, gi, cb_ref, K_B, HIST_B, r0, CONV_ROWS)

        def norm_act(r0):
            crow = slice(r0, r0 + CONV_ROWS)
            yb = _silu(_ln(cbv_ref[crow, :] + cbb_ref[...], lng_ref[...], lnb_ref[...]))
            cat_ref[crow, W_MIX:2 * W_MIX] = yb.astype(BF16)

        pieces = [functools.partial(gates, r0) for r0 in range(rows.start, rows.stop, ROW_CHUNK)]
        for r0 in range(rows.start, rows.stop, CONV_ROWS):
            pieces += [functools.partial(convs, r0, gi) for gi in range(N_GROUPS)]
            pieces.append(functools.partial(norm_act, r0))
        return pieces

    def out_pieces(rows):
        def piece(j):
            cols = slice(j * W_MIX, (j + 1) * W_MIX)
            o_ref[rows, cols] = x_ref[rows, cols] + _dot(cat_ref[rows, :], wout_ref[:, cols])

        return [functools.partial(piece, j) for j in range(D_MODEL // W_MIX)]

    def emit(mxu_pieces, vpu_pieces):
        n_m, n_v = len(mxu_pieces), len(vpu_pieces)
        i = j = 0
        while i < n_m or j < n_v:
            if j >= n_v or (i < n_m and i * n_v <= j * n_m):
                mxu_pieces[i]()
                i += 1
            else:
                vpu_pieces[j]()
                j += 1

    blocks = [slice(r, r + MIX_BLOCK_ROWS) for r in range(0, tm, MIX_BLOCK_ROWS)]
    emit(project_pieces(blocks[0]), [])
    for i in range(len(blocks)):
        mxu = out_pieces(blocks[i - 1]) if i > 0 else []
        if i + 1 < len(blocks):
            mxu = project_pieces(blocks[i + 1]) + mxu
        emit(mxu, mix_pieces(blocks[i]))
    emit(out_pieces(blocks[-1]), [])

    @pl.when(s == pl.num_programs(1) - 1)
    def _():
        for gi in range(N_GROUPS):
            lanes = slice(gi * GROUP_W, (gi + 1) * GROUP_W)
            sa_ref[:, lanes] = ua_ext[gi, HIST_A + tm - (K_A - 1):HIST_A + tm, :]
            sb_ref[:, lanes] = ub_ext[gi, HIST_B + tm - (K_B - 1):HIST_B + tm, :]


def _even_prompt(x, g, win, ca, cb, cbb, lng, lnb, wout, tm, cast_next=()):
    b, s, _ = x.shape
    n_s = s // tm
    tile = pl.BlockSpec((None, tm, D_MODEL), lambda i, j: (i, j, 0))
    n_in = win.shape[1]
    c_in, c_out, c_shapes = _cast_plumbing(cast_next, b * n_s, lambda i, j: i * n_s + j)
    return pl.pallas_call(
        functools.partial(_even_prompt_kernel, tm=tm, n_cast=len(cast_next)),
        grid=(b, n_s),
        in_specs=[tile, _const_spec((1, D_MODEL)), _const_spec((D_MODEL, n_in)),
                  _const_spec((K_A, W_MIX)), _const_spec((K_B, W_MIX)), _const_spec((1, W_MIX)),
                  _const_spec((1, W_MIX)), _const_spec((1, W_MIX)),
                  _const_spec((2 * W_MIX, D_MODEL))] + c_in,
        out_specs=[tile,
                   pl.BlockSpec((None, K_A - 1, W_MIX), lambda i, j: (i, 0, 0)),
                   pl.BlockSpec((None, K_B - 1, W_MIX), lambda i, j: (i, 0, 0))] + c_out,
        out_shape=[jax.ShapeDtypeStruct((b, s, D_MODEL), F32),
                   jax.ShapeDtypeStruct((b, K_A - 1, W_MIX), F32),
                   jax.ShapeDtypeStruct((b, K_B - 1, W_MIX), F32)] + c_shapes,
        scratch_shapes=[pltpu.VMEM((tm, D_MODEL), BF16),
                        pltpu.VMEM((tm, n_in), F32),
                        pltpu.VMEM((N_GROUPS, tm + HIST_A, GROUP_W), F32),
                        pltpu.VMEM((N_GROUPS, tm + HIST_B, GROUP_W), F32),
                        pltpu.VMEM((tm, W_MIX), F32),
                        pltpu.VMEM((tm, 2 * W_MIX), BF16)],
        compiler_params=_params(2),
        name="even_prompt",
    )(x, g.reshape(1, D_MODEL), win, ca, cb, cbb.reshape(1, W_MIX), lng.reshape(1, W_MIX),
      lnb.reshape(1, W_MIX), wout, *(a for a, _ in cast_next))


def _tril_bf16(w):
    t = lax.broadcasted_iota(jnp.int32, (CHUNK, CHUNK), 0)
    s = lax.broadcasted_iota(jnp.int32, (CHUNK, CHUNK), 1)
    return jnp.where(s <= t, w, 0.0).astype(BF16)


def _odd_prompt_kernel(*refs, tm, n_cast):
    (x_ref, g_ref, win_ref, wpool_ref, pscale_ref, sguw_ref, sgubt_ref, lng_ref, lnb_ref,
     wout_ref) = refs[:10]
    cast_in = refs[10:10 + n_cast]
    o_ref, sp_ref = refs[10 + n_cast:12 + n_cast]
    cast_out = refs[12 + n_cast:12 + 2 * n_cast]
    hb_ref, p_ext, ps_ref, inv_ref, u_ref, vn_ref, d_ref, cat_ref, wm_ref = refs[12 + 2 * n_cast:]
    s = pl.program_id(1)
    _cast_rows(cast_in, cast_out)

    @pl.when(jnp.logical_and(pl.program_id(0) == 0, s == 0))
    def _():
        for gi in range(N_GROUPS):
            wm_ref[gi] = _tril_bf16(sguw_ref[gi])

    @pl.when(s == 0)
    def _():
        p_ext[:, 0:HIST_P, :] = jnp.zeros((N_GROUPS, HIST_P, GROUP_W), F32)

    @pl.when(s > 0)
    def _():
        p_ext[:, 0:HIST_P, :] = p_ext[:, tm:tm + HIST_P, :]

    hb_ref[...] = _rms(x_ref[...], g_ref[...]).astype(BF16)

    def seg(i):
        return _dot(hb_ref[...], win_ref[:, i * W_MIX:(i + 1) * W_MIX])

    pc = seg(0)
    for gi in range(N_GROUPS):
        p_ext[gi, HIST_P:HIST_P + tm, :] = pc[:, gi * GROUP_W:(gi + 1) * GROUP_W]
    u_ref[...] = seg(1)
    vn_ref[...] = _ln(seg(2), lng_ref[...], lnb_ref[...]).astype(BF16)

    assert ROW_CHUNK >= max(POOL_WINDOWS)
    pos0 = s * tm + lax.broadcasted_iota(jnp.int32, (ROW_CHUNK, GROUP_W), 0)
    for gi, w in enumerate(POOL_WINDOWS):
        inv_ref[gi] = 1.0 / jnp.minimum(pos0 + 1, w).astype(F32)

    def pool_body(c, carry):
        r0 = pl.multiple_of(c * ROW_CHUNK, ROW_CHUNK)
        n = ROW_CHUNK + HIST_P
        for gi, w in enumerate(POOL_WINDOWS):
            win = p_ext.at[gi, pl.ds(r0, n), :]
            prev, lo, step = win, 0, 1
            while step < w:
                lo += SUBLANES
                sums = prev[lo:n, :] + prev[lo - step:n - step, :]
                step *= 2
                if step < w:
                    ps_ref[gi, lo:n, :] = sums
                    prev = ps_ref.at[gi]
            tok = win[HIST_P:n, :]
            inv = jnp.where(c == 0, inv_ref[gi], 1.0 / w)
            lanes = slice(gi * GROUP_W, (gi + 1) * GROUP_W)
            d_ref[pl.ds(r0, ROW_CHUNK), lanes] = (sums[HIST_P - lo:] * inv - tok).astype(BF16)
        return carry

    lax.fori_loop(0, tm // ROW_CHUNK, pool_body, 0)

    for gi in range(N_GROUPS):
        lanes = slice(gi * GROUP_W, (gi + 1) * GROUP_W)
        yc = _dot(d_ref[:, lanes], wpool_ref[gi]) * pscale_ref[:, lanes]
        cat_ref[:, lanes] = yc.astype(BF16)

    for c in range(tm // CHUNK):
        rows = slice(c * CHUNK, (c + 1) * CHUNK)
        for gi in range(N_GROUPS):
            lanes = slice(gi * GROUP_W, (gi + 1) * GROUP_W)
            mixed = _dot(wm_ref[gi], vn_ref[rows, lanes]) + sgubt_ref[:, gi:gi + 1]
            yd = u_ref[rows, lanes] * mixed
            cat_ref[rows, W_MIX + gi * GROUP_W:W_MIX + (gi + 1) * GROUP_W] = yd.astype(BF16)

    o_ref[...] = x_ref[...] + _dot(cat_ref[...], wout_ref[...])

    @pl.when(s == pl.num_programs(1) - 1)
    def _():
        for gi in range(N_GROUPS):
            sp_ref[:, gi * GROUP_W:(gi + 1) * GROUP_W] = p_ext[gi, HIST_P + tm - POOL_HIST:HIST_P + tm, :]


def _odd_prompt(x, g, win, wpool, pscale, sguw, sgub, lng, lnb, wout, tm, cast_next=()):
    b, s, _ = x.shape
    n_s = s // tm
    tile = pl.BlockSpec((None, tm, D_MODEL), lambda i, j: (i, j, 0))
    n_in = win.shape[1]
    c_in, c_out, c_shapes = _cast_plumbing(cast_next, b * n_s, lambda i, j: i * n_s + j)
    return pl.pallas_call(
        functools.partial(_odd_prompt_kernel, tm=tm, n_cast=len(cast_next)),
        grid=(b, n_s),
        in_specs=[tile, _const_spec((1, D_MODEL)), _const_spec((D_MODEL, n_in)),
                  _const_spec((N_GROUPS, GROUP_W, GROUP_W)), _const_spec((1, W_MIX)),
                  _const_spec((N_GROUPS, CHUNK, CHUNK)), _const_spec((CHUNK, N_GROUPS)),
                  _const_spec((1, W_MIX)), _const_spec((1, W_MIX)),
                  _const_spec((2 * W_MIX, D_MODEL))] + c_in,
        out_specs=[tile, pl.BlockSpec((None, POOL_HIST, W_MIX), lambda i, j: (i, 0, 0))] + c_out,
        out_shape=[jax.ShapeDtypeStruct((b, s, D_MODEL), F32),
                   jax.ShapeDtypeStruct((b, POOL_HIST, W_MIX), F32)] + c_shapes,
        scratch_shapes=[pltpu.VMEM((tm, D_MODEL), BF16),
                        pltpu.VMEM((N_GROUPS, tm + HIST_P, GROUP_W), F32),
                        pltpu.VMEM((N_GROUPS, ROW_CHUNK + HIST_P, GROUP_W), F32),
                        pltpu.VMEM((len(POOL_WINDOWS), ROW_CHUNK, GROUP_W), F32),
                        pltpu.VMEM((tm, W_MIX), F32),
                        pltpu.VMEM((tm, W_MIX), BF16),
                        pltpu.VMEM((tm, W_MIX), BF16),
                        pltpu.VMEM((tm, 2 * W_MIX), BF16),
                        pltpu.VMEM((N_GROUPS, CHUNK, CHUNK), BF16)],
        compiler_params=_params(2),
        name="odd_prompt",
    )(x, g.reshape(1, D_MODEL), win, wpool, pscale.reshape(1, W_MIX), sguw, sgub.T,
      lng.reshape(1, W_MIX), lnb.reshape(1, W_MIX), wout, *(a for a, _ in cast_next))


def _even_sample_kernel(x_ref, g_ref, win_ref, ca_ref, cb_ref, cbb_ref, lng_ref, lnb_ref, wout_ref,
                        ha_ref, hbst_ref,
                        o_ref, sa_ref, sb_ref,
                        hb_ref, gate_ref, ua_ext, ub_ext, cbv_ref, cat_ref, *, t_len, n_seq,
                        x_seq_major):
    if x_seq_major:
        x = jnp.concatenate([x_ref[:, t, :] for t in range(t_len)], axis=0)
    else:
        x = x_ref[...].reshape(t_len * n_seq, D_MODEL)
    hb_ref[...] = _rms(x, g_ref[...]).astype(BF16)

    def seg(i):
        return _dot(hb_ref[...], win_ref[:, i * W_MIX:(i + 1) * W_MIX])

    gate_ref[...] = seg(0)
    for j in range(K_A - 1):
        ua_ext[j] = ha_ref[:, j, :]
    ub_ext[0:K_B - 1] = hbst_ref[...]
    ua_ext[K_A - 1:K_A - 1 + t_len] = (seg(1) * seg(2)).reshape(t_len, n_seq, W_MIX)
    ub_ext[K_B - 1:K_B - 1 + t_len] = (seg(3) * jax.nn.sigmoid(seg(4))).reshape(t_len, n_seq, W_MIX)

    def body(t, carry):
        rows = pl.ds(pl.multiple_of(t * n_seq, n_seq), n_seq)
        for gi in range(N_GROUPS):
            lanes = slice(gi * GROUP_W, (gi + 1) * GROUP_W)
            acc = jnp.zeros((n_seq, GROUP_W), F32)
            for k in range(K_A):
                acc = acc + ca_ref[k:k + 1, lanes] * ua_ext[t + k, :, lanes]
            cat_ref[rows, lanes] = (gate_ref[rows, lanes] * acc).astype(BF16)
            acc = jnp.zeros((n_seq, GROUP_W), F32)
            for k in range(K_B):
                acc = acc + cb_ref[k:k + 1, lanes] * ub_ext[t + k, :, lanes]
            cbv_ref[:, lanes] = acc + cbb_ref[:, lanes]
        yb = _silu(_ln(cbv_ref[...], lng_ref[...], lnb_ref[...]))
        cat_ref[rows, W_MIX:2 * W_MIX] = yb.astype(BF16)
        return carry

    lax.fori_loop(0, t_len, body, 0)

    o_ref[...] = (x + _dot(cat_ref[...], wout_ref[...])).reshape(t_len, n_seq, D_MODEL)
    for j in range(K_A - 1):
        sa_ref[:, j, :] = ua_ext[t_len + j]
    sb_ref[...] = ub_ext[t_len:t_len + K_B - 1]


def _seq_block_spec(rows, nb, width):
    return pl.BlockSpec((rows, nb, width), lambda i, *_: (0, i, 0))


def _seq_major_spec(rows, nb, width):
    return pl.BlockSpec((nb, rows, width), lambda i, *_: (i, 0, 0))


def _xs_block_spec(t_len, nb):
    return pl.BlockSpec((None, t_len, nb, D_MODEL), lambda i, *_: (i, 0, 0, 0))


def _even_sample(x, g, win, ca, cb, cbb, lng, lnb, wout, hist_a, hist_b, nb, x_seq_major):
    n_seq, t_len = hist_a.shape[0], x.size // (hist_a.shape[0] * D_MODEL)
    m = t_len * nb
    n_in = win.shape[1]
    x_spec = _seq_major_spec(t_len, nb, D_MODEL) if x_seq_major else _xs_block_spec(t_len, nb)
    return pl.pallas_call(
        functools.partial(_even_sample_kernel, t_len=t_len, n_seq=nb, x_seq_major=x_seq_major),
        grid=(n_seq // nb,),
        in_specs=[x_spec, _const_spec((1, D_MODEL)),
                  _const_spec((D_MODEL, n_in)),
                  _const_spec((K_A, W_MIX)), _const_spec((K_B, W_MIX)), _const_spec((1, W_MIX)),
                  _const_spec((1, W_MIX)), _const_spec((1, W_MIX)), _const_spec((2 * W_MIX, D_MODEL)),
                  _seq_major_spec(K_A - 1, nb, W_MIX), _seq_block_spec(K_B - 1, nb, W_MIX)],
        out_specs=[_xs_block_spec(t_len, nb), _seq_major_spec(K_A - 1, nb, W_MIX),
                   _seq_block_spec(K_B - 1, nb, W_MIX)],
        out_shape=[jax.ShapeDtypeStruct((n_seq // nb, t_len, nb, D_MODEL), F32),
                   jax.ShapeDtypeStruct((n_seq, K_A - 1, W_MIX), F32),
                   jax.ShapeDtypeStruct((K_B - 1, n_seq, W_MIX), F32)],
        scratch_shapes=[pltpu.VMEM((m, D_MODEL), BF16),
                        pltpu.VMEM((m, W_MIX), F32),
                        pltpu.VMEM((t_len + K_A - 1, nb, W_MIX), F32),
                        pltpu.VMEM((t_len + K_B - 1, nb, W_MIX), F32),
                        pltpu.VMEM((nb, W_MIX), F32),
                        pltpu.VMEM((m, 2 * W_MIX), BF16)],
        compiler_params=_params(1),
        name="even_sample",
    )(x, g.reshape(1, D_MODEL), win, ca, cb, cbb.reshape(1, W_MIX), lng.reshape(1, W_MIX),
      lnb.reshape(1, W_MIX), wout, hist_a, hist_b)


def _odd_sample_kernel(sw_ref, sb_ref_smem,
                       x_ref, g_ref, win_ref, wpool_ref, pscale_ref, lng_ref, lnb_ref, wout_ref,
                       hp_ref,
                       o_ref, sp_ref, vn_out_ref,
                       hb_ref, p_ext, u_ref, vn_ref, d_ref, cat_ref, *, t_len, n_seq, start_pos):
    x = x_ref[...].reshape(t_len * n_seq, D_MODEL)
    hb_ref[...] = _rms(x, g_ref[...]).astype(BF16)

    def seg(i):
        return _dot(hb_ref[...], win_ref[:, i * W_MIX:(i + 1) * W_MIX])

    p_ext[0:POOL_HIST] = hp_ref[...]
    p_ext[POOL_HIST:POOL_HIST + t_len] = seg(0).reshape(t_len, n_seq, W_MIX)
    u_ref[...] = seg(1).reshape(t_len, n_seq, W_MIX)
    vn_ref[...] = _ln(seg(2), lng_ref[...], lnb_ref[...]).reshape(t_len, n_seq, W_MIX)

    for t in range(t_len):
        rows = slice(t * n_seq, (t + 1) * n_seq)
        for gi, w in enumerate(POOL_WINDOWS):
            lanes = slice(gi * GROUP_W, (gi + 1) * GROUP_W)
            tok = p_ext[POOL_HIST + t, :, lanes]
            wsum = tok
            for j in range(1, w):
                wsum = wsum + p_ext[POOL_HIST + t - j, :, lanes]
            cnt = float(min(start_pos + t + 1, w))
            d_ref[rows, lanes] = (wsum / cnt - tok).astype(BF16)
            mixed = jnp.zeros((n_seq, GROUP_W), F32) + sb_ref_smem[gi * t_len + t]
            for s in range(t + 1):
                mixed = mixed + sw_ref[(gi * t_len + t) * t_len + s] * vn_ref[s, :, lanes]
            cat_ref[rows, W_MIX + gi * GROUP_W:W_MIX + (gi + 1) * GROUP_W] = (
                u_ref[t, :, lanes] * mixed).astype(BF16)

    for gi in range(N_GROUPS):
        lanes = slice(gi * GROUP_W, (gi + 1) * GROUP_W)
        yc = _dot(d_ref[:, lanes], wpool_ref[gi]) * pscale_ref[:, lanes]
        cat_ref[:, lanes] = yc.astype(BF16)

    o_ref[...] = (x + _dot(cat_ref[...], wout_ref[...])).reshape(t_len, n_seq, D_MODEL)
    sp_ref[...] = p_ext[t_len:t_len + POOL_HIST]
    for t in range(t_len):
        vn_out_ref[:, t, :] = vn_ref[t]


def _odd_sample(x, g, win, wpool, pscale, sguw, sgub, lng, lnb, wout, hist_p, nb, start_pos):
    n_blocks, t_len, _, _ = x.shape
    n_seq = n_blocks * nb
    m = t_len * nb
    n_in = win.shape[1]
    assert t_len <= CHUNK
    sw = sguw[:, :t_len, :t_len].reshape(-1)
    sb = sgub[:, :t_len].reshape(-1)

    grid_spec = pltpu.PrefetchScalarGridSpec(
        num_scalar_prefetch=2,
        grid=(n_seq // nb,),
        in_specs=[_xs_block_spec(t_len, nb), _const_spec((1, D_MODEL)),
                  _const_spec((D_MODEL, n_in)),
                  _const_spec((N_GROUPS, GROUP_W, GROUP_W)), _const_spec((1, W_MIX)),
                  _const_spec((1, W_MIX)), _const_spec((1, W_MIX)),
                  _const_spec((2 * W_MIX, D_MODEL)),
                  _seq_block_spec(POOL_HIST, nb, W_MIX)],
        out_specs=[_xs_block_spec(t_len, nb), _seq_block_spec(POOL_HIST, nb, W_MIX),
                   _seq_major_spec(t_len, nb, W_MIX)],
        scratch_shapes=[pltpu.VMEM((m, D_MODEL), BF16),
                        pltpu.VMEM((t_len + POOL_HIST, nb, W_MIX), F32),
                        pltpu.VMEM((t_len, nb, W_MIX), F32),
                        pltpu.VMEM((t_len, nb, W_MIX), F32),
                        pltpu.VMEM((m, W_MIX), BF16),
                        pltpu.VMEM((m, 2 * W_MIX), BF16)],
    )
    return pl.pallas_call(
        functools.partial(_odd_sample_kernel, t_len=t_len, n_seq=nb, start_pos=start_pos),
        grid_spec=grid_spec,
        out_shape=[jax.ShapeDtypeStruct((n_blocks, t_len, nb, D_MODEL), F32),
                   jax.ShapeDtypeStruct((POOL_HIST, n_seq, W_MIX), F32),
                   jax.ShapeDtypeStruct((n_seq, t_len, W_MIX), F32)],
        compiler_params=_params(1),
        name="odd_sample",
    )(sw, sb, x, g.reshape(1, D_MODEL), win, wpool, pscale.reshape(1, W_MIX),
      lng.reshape(1, W_MIX), lnb.reshape(1, W_MIX), wout, hist_p)


def _time_major(a):
    return jnp.swapaxes(a, 0, 1)


def kernel(x_prompt, x_sample, state_conv_a, state_conv_b, state_pool, norm_mix, norm_ffn, ev_w_in,
           ev_conv_a, ev_conv_b, ev_conv_b_bias, ev_ln_g, ev_ln_b, ev_w_out, od_w_in, od_pool_w,
           od_pool_scale, od_sgu_w, od_sgu_b, od_sgu_ln_g, od_sgu_ln_b, od_w_out, ffn_w1, ffn_w2,
           norm_final):
    depth = norm_mix.shape[0]
    batch, seq, _ = x_prompt.shape
    n_seq, t_len, _ = x_sample.shape
    tm_mix = 1024
    tm_ffn = 512
    nb = tm_ffn // t_len

    def mixer_weights(l):
        i = l // 2
        return ((ev_w_in, i), (ev_w_out, i)) if l % 2 == 0 else ((od_w_in, i), (od_w_out, i))

    xp = x_prompt
    xs = x_sample
    sa_p, sa_s, sb_p, sb_s, sc_p, sc_s, vn_s = [], [], [], [], [], [], []
    win, wout = (w[i].astype(BF16) for w, i in mixer_weights(0))
    for l in range(depth):
        i = l // 2
        ffn_f32 = ((ffn_w1, l), (ffn_w2, l))
        if l % 2 == 0:
            common = (norm_mix[l], win, ev_conv_a[i], ev_conv_b[i], ev_conv_b_bias[i], ev_ln_g[i],
                      ev_ln_b[i], wout)
            xp, a_p, b_p, w1, w2 = _even_prompt(xp, *common, tm=tm_mix, cast_next=ffn_f32)
            xs, a_s, b_s = _even_sample(xs, *common, state_conv_a[i], _time_major(state_conv_b[i]),
                                        nb, x_seq_major=(l == 0))
            sa_p.append(a_p)
            sb_p.append(b_p)
            sa_s.append(a_s)
            sb_s.append(_time_major(b_s))
        else:
            common = (norm_mix[l], win, od_pool_w[i].astype(BF16), od_pool_scale[i], od_sgu_w[i],
                      od_sgu_b[i], od_sgu_ln_g[i], od_sgu_ln_b[i], wout)
            xp, c_p, w1, w2 = _odd_prompt(xp, *common, tm=tm_mix, cast_next=ffn_f32)
            xs, c_s, v_s = _odd_sample(xs, *common, _time_major(state_pool[i]), nb, PAST_LEN)
            sc_p.append(c_p)
            sc_s.append(_time_major(c_s))
            vn_s.append(v_s)
        last = l == depth - 1
        xp, xs, *next_w = _ffn(xp.reshape(batch * seq, D_MODEL), xs.reshape(t_len * n_seq, D_MODEL),
                               norm_ffn[l], w1, w2, norm_final if last else None, tm_ffn, t_len,
                               cast_next=() if last else mixer_weights(l + 1))
        xp = xp.reshape(batch, seq, D_MODEL)
        if not last:
            xs = xs.reshape(n_seq // nb, t_len, nb, D_MODEL)
            win, wout = next_w

    return (xp, xs, jnp.stack(sa_p), jnp.stack(sa_s), jnp.stack(sb_p), jnp.stack(sb_s),
            jnp.stack(sc_p), jnp.stack(sc_s), jnp.stack(vn_s))
```

```python
import functools

import jax
import jax.numpy as jnp
from jax import lax
from jax.experimental import pallas as pl
from jax.experimental.pallas import tpu as pltpu

F32 = jnp.float32
BF16 = jnp.bfloat16

D_MODEL = 1024
W_MIX = D_MODEL // 2
K_A = 3
K_B = 31
POOL_WINDOWS = (2, 4, 8, 16)
POOL_HIST = max(POOL_WINDOWS) - 1
CHUNK = 128
N_GROUPS = 4
GROUP_W = W_MIX // N_GROUPS
D_FF = 4 * D_MODEL
PAST_LEN = 16384
EPS = 1e-6

SUBLANES = 8
HIST_A = 8
HIST_B = 32
HIST_P = SUBLANES * len(POOL_WINDOWS)
ROW_CHUNK = 64
CONV_ROWS = 64
MIX_BLOCK_ROWS = 256
SAMPLE_CONV_STEPS = 4
FFN_COL_CHUNK = 1024
VMEM_LIMIT = 56 * 1024 * 1024


def _rms(x, g):
    return x * lax.rsqrt(jnp.mean(x * x, axis=-1, keepdims=True) + EPS) * g


def _ln(x, g, b):
    mu = jnp.mean(x, axis=-1, keepdims=True)
    xc = x - mu
    return xc * lax.rsqrt(jnp.mean(xc * xc, axis=-1, keepdims=True) + EPS) * g + b


def _dot(a, b):
    return jnp.dot(a, b, preferred_element_type=F32)


def _silu(x):
    return x * jax.nn.sigmoid(x)


def _const_spec(shape):
    nd = len(shape)
    return pl.BlockSpec(shape, lambda *_: (0,) * nd, pipeline_mode=pl.Buffered(1))


def _params(n_grid_axes):
    return pltpu.CompilerParams(
        dimension_semantics=("arbitrary",) * n_grid_axes,
        vmem_limit_bytes=VMEM_LIMIT,
    )


def _cast_plumbing(arrays, n_steps, step_of):
    in_specs, out_specs, shapes = [], [], []
    for stacked, layer in arrays:
        _, n_rows, n_cols = stacked.shape
        rows, rem = divmod(n_rows, n_steps)
        assert rem == 0 and rows % (2 * SUBLANES) == 0, stacked.shape
        in_specs.append(pl.BlockSpec((None, rows, n_cols),
                                     lambda *idx, _l=layer: (_l, step_of(*idx), 0)))
        out_specs.append(pl.BlockSpec((rows, n_cols), lambda *idx: (step_of(*idx), 0)))
        shapes.append(jax.ShapeDtypeStruct((n_rows, n_cols), BF16))
    return in_specs, out_specs, shapes


def _cast_rows(src_refs, dst_refs):
    for src, dst in zip(src_refs, dst_refs, strict=True):
        dst[...] = src[...].astype(BF16)


def _ffn_kernel(*refs, final, n_prompt, n_cast, t_len):
    n_in = 6 if final else 5
    xp_ref, xs_ref, g_ref, w1_ref, w2_ref = refs[:5]
    cast_in = refs[n_in:n_in + n_cast]
    op_ref, os_ref = refs[n_in + n_cast:n_in + n_cast + 2]
    cast_out = refs[n_in + n_cast + 2:n_in + 2 * n_cast + 2]
    hb_ref, act_ref = refs[n_in + 2 * n_cast + 2:]
    step = pl.program_id(0)

    def rows_block(x_ref):
        hb_ref[...] = _rms(x_ref[...], g_ref[...]).astype(BF16)
        for c in range(D_FF // FFN_COL_CHUNK):
            cols = slice(c * FFN_COL_CHUNK, (c + 1) * FFN_COL_CHUNK)
            a = _dot(hb_ref[...], w1_ref[:, cols])
            act_ref[:, cols] = jnp.square(jnp.maximum(a, 0.0)).astype(BF16)
        y = x_ref[...] + _dot(act_ref[...], w2_ref[...])
        return _rms(y, refs[5][...]) if final else y

    @pl.when(step < n_prompt)
    def _():
        op_ref[...] = rows_block(xp_ref)
        _cast_rows(cast_in, cast_out)

    @pl.when(step >= n_prompt)
    def _():
        y = rows_block(xs_ref)
        if final:
            nb = y.shape[0] // t_len
            for t in range(t_len):
                os_ref[:, t, :] = y[t * nb:(t + 1) * nb]
        else:
            os_ref[...] = y


def _ffn(xp2d, xs2d, g, w1, w2, g_final, tm, t_len, cast_next=()):
    n_prompt, n_sample = xp2d.shape[0] // tm, xs2d.shape[0] // tm
    final = g_final is not None
    last = n_prompt - 1
    p_spec = pl.BlockSpec((tm, D_MODEL), lambda i: (jnp.minimum(i, last), 0))
    s_spec = pl.BlockSpec((tm, D_MODEL), lambda i: (jnp.maximum(i - n_prompt, 0), 0))
    in_specs = [p_spec, s_spec, _const_spec((1, D_MODEL)), _const_spec((D_MODEL, D_FF)),
                _const_spec((D_FF, D_MODEL))]
    args = [xp2d, xs2d, g.reshape(1, D_MODEL), w1, w2]
    s_out_spec, s_out_shape = s_spec, jax.ShapeDtypeStruct(xs2d.shape, F32)
    if final:
        in_specs.append(_const_spec((1, D_MODEL)))
        args.append(g_final.reshape(1, D_MODEL))
        nb = tm // t_len
        s_out_spec = pl.BlockSpec((nb, t_len, D_MODEL), lambda i: (jnp.maximum(i - n_prompt, 0), 0, 0))
        s_out_shape = jax.ShapeDtypeStruct((xs2d.shape[0] // t_len, t_len, D_MODEL), F32)
    c_in, c_out, c_shapes = _cast_plumbing(cast_next, n_prompt, lambda i: jnp.minimum(i, last))
    return pl.pallas_call(
        functools.partial(_ffn_kernel, final=final, n_prompt=n_prompt, n_cast=len(cast_next),
                          t_len=t_len),
        grid=(n_prompt + n_sample,),
        in_specs=in_specs + c_in,
        out_specs=[p_spec, s_out_spec] + c_out,
        out_shape=[jax.ShapeDtypeStruct(xp2d.shape, F32), s_out_shape] + c_shapes,
        scratch_shapes=[pltpu.VMEM((tm, D_MODEL), BF16), pltpu.VMEM((tm, D_FF), BF16)],
        compiler_params=_params(1),
        name="ffn_final" if final else "ffn",
    )(*args, *(a for a, _ in cast_next))


def _causal_dwconv(ext_ref, gi, w_ref, n_taps, hist, r0, rows):
    lanes = slice(gi * GROUP_W, (gi + 1) * GROUP_W)
    acc = None
    for k in range(n_taps):
        lo = hist + r0 - (n_taps - 1) + k
        term = w_ref[k:k + 1, lanes] * ext_ref[gi, lo:lo + rows, :]
        acc = term if acc is None else acc + term
    return acc


def _even_prompt_kernel(*refs, tm, n_cast):
    x_ref, g_ref, win_ref, ca_ref, cb_ref, cbb_ref, lng_ref, lnb_ref, wout_ref = refs[:9]
    cast_in = refs[9:9 + n_cast]
    o_ref, sa_ref, sb_ref = refs[9 + n_cast:12 + n_cast]
    cast_out = refs[12 + n_cast:12 + 2 * n_cast]
    hb_ref, z_ref, ua_ext, ub_ext, cbv_ref, cat_ref = refs[12 + 2 * n_cast:]
    s = pl.program_id(1)
    _cast_rows(cast_in, cast_out)

    @pl.when(s == 0)
    def _():
        ua_ext[:, 0:HIST_A, :] = jnp.zeros((N_GROUPS, HIST_A, GROUP_W), F32)
        ub_ext[:, 0:HIST_B, :] = jnp.zeros((N_GROUPS, HIST_B, GROUP_W), F32)

    @pl.when(s > 0)
    def _():
        ua_ext[:, 0:HIST_A, :] = ua_ext[:, tm:tm + HIST_A, :]
        ub_ext[:, 0:HIST_B, :] = ub_ext[:, tm:tm + HIST_B, :]

    def zseg(rows, i, lanes=slice(0, W_MIX)):
        return z_ref[rows, i * W_MIX + lanes.start:i * W_MIX + lanes.stop]

    def project_pieces(rows):
        def norm():
            hb_ref[rows, :] = _rms(x_ref[rows, :], g_ref[...]).astype(BF16)

        def dot_piece(i):
            cols = slice(i * W_MIX, (i + 1) * W_MIX)
            z_ref[rows, cols] = _dot(hb_ref[rows, :], win_ref[:, cols])

        return [norm] + [functools.partial(dot_piece, i) for i in (3, 4, 1, 2, 0)]

    def mix_pieces(rows):
        def gates(r0):
            crow = slice(r0, r0 + ROW_CHUNK)
            for gi in range(N_GROUPS):
                lanes = slice(gi * GROUP_W, (gi + 1) * GROUP_W)
                ua_ext[gi, HIST_A + r0:HIST_A + r0 + ROW_CHUNK, :] = (
                    zseg(crow, 1, lanes) * zseg(crow, 2, lanes))
                ub_ext[gi, HIST_B + r0:HIST_B + r0 + ROW_CHUNK, :] = (
                    zseg(crow, 3, lanes) * jax.nn.sigmoid(zseg(crow, 4, lanes)))

        def convs(r0, gi):
            crow = slice(r0, r0 + CONV_ROWS)
            lanes = slice(gi * GROUP_W, (gi + 1) * GROUP_W)
            ya = zseg(crow, 0, lanes) * _causal_dwconv(ua_ext, gi, ca_ref, K_A, HIST_A, r0, CONV_ROWS)
            cat_ref[crow, lanes] = ya.astype(BF16)
            cbv_ref[crow, lanes] = _causal_dwconv(ub_ext, gi, cb_ref, K_B, HIST_B, r0, CONV_ROWS)

        def norm_act(r0):
            crow = slice(r0, r0 + CONV_ROWS)
            yb = _silu(_ln(cbv_ref[crow, :] + cbb_ref[...], lng_ref[...], lnb_ref[...]))
            cat_ref[crow, W_MIX:2 * W_MIX] = yb.astype(BF16)

        pieces = [functools.partial(gates, r0) for r0 in range(rows.start, rows.stop, ROW_CHUNK)]
        for r0 in range(rows.start, rows.stop, CONV_ROWS):
            pieces += [functools.partial(convs, r0, gi) for gi in range(N_GROUPS)]
            pieces.append(functools.partial(norm_act, r0))
        return pieces

    def out_pieces(rows):
        def piece(j):
            cols = slice(j * W_MIX, (j + 1) * W_MIX)
            o_ref[rows, cols] = x_ref[rows, cols] + _dot(cat_ref[rows, :], wout_ref[:, cols])

        return [functools.partial(piece, j) for j in range(D_MODEL // W_MIX)]

    def emit(mxu_pieces, vpu_pieces):
        n_m, n_v = len(mxu_pieces), len(vpu_pieces)
        i = j = 0
        while i < n_m or j < n_v:
            if j >= n_v or (i < n_m and i * n_v <= j * n_m):
                mxu_pieces[i]()
                i += 1
            else:
                vpu_pieces[j]()
                j += 1

    blocks = [slice(r, r + MIX_BLOCK_ROWS) for r in range(0, tm, MIX_BLOCK_ROWS)]
    emit(project_pieces(blocks[0]), [])
    for i in range(len(blocks)):
        mxu = out_pieces(blocks[i - 1]) if i > 0 else []
        if i + 1 < len(blocks):
            mxu = project_pieces(blocks[i + 1]) + mxu
        emit(mxu, mix_pieces(blocks[i]))
    emit(out_pieces(blocks[-1]), [])

    @pl.when(s == pl.num_programs(1) - 1)
    def _():
        for gi in range(N_GROUPS):
            lanes = slice(gi * GROUP_W, (gi + 1) * GROUP_W)
            sa_ref[:, lanes] = ua_ext[gi, HIST_A + tm - (K_A - 1):HIST_A + tm, :]
            sb_ref[:, lanes] = ub_ext[gi, HIST_B + tm - (K_B - 1):HIST_B + tm, :]


def _even_prompt(x, g, win, ca, cb, cbb, lng, lnb, wout, tm, cast_next=()):
    b, s, _ = x.shape
    n_s = s // tm
    tile = pl.BlockSpec((None, tm, D_MODEL), lambda i, j: (i, j, 0))
    n_in = win.shape[1]
    c_in, c_out, c_shapes = _cast_plumbing(cast_next, b * n_s, lambda i, j: i * n_s + j)
    return pl.pallas_call(
        functools.partial(_even_prompt_kernel, tm=tm, n_cast=len(cast_next)),
        grid=(b, n_s),
        in_specs=[tile, _const_spec((1, D_MODEL)), _const_spec((D_MODEL, n_in)),
                  _const_spec((K_A, W_MIX)), _const_spec((K_B, W_MIX)), _const_spec((1, W_MIX)),
                  _const_spec((1, W_MIX)), _const_spec((1, W_MIX)),
                  _const_spec((2 * W_MIX, D_MODEL))] + c_in,
        out_specs=[tile,
                   pl.BlockSpec((None, K_A - 1, W_MIX), lambda i, j: (i, 0, 0)),
                   pl.BlockSpec((None, K_B - 1, W_MIX), lambda i, j: (i, 0, 0))] + c_out,
        out_shape=[jax.ShapeDtypeStruct((b, s, D_MODEL), F32),
                   jax.ShapeDtypeStruct((b, K_A - 1, W_MIX), F32),
                   jax.ShapeDtypeStruct((b, K_B - 1, W_MIX), F32)] + c_shapes,
        scratch_shapes=[pltpu.VMEM((tm, D_MODEL), BF16),
                        pltpu.VMEM((tm, n_in), F32),
                        pltpu.VMEM((N_GROUPS, tm + HIST_A, GROUP_W), F32),
                        pltpu.VMEM((N_GROUPS, tm + HIST_B, GROUP_W), F32),
                        pltpu.VMEM((tm, W_MIX), F32),
                        pltpu.VMEM((tm, 2 * W_MIX), BF16)],
        compiler_params=_params(2),
        name="even_prompt",
    )(x, g.reshape(1, D_MODEL), win, ca, cb, cbb.reshape(1, W_MIX), lng.reshape(1, W_MIX),
      lnb.reshape(1, W_MIX), wout, *(a for a, _ in cast_next))


def _tril_bf16(w):
    t = lax.broadcasted_iota(jnp.int32, (CHUNK, CHUNK), 0)
    s = lax.broadcasted_iota(jnp.int32, (CHUNK, CHUNK), 1)
    return jnp.where(s <= t, w, 0.0).astype(BF16)


def _odd_prompt_kernel(*refs, tm, n_cast):
    (x_ref, g_ref, win_ref, wpool_ref, pscale_ref, sguw_ref, sgubt_ref, lng_ref, lnb_ref,
     wout_ref) = refs[:10]
    cast_in = refs[10:10 + n_cast]
    o_ref, sp_ref = refs[10 + n_cast:12 + n_cast]
    cast_out = refs[12 + n_cast:12 + 2 * n_cast]
    hb_ref, p_ext, ps_ref, inv_ref, u_ref, vn_ref, d_ref, cat_ref, wm_ref = refs[12 + 2 * n_cast:]
    s = pl.program_id(1)
    _cast_rows(cast_in, cast_out)

    @pl.when(jnp.logical_and(pl.program_id(0) == 0, s == 0))
    def _():
        for gi in range(N_GROUPS):
            wm_ref[gi] = _tril_bf16(sguw_ref[gi])

    @pl.when(s == 0)
    def _():
        p_ext[:, 0:HIST_P, :] = jnp.zeros((N_GROUPS, HIST_P, GROUP_W), F32)

    @pl.when(s > 0)
    def _():
        p_ext[:, 0:HIST_P, :] = p_ext[:, tm:tm + HIST_P, :]

    hb_ref[...] = _rms(x_ref[...], g_ref[...]).astype(BF16)

    def seg(i):
        return _dot(hb_ref[...], win_ref[:, i * W_MIX:(i + 1) * W_MIX])

    pc = seg(0)
    for gi in range(N_GROUPS):
        p_ext[gi, HIST_P:HIST_P + tm, :] = pc[:, gi * GROUP_W:(gi + 1) * GROUP_W]
    u_ref[...] = seg(1)
    vn_ref[...] = _ln(seg(2), lng_ref[...], lnb_ref[...]).astype(BF16)

    assert ROW_CHUNK >= max(POOL_WINDOWS)
    pos0 = s * tm + lax.broadcasted_iota(jnp.int32, (ROW_CHUNK, GROUP_W), 0)
    for gi, w in enumerate(POOL_WINDOWS):
        inv_ref[gi] = 1.0 / jnp.minimum(pos0 + 1, w).astype(F32)

    def pool_body(c, carry):
        r0 = pl.multiple_of(c * ROW_CHUNK, ROW_CHUNK)
        n = ROW_CHUNK + HIST_P
        for gi, w in enumerate(POOL_WINDOWS):
            win = p_ext.at[gi, pl.ds(r0, n), :]
            prev, lo, step = win, 0, 1
            sums = win[0:n, :]
            while step < w:
                lo += SUBLANES
                sums = sums[SUBLANES:] + prev[lo - step:n - step, :]
                step *= 2
                if step < w:
                    ps_ref[gi, lo:n, :] = sums
                    prev = ps_ref.at[gi]
            tok = win[HIST_P:n, :]
            inv = jnp.where(c == 0, inv_ref[gi], 1.0 / w)
            lanes = slice(gi * GROUP_W, (gi + 1) * GROUP_W)
            d_ref[pl.ds(r0, ROW_CHUNK), lanes] = (sums[HIST_P - lo:] * inv - tok).astype(BF16)
        return carry

    lax.fori_loop(0, tm // ROW_CHUNK, pool_body, 0)

    for gi in range(N_GROUPS):
        lanes = slice(gi * GROUP_W, (gi + 1) * GROUP_W)
        yc = _dot(d_ref[:, lanes], wpool_ref[gi]) * pscale_ref[:, lanes]
        cat_ref[:, lanes] = yc.astype(BF16)

    for c in range(tm // CHUNK):
        rows = slice(c * CHUNK, (c + 1) * CHUNK)
        for gi in range(N_GROUPS):
            lanes = slice(gi * GROUP_W, (gi + 1) * GROUP_W)
            mixed = _dot(wm_ref[gi], vn_ref[rows, lanes]) + sgubt_ref[:, gi:gi + 1]
            yd = u_ref[rows, lanes] * mixed
            cat_ref[rows, W_MIX + gi * GROUP_W:W_MIX + (gi + 1) * GROUP_W] = yd.astype(BF16)

    o_ref[...] = x_ref[...] + _dot(cat_ref[...], wout_ref[...])

    @pl.when(s == pl.num_programs(1) - 1)
    def _():
        for gi in range(N_GROUPS):
            sp_ref[:, gi * GROUP_W:(gi + 1) * GROUP_W] = p_ext[gi, HIST_P + tm - POOL_HIST:HIST_P + tm, :]


def _odd_prompt(x, g, win, wpool, pscale, sguw, sgub, lng, lnb, wout, tm, cast_next=()):
    b, s, _ = x.shape
    n_s = s // tm
    tile = pl.BlockSpec((None, tm, D_MODEL), lambda i, j: (i, j, 0))
    n_in = win.shape[1]
    c_in, c_out, c_shapes = _cast_plumbing(cast_next, b * n_s, lambda i, j: i * n_s + j)
    return pl.pallas_call(
        functools.partial(_odd_prompt_kernel, tm=tm, n_cast=len(cast_next)),
        grid=(b, n_s),
        in_specs=[tile, _const_spec((1, D_MODEL)), _const_spec((D_MODEL, n_in)),
                  _const_spec((N_GROUPS, GROUP_W, GROUP_W)), _const_spec((1, W_MIX)),
                  _const_spec((N_GROUPS, CHUNK, CHUNK)), _const_spec((CHUNK, N_GROUPS)),
                  _const_spec((1, W_MIX)), _const_spec((1, W_MIX)),
                  _const_spec((2 * W_MIX, D_MODEL))] + c_in,
        out_specs=[tile, pl.BlockSpec((None, POOL_HIST, W_MIX), lambda i, j: (i, 0, 0))] + c_out,
        out_shape=[jax.ShapeDtypeStruct((b, s, D_MODEL), F32),
                   jax.ShapeDtypeStruct((b, POOL_HIST, W_MIX), F32)] + c_shapes,
        scratch_shapes=[pltpu.VMEM((tm, D_MODEL), BF16),
                        pltpu.VMEM((N_GROUPS, tm + HIST_P, GROUP_W), F32),
                        pltpu.VMEM((N_GROUPS, ROW_CHUNK + HIST_P, GROUP_W), F32),
                        pltpu.VMEM((len(POOL_WINDOWS), ROW_CHUNK, GROUP_W), F32),
                        pltpu.VMEM((tm, W_MIX), F32),
                        pltpu.VMEM((tm, W_MIX), BF16),
                        pltpu.VMEM((tm, W_MIX), BF16),
                        pltpu.VMEM((tm, 2 * W_MIX), BF16),
                        pltpu.VMEM((N_GROUPS, CHUNK, CHUNK), BF16)],
        compiler_params=_params(2),
        name="odd_prompt",
    )(x, g.reshape(1, D_MODEL), win, wpool, pscale.reshape(1, W_MIX), sguw, sgub.T,
      lng.reshape(1, W_MIX), lnb.reshape(1, W_MIX), wout, *(a for a, _ in cast_next))


def _even_sample_kernel(x_ref, g_ref, win_ref, ca_ref, cb_ref, cbb_ref, lng_ref, lnb_ref, wout_ref,
                        ha_ref, hbst_ref,
                        o_ref, sa_ref, sb_ref,
                        hb_ref, gate_ref, ua_ext, ub_ext, cbv_ref, cat_ref, *, t_len, n_seq,
                        x_seq_major):
    if x_seq_major:
        x = jnp.concatenate([x_ref[:, t, :] for t in range(t_len)], axis=0)
    else:
        x = x_ref[...].reshape(t_len * n_seq, D_MODEL)
    hb_ref[...] = _rms(x, g_ref[...]).astype(BF16)

    def seg(i):
        return _dot(hb_ref[...], win_ref[:, i * W_MIX:(i + 1) * W_MIX])

    gate_ref[...] = seg(0)
    for j in range(K_A - 1):
        ua_ext[j] = ha_ref[:, j, :]
    ub_ext[0:K_B - 1] = hbst_ref[...]
    ua_ext[K_A - 1:K_A - 1 + t_len] = (seg(1) * seg(2)).reshape(t_len, n_seq, W_MIX)
    ub_ext[K_B - 1:K_B - 1 + t_len] = (seg(3) * jax.nn.sigmoid(seg(4))).reshape(t_len, n_seq, W_MIX)

    def taps(ext_ref, w_ref, n_taps, t0, lanes):
        acc = None
        for k in range(n_taps):
            term = w_ref[k:k + 1, lanes] * ext_ref[t0 + k:t0 + k + SAMPLE_CONV_STEPS, :, lanes]
            acc = term if acc is None else acc + term
        return acc.reshape(SAMPLE_CONV_STEPS * n_seq, GROUP_W)

    for t0 in range(0, t_len, SAMPLE_CONV_STEPS):
        rows = slice(t0 * n_seq, (t0 + SAMPLE_CONV_STEPS) * n_seq)
        for gi in range(N_GROUPS):
            lanes = slice(gi * GROUP_W, (gi + 1) * GROUP_W)
            cat_ref[rows, lanes] = (gate_ref[rows, lanes] * taps(ua_ext, ca_ref, K_A, t0, lanes)).astype(BF16)
            cbv_ref[rows, lanes] = taps(ub_ext, cb_ref, K_B, t0, lanes) + cbb_ref[:, lanes]
        yb = _silu(_ln(cbv_ref[rows, :], lng_ref[...], lnb_ref[...]))
        cat_ref[rows, W_MIX:2 * W_MIX] = yb.astype(BF16)

    o_ref[...] = (x + _dot(cat_ref[...], wout_ref[...])).reshape(t_len, n_seq, D_MODEL)
    for j in range(K_A - 1):
        sa_ref[:, j, :] = ua_ext[t_len + j]
    sb_ref[...] = ub_ext[t_len:t_len + K_B - 1]


def _seq_block_spec(rows, nb, width):
    return pl.BlockSpec((rows, nb, width), lambda i, *_: (0, i, 0))


def _seq_major_spec(rows, nb, width):
    return pl.BlockSpec((nb, rows, width), lambda i, *_: (i, 0, 0))


def _xs_block_spec(t_len, nb):
    return pl.BlockSpec((None, t_len, nb, D_MODEL), lambda i, *_: (i, 0, 0, 0))


def _even_sample(x, g, win, ca, cb, cbb, lng, lnb, wout, hist_a, hist_b, nb, x_seq_major):
    n_seq, t_len = hist_a.shape[0], x.size // (hist_a.shape[0] * D_MODEL)
    m = t_len * nb
    n_in = win.shape[1]
    x_spec = _seq_major_spec(t_len, nb, D_MODEL) if x_seq_major else _xs_block_spec(t_len, nb)
    return pl.pallas_call(
        functools.partial(_even_sample_kernel, t_len=t_len, n_seq=nb, x_seq_major=x_seq_major),
        grid=(n_seq // nb,),
        in_specs=[x_spec, _const_spec((1, D_MODEL)),
                  _const_spec((D_MODEL, n_in)),
                  _const_spec((K_A, W_MIX)), _const_spec((K_B, W_MIX)), _const_spec((1, W_MIX)),
                  _const_spec((1, W_MIX)), _const_spec((1, W_MIX)), _const_spec((2 * W_MIX, D_MODEL)),
                  _seq_major_spec(K_A - 1, nb, W_MIX), _seq_block_spec(K_B - 1, nb, W_MIX)],
        out_specs=[_xs_block_spec(t_len, nb), _seq_major_spec(K_A - 1, nb, W_MIX),
                   _seq_block_spec(K_B - 1, nb, W_MIX)],
        out_shape=[jax.ShapeDtypeStruct((n_seq // nb, t_len, nb, D_MODEL), F32),
                   jax.ShapeDtypeStruct((n_seq, K_A - 1, W_MIX), F32),
                   jax.ShapeDtypeStruct((K_B - 1, n_seq, W_MIX), F32)],
        scratch_shapes=[pltpu.VMEM((m, D_MODEL), BF16),
                        pltpu.VMEM((m, W_MIX), F32),
                        pltpu.VMEM((t_len + K_A - 1, nb, W_MIX), F32),
                        pltpu.VMEM((t_len + K_B - 1, nb, W_MIX), F32),
                        pltpu.VMEM((m, W_MIX), F32),
                        pltpu.VMEM((m, 2 * W_MIX), BF16)],
        compiler_params=_params(1),
        name="even_sample",
    )(x, g.reshape(1, D_MODEL), win, ca, cb, cbb.reshape(1, W_MIX), lng.reshape(1, W_MIX),
      lnb.reshape(1, W_MIX), wout, hist_a, hist_b)


def _odd_sample_kernel(sw_ref, sb_ref_smem,
                       x_ref, g_ref, win_ref, wpool_ref, pscale_ref, lng_ref, lnb_ref, wout_ref,
                       hp_ref,
                       o_ref, sp_ref, vn_out_ref,
                       hb_ref, p_ext, u_ref, vn_ref, d_ref, cat_ref, *, t_len, n_seq, start_pos):
    x = x_ref[...].reshape(t_len * n_seq, D_MODEL)
    hb_ref[...] = _rms(x, g_ref[...]).astype(BF16)

    def seg(i):
        return _dot(hb_ref[...], win_ref[:, i * W_MIX:(i + 1) * W_MIX])

    p_ext[0:POOL_HIST] = hp_ref[...]
    p_ext[POOL_HIST:POOL_HIST + t_len] = seg(0).reshape(t_len, n_seq, W_MIX)
    u_ref[...] = seg(1).reshape(t_len, n_seq, W_MIX)
    vn_ref[...] = _ln(seg(2), lng_ref[...], lnb_ref[...]).reshape(t_len, n_seq, W_MIX)

    for t in range(t_len):
        rows = slice(t * n_seq, (t + 1) * n_seq)
        for gi, w in enumerate(POOL_WINDOWS):
            lanes = slice(gi * GROUP_W, (gi + 1) * GROUP_W)
            tok = p_ext[POOL_HIST + t, :, lanes]
            wsum = tok
            for j in range(1, w):
                wsum = wsum + p_ext[POOL_HIST + t - j, :, lanes]
            cnt = float(min(start_pos + t + 1, w))
            d_ref[rows, lanes] = (wsum / cnt - tok).astype(BF16)
            mixed = jnp.zeros((n_seq, GROUP_W), F32) + sb_ref_smem[gi * t_len + t]
            for s in range(t + 1):
                mixed = mixed + sw_ref[(gi * t_len + t) * t_len + s] * vn_ref[s, :, lanes]
            cat_ref[rows, W_MIX + gi * GROUP_W:W_MIX + (gi + 1) * GROUP_W] = (
                u_ref[t, :, lanes] * mixed).astype(BF16)

    for gi in range(N_GROUPS):
        lanes = slice(gi * GROUP_W, (gi + 1) * GROUP_W)
        yc = _dot(d_ref[:, lanes], wpool_ref[gi]) * pscale_ref[:, lanes]
        cat_ref[:, lanes] = yc.astype(BF16)

    o_ref[...] = (x + _dot(cat_ref[...], wout_ref[...])).reshape(t_len, n_seq, D_MODEL)
    sp_ref[...] = p_ext[t_len:t_len + POOL_HIST]
    for t in range(t_len):
        vn_out_ref[:, t, :] = vn_ref[t]


def _odd_sample(x, g, win, wpool, pscale, sguw, sgub, lng, lnb, wout, hist_p, nb, start_pos):
    n_blocks, t_len, _, _ = x.shape
    n_seq = n_blocks * nb
    m = t_len * nb
    n_in = win.shape[1]
    assert t_len <= CHUNK
    sw = sguw[:, :t_len, :t_len].reshape(-1)
    sb = sgub[:, :t_len].reshape(-1)

    grid_spec = pltpu.PrefetchScalarGridSpec(
        num_scalar_prefetch=2,
        grid=(n_seq // nb,),
        in_specs=[_xs_block_spec(t_len, nb), _const_spec((1, D_MODEL)),
                  _const_spec((D_MODEL, n_in)),
                  _const_spec((N_GROUPS, GROUP_W, GROUP_W)), _const_spec((1, W_MIX)),
                  _const_spec((1, W_MIX)), _const_spec((1, W_MIX)),
                  _const_spec((2 * W_MIX, D_MODEL)),
                  _seq_block_spec(POOL_HIST, nb, W_MIX)],
        out_specs=[_xs_block_spec(t_len, nb), _seq_block_spec(POOL_HIST, nb, W_MIX),
                   _seq_major_spec(t_len, nb, W_MIX)],
        scratch_shapes=[pltpu.VMEM((m, D_MODEL), BF16),
                        pltpu.VMEM((t_len + POOL_HIST, nb, W_MIX), F32),
                        pltpu.VMEM((t_len, nb, W_MIX), F32),
                        pltpu.VMEM((t_len, nb, W_MIX), F32),
                        pltpu.VMEM((m, W_MIX), BF16),
                        pltpu.VMEM((m, 2 * W_MIX), BF16)],
    )
    return pl.pallas_call(
        functools.partial(_odd_sample_kernel, t_len=t_len, n_seq=nb, start_pos=start_pos),
        grid_spec=grid_spec,
        out_shape=[jax.ShapeDtypeStruct((n_blocks, t_len, nb, D_MODEL), F32),
                   jax.ShapeDtypeStruct((POOL_HIST, n_seq, W_MIX), F32),
                   jax.ShapeDtypeStruct((n_seq, t_len, W_MIX), F32)],
        compiler_params=_params(1),
        name="odd_sample",
    )(sw, sb, x, g.reshape(1, D_MODEL), win, wpool, pscale.reshape(1, W_MIX),
      lng.reshape(1, W_MIX), lnb.reshape(1, W_MIX), wout, hist_p)


def _time_major(a):
    return jnp.swapaxes(a, 0, 1)


def kernel(x_prompt, x_sample, state_conv_a, state_conv_b, state_pool, norm_mix, norm_ffn, ev_w_in,
           ev_conv_a, ev_conv_b, ev_conv_b_bias, ev_ln_g, ev_ln_b, ev_w_out, od_w_in, od_pool_w,
           od_pool_scale, od_sgu_w, od_sgu_b, od_sgu_ln_g, od_sgu_ln_b, od_w_out, ffn_w1, ffn_w2,
           norm_final):
    depth = norm_mix.shape[0]
    batch, seq, _ = x_prompt.shape
    n_seq, t_len, _ = x_sample.shape
    tm_mix = 1024
    tm_ffn = 512
    nb = tm_ffn // t_len

    def mixer_weights(l):
        i = l // 2
        return ((ev_w_in, i), (ev_w_out, i)) if l % 2 == 0 else ((od_w_in, i), (od_w_out, i))

    xp = x_prompt
    xs = x_sample
    sa_p, sa_s, sb_p, sb_s, sc_p, sc_s, vn_s = [], [], [], [], [], [], []
    win, wout = (w[i].astype(BF16) for w, i in mixer_weights(0))
    for l in range(depth):
        i = l // 2
        ffn_f32 = ((ffn_w1, l), (ffn_w2, l))
        if l % 2 == 0:
            common = (norm_mix[l], win, ev_conv_a[i], ev_conv_b[i], ev_conv_b_bias[i], ev_ln_g[i],
                      ev_ln_b[i], wout)
            xp, a_p, b_p, w1, w2 = _even_prompt(xp, *common, tm=tm_mix, cast_next=ffn_f32)
            xs, a_s, b_s = _even_sample(xs, *common, state_conv_a[i], _time_major(state_conv_b[i]),
                                        nb, x_seq_major=(l == 0))
            sa_p.append(a_p)
            sb_p.append(b_p)
            sa_s.append(a_s)
            sb_s.append(_time_major(b_s))
        else:
            common = (norm_mix[l], win, od_pool_w[i].astype(BF16), od_pool_scale[i], od_sgu_w[i],
                      od_sgu_b[i], od_sgu_ln_g[i], od_sgu_ln_b[i], wout)
            xp, c_p, w1, w2 = _odd_prompt(xp, *common, tm=tm_mix, cast_next=ffn_f32)
            xs, c_s, v_s = _odd_sample(xs, *common, _time_major(state_pool[i]), nb, PAST_LEN)
            sc_p.append(c_p)
            sc_s.append(_time_major(c_s))
            vn_s.append(v_s)
        last = l == depth - 1
        xp, xs, *next_w = _ffn(xp.reshape(batch * seq, D_MODEL), xs.reshape(t_len * n_seq, D_MODEL),
                               norm_ffn[l], w1, w2, norm_final if last else None, tm_ffn, t_len,
                               cast_next=() if last else mixer_weights(l + 1))
        xp = xp.reshape(batch, seq, D_MODEL)
        if not last:
            xs = xs.reshape(n_seq // nb, t_len, nb, D_MODEL)
            win, wout = next_w

    return (xp, xs, jnp.stack(sa_p), jnp.stack(sa_s), jnp.stack(sb_p), jnp.stack(sb_s),
            jnp.stack(sc_p), jnp.stack(sc_s), jnp.stack(vn_s))
```

```python
import functools

import jax
import jax.numpy as jnp
from jax import lax
from jax.experimental import pallas as pl
from jax.experimental.pallas import tpu as pltpu

F32 = jnp.float32
BF16 = jnp.bfloat16

D_MODEL = 1024
W_MIX = D_MODEL // 2
K_A = 3
K_B = 31
POOL_WINDOWS = (2, 4, 8, 16)
POOL_HIST = max(POOL_WINDOWS) - 1
CHUNK = 128
N_GROUPS = 4
GROUP_W = W_MIX // N_GROUPS
D_FF = 4 * D_MODEL
PAST_LEN = 16384
EPS = 1e-6

SUBLANES = 8
HIST_A = 8
HIST_B = 32
HIST_P = SUBLANES * len(POOL_WINDOWS)
ROW_CHUNK = 64
CONV_ROWS = 64
MIX_BLOCK_ROWS = 256
SAMPLE_CONV_STEPS = 4
FFN_COL_CHUNK = 1024
VMEM_LIMIT = 56 * 1024 * 1024


def _rms(x, g):
    return x * lax.rsqrt(jnp.mean(x * x, axis=-1, keepdims=True) + EPS) * g


def _ln(x, g, b):
    mu = jnp.mean(x, axis=-1, keepdims=True)
    xc = x - mu
    return xc * lax.rsqrt(jnp.mean(xc * xc, axis=-1, keepdims=True) + EPS) * g + b


def _dot(a, b):
    return jnp.dot(a, b, preferred_element_type=F32)


def _silu(x):
    return x * jax.nn.sigmoid(x)


def _const_spec(shape):
    nd = len(shape)
    return pl.BlockSpec(shape, lambda *_: (0,) * nd, pipeline_mode=pl.Buffered(1))


def _emit_interleaved(mxu_pieces, vpu_pieces):
    n_m, n_v = len(mxu_pieces), len(vpu_pieces)
    i = j = 0
    while i < n_m or j < n_v:
        if j >= n_v or (i < n_m and i * n_v <= j * n_m):
            mxu_pieces[i]()
            i += 1
        else:
            vpu_pieces[j]()
            j += 1


def _params(n_grid_axes):
    return pltpu.CompilerParams(
        dimension_semantics=("arbitrary",) * n_grid_axes,
        vmem_limit_bytes=VMEM_LIMIT,
    )


def _cast_plumbing(arrays, n_steps, step_of):
    in_specs, out_specs, shapes = [], [], []
    for stacked, layer in arrays:
        _, n_rows, n_cols = stacked.shape
        rows, rem = divmod(n_rows, n_steps)
        assert rem == 0 and rows % (2 * SUBLANES) == 0, stacked.shape
        in_specs.append(pl.BlockSpec((None, rows, n_cols),
                                     lambda *idx, _l=layer: (_l, step_of(*idx), 0)))
        out_specs.append(pl.BlockSpec((rows, n_cols), lambda *idx: (step_of(*idx), 0)))
        shapes.append(jax.ShapeDtypeStruct((n_rows, n_cols), BF16))
    return in_specs, out_specs, shapes


def _cast_rows(src_refs, dst_refs):
    for src, dst in zip(src_refs, dst_refs, strict=True):
        dst[...] = src[...].astype(BF16)


def _ffn_kernel(*refs, final, n_prompt, n_cast, t_len):
    n_in = 6 if final else 5
    xp_ref, xs_ref, g_ref, w1_ref, w2_ref = refs[:5]
    cast_in = refs[n_in:n_in + n_cast]
    op_ref, os_ref = refs[n_in + n_cast:n_in + n_cast + 2]
    cast_out = refs[n_in + n_cast + 2:n_in + 2 * n_cast + 2]
    hb_ref, act_ref = refs[n_in + 2 * n_cast + 2:]
    step = pl.program_id(0)

    def rows_block(x_ref):
        hb_ref[...] = _rms(x_ref[...], g_ref[...]).astype(BF16)
        for c in range(D_FF // FFN_COL_CHUNK):
            cols = slice(c * FFN_COL_CHUNK, (c + 1) * FFN_COL_CHUNK)
            a = _dot(hb_ref[...], w1_ref[:, cols])
            act_ref[:, cols] = jnp.square(jnp.maximum(a, 0.0)).astype(BF16)
        y = x_ref[...] + _dot(act_ref[...], w2_ref[...])
        return _rms(y, refs[5][...]) if final else y

    @pl.when(step < n_prompt)
    def _():
        op_ref[...] = rows_block(xp_ref)
        _cast_rows(cast_in, cast_out)

    @pl.when(step >= n_prompt)
    def _():
        y = rows_block(xs_ref)
        if final:
            nb = y.shape[0] // t_len
            for t in range(t_len):
                os_ref[:, t, :] = y[t * nb:(t + 1) * nb]
        else:
            os_ref[...] = y


def _ffn(xp2d, xs2d, g, w1, w2, g_final, tm, t_len, cast_next=()):
    n_prompt, n_sample = xp2d.shape[0] // tm, xs2d.shape[0] // tm
    final = g_final is not None
    last = n_prompt - 1
    p_spec = pl.BlockSpec((tm, D_MODEL), lambda i: (jnp.minimum(i, last), 0))
    s_spec = pl.BlockSpec((tm, D_MODEL), lambda i: (jnp.maximum(i - n_prompt, 0), 0))
    in_specs = [p_spec, s_spec, _const_spec((1, D_MODEL)), _const_spec((D_MODEL, D_FF)),
                _const_spec((D_FF, D_MODEL))]
    args = [xp2d, xs2d, g.reshape(1, D_MODEL), w1, w2]
    s_out_spec, s_out_shape = s_spec, jax.ShapeDtypeStruct(xs2d.shape, F32)
    if final:
        in_specs.append(_const_spec((1, D_MODEL)))
        args.append(g_final.reshape(1, D_MODEL))
        nb = tm // t_len
        s_out_spec = pl.BlockSpec((nb, t_len, D_MODEL), lambda i: (jnp.maximum(i - n_prompt, 0), 0, 0))
        s_out_shape = jax.ShapeDtypeStruct((xs2d.shape[0] // t_len, t_len, D_MODEL), F32)
    c_in, c_out, c_shapes = _cast_plumbing(cast_next, n_prompt, lambda i: jnp.minimum(i, last))
    return pl.pallas_call(
        functools.partial(_ffn_kernel, final=final, n_prompt=n_prompt, n_cast=len(cast_next),
                          t_len=t_len),
        grid=(n_prompt + n_sample,),
        in_specs=in_specs + c_in,
        out_specs=[p_spec, s_out_spec] + c_out,
        out_shape=[jax.ShapeDtypeStruct(xp2d.shape, F32), s_out_shape] + c_shapes,
        scratch_shapes=[pltpu.VMEM((tm, D_MODEL), BF16), pltpu.VMEM((tm, D_FF), BF16)],
        compiler_params=_params(1),
        name="ffn_final" if final else "ffn",
    )(*args, *(a for a, _ in cast_next))


def _causal_dwconv(ext_ref, gi, w_ref, n_taps, hist, r0, rows):
    lanes = slice(gi * GROUP_W, (gi + 1) * GROUP_W)
    acc = None
    for k in range(n_taps):
        lo = hist + r0 - (n_taps - 1) + k
        term = w_ref[k:k + 1, lanes] * ext_ref[gi, lo:lo + rows, :]
        acc = term if acc is None else acc + term
    return acc


def _even_prompt_kernel(*refs, tm, n_cast):
    x_ref, g_ref, win_ref, ca_ref, cb_ref, cbb_ref, lng_ref, lnb_ref, wout_ref = refs[:9]
    cast_in = refs[9:9 + n_cast]
    o_ref, sa_ref, sb_ref = refs[9 + n_cast:12 + n_cast]
    cast_out = refs[12 + n_cast:12 + 2 * n_cast]
    hb_ref, z_ref, ua_ext, ub_ext, cbv_ref, cat_ref = refs[12 + 2 * n_cast:]
    s = pl.program_id(1)
    _cast_rows(cast_in, cast_out)

    @pl.when(s == 0)
    def _():
        ua_ext[:, 0:HIST_A, :] = jnp.zeros((N_GROUPS, HIST_A, GROUP_W), F32)
        ub_ext[:, 0:HIST_B, :] = jnp.zeros((N_GROUPS, HIST_B, GROUP_W), F32)

    @pl.when(s > 0)
    def _():
        ua_ext[:, 0:HIST_A, :] = ua_ext[:, tm:tm + HIST_A, :]
        ub_ext[:, 0:HIST_B, :] = ub_ext[:, tm:tm + HIST_B, :]

    def zseg(rows, i, lanes=slice(0, W_MIX)):
        return z_ref[rows, i * W_MIX + lanes.start:i * W_MIX + lanes.stop]

    def project_pieces(rows):
        def norm():
            hb_ref[rows, :] = _rms(x_ref[rows, :], g_ref[...]).astype(BF16)

        def dot_piece(i):
            cols = slice(i * W_MIX, (i + 1) * W_MIX)
            z_ref[rows, cols] = _dot(hb_ref[rows, :], win_ref[:, cols])

        return [norm] + [functools.partial(dot_piece, i) for i in (3, 4, 1, 2, 0)]

    def mix_pieces(rows):
        def gates(r0):
            crow = slice(r0, r0 + ROW_CHUNK)
            for gi in range(N_GROUPS):
                lanes = slice(gi * GROUP_W, (gi + 1) * GROUP_W)
                ua_ext[gi, HIST_A + r0:HIST_A + r0 + ROW_CHUNK, :] = (
                    zseg(crow, 1, lanes) * zseg(crow, 2, lanes))
                ub_ext[gi, HIST_B + r0:HIST_B + r0 + ROW_CHUNK, :] = (
                    zseg(crow, 3, lanes) * jax.nn.sigmoid(zseg(crow, 4, lanes)))

        def convs(r0, gi):
            crow = slice(r0, r0 + CONV_ROWS)
            lanes = slice(gi * GROUP_W, (gi + 1) * GROUP_W)
            ya = zseg(crow, 0, lanes) * _causal_dwconv(ua_ext, gi, ca_ref, K_A, HIST_A, r0, CONV_ROWS)
            cat_ref[crow, lanes] = ya.astype(BF16)
            cbv_ref[crow, lanes] = _causal_dwconv(ub_ext, gi, cb_ref, K_B, HIST_B, r0, CONV_ROWS)

        def norm_act(r0):
            crow = slice(r0, r0 + CONV_ROWS)
            yb = _silu(_ln(cbv_ref[crow, :] + cbb_ref[...], lng_ref[...], lnb_ref[...]))
            cat_ref[crow, W_MIX:2 * W_MIX] = yb.astype(BF16)

        pieces = [functools.partial(gates, r0) for r0 in range(rows.start, rows.stop, ROW_CHUNK)]
        for r0 in range(rows.start, rows.stop, CONV_ROWS):
            pieces += [functools.partial(convs, r0, gi) for gi in range(N_GROUPS)]
            pieces.append(functools.partial(norm_act, r0))
        return pieces

    def out_pieces(rows):
        def piece(j):
            cols = slice(j * W_MIX, (j + 1) * W_MIX)
            o_ref[rows, cols] = x_ref[rows, cols] + _dot(cat_ref[rows, :], wout_ref[:, cols])

        return [functools.partial(piece, j) for j in range(D_MODEL // W_MIX)]

    blocks = [slice(r, r + MIX_BLOCK_ROWS) for r in range(0, tm, MIX_BLOCK_ROWS)]
    _emit_interleaved(project_pieces(blocks[0]), [])
    for i in range(len(blocks)):
        mxu = out_pieces(blocks[i - 1]) if i > 0 else []
        if i + 1 < len(blocks):
            mxu = project_pieces(blocks[i + 1]) + mxu
        _emit_interleaved(mxu, mix_pieces(blocks[i]))
    _emit_interleaved(out_pieces(blocks[-1]), [])

    @pl.when(s == pl.num_programs(1) - 1)
    def _():
        for gi in range(N_GROUPS):
            lanes = slice(gi * GROUP_W, (gi + 1) * GROUP_W)
            sa_ref[:, lanes] = ua_ext[gi, HIST_A + tm - (K_A - 1):HIST_A + tm, :]
            sb_ref[:, lanes] = ub_ext[gi, HIST_B + tm - (K_B - 1):HIST_B + tm, :]


def _even_prompt(x, g, win, ca, cb, cbb, lng, lnb, wout, tm, cast_next=()):
    b, s, _ = x.shape
    n_s = s // tm
    tile = pl.BlockSpec((None, tm, D_MODEL), lambda i, j: (i, j, 0))
    n_in = win.shape[1]
    c_in, c_out, c_shapes = _cast_plumbing(cast_next, b * n_s, lambda i, j: i * n_s + j)
    return pl.pallas_call(
        functools.partial(_even_prompt_kernel, tm=tm, n_cast=len(cast_next)),
        grid=(b, n_s),
        in_specs=[tile, _const_spec((1, D_MODEL)), _const_spec((D_MODEL, n_in)),
                  _const_spec((K_A, W_MIX)), _const_spec((K_B, W_MIX)), _const_spec((1, W_MIX)),
                  _const_spec((1, W_MIX)), _const_spec((1, W_MIX)),
                  _const_spec((2 * W_MIX, D_MODEL))] + c_in,
        out_specs=[tile,
                   pl.BlockSpec((None, K_A - 1, W_MIX), lambda i, j: (i, 0, 0)),
                   pl.BlockSpec((None, K_B - 1, W_MIX), lambda i, j: (i, 0, 0))] + c_out,
        out_shape=[jax.ShapeDtypeStruct((b, s, D_MODEL), F32),
                   jax.ShapeDtypeStruct((b, K_A - 1, W_MIX), F32),
                   jax.ShapeDtypeStruct((b, K_B - 1, W_MIX), F32)] + c_shapes,
        scratch_shapes=[pltpu.VMEM((tm, D_MODEL), BF16),
                        pltpu.VMEM((tm, n_in), F32),
                        pltpu.VMEM((N_GROUPS, tm + HIST_A, GROUP_W), F32),
                        pltpu.VMEM((N_GROUPS, tm + HIST_B, GROUP_W), F32),
                        pltpu.VMEM((tm, W_MIX), F32),
                        pltpu.VMEM((tm, 2 * W_MIX), BF16)],
        compiler_params=_params(2),
        name="even_prompt",
    )(x, g.reshape(1, D_MODEL), win, ca, cb, cbb.reshape(1, W_MIX), lng.reshape(1, W_MIX),
      lnb.reshape(1, W_MIX), wout, *(a for a, _ in cast_next))


def _tril_bf16(w):
    t = lax.broadcasted_iota(jnp.int32, (CHUNK, CHUNK), 0)
    s = lax.broadcasted_iota(jnp.int32, (CHUNK, CHUNK), 1)
    return jnp.where(s <= t, w, 0.0).astype(BF16)


def _odd_prompt_kernel(*refs, tm, n_cast):
    (x_ref, g_ref, win_ref, wpool_ref, pscale_ref, sguw_ref, sgubt_ref, lng_ref, lnb_ref,
     wout_ref) = refs[:10]
    cast_in = refs[10:10 + n_cast]
    o_ref, sp_ref = refs[10 + n_cast:12 + n_cast]
    cast_out = refs[12 + n_cast:12 + 2 * n_cast]
    hb_ref, p_ext, ps_ref, inv_ref, u_ref, vn_ref, d_ref, cat_ref, wm_ref = refs[12 + 2 * n_cast:]
    s = pl.program_id(1)
    _cast_rows(cast_in, cast_out)

    @pl.when(jnp.logical_and(pl.program_id(0) == 0, s == 0))
    def _():
        for gi in range(N_GROUPS):
            wm_ref[gi] = _tril_bf16(sguw_ref[gi])

    @pl.when(s == 0)
    def _():
        p_ext[:, 0:HIST_P, :] = jnp.zeros((N_GROUPS, HIST_P, GROUP_W), F32)

    @pl.when(s > 0)
    def _():
        p_ext[:, 0:HIST_P, :] = p_ext[:, tm:tm + HIST_P, :]

    assert ROW_CHUNK >= max(POOL_WINDOWS)
    pos0 = s * tm + lax.broadcasted_iota(jnp.int32, (ROW_CHUNK, GROUP_W), 0)
    for gi, w in enumerate(POOL_WINDOWS):
        inv_ref[gi] = 1.0 / jnp.minimum(pos0 + 1, w).astype(F32)

    def project_pieces(rows):
        def seg(i):
            return _dot(hb_ref[rows, :], win_ref[:, i * W_MIX:(i + 1) * W_MIX])

        def norm():
            hb_ref[rows, :] = _rms(x_ref[rows, :], g_ref[...]).astype(BF16)

        def pooled():
            pc = seg(0)
            for gi in range(N_GROUPS):
                p_ext[gi, HIST_P + rows.start:HIST_P + rows.stop, :] = pc[:, gi * GROUP_W:(gi + 1) * GROUP_W]

        def gate():
            u_ref[rows, :] = seg(1)

        def normed():
            vn_ref[rows, :] = _ln(seg(2), lng_ref[...], lnb_ref[...]).astype(BF16)

        return [norm, pooled, gate, normed]

    def pool_chunk(r0):
        n = ROW_CHUNK + HIST_P
        for gi, w in enumerate(POOL_WINDOWS):
            win = p_ext.at[gi, r0:r0 + n, :]
            scratch = ps_ref.at[r0 // ROW_CHUNK % 2, gi]
            prev, lo, step = win, 0, 1
            sums = win[0:n, :]
            while step < w:
                lo += SUBLANES
                sums = sums[SUBLANES:] + prev[lo - step:n - step, :]
                step *= 2
                if step < w:
                    scratch[lo:n, :] = sums
                    prev = scratch
            tok = win[HIST_P:n, :]
            inv = inv_ref[gi] if r0 == 0 else 1.0 / w
            lanes = slice(gi * GROUP_W, (gi + 1) * GROUP_W)
            d_ref[r0:r0 + ROW_CHUNK, lanes] = (sums[HIST_P - lo:] * inv - tok).astype(BF16)

    def pool_project(rows):
        for gi in range(N_GROUPS):
            lanes = slice(gi * GROUP_W, (gi + 1) * GROUP_W)
            yc = _dot(d_ref[rows, lanes], wpool_ref[gi]) * pscale_ref[:, lanes]
            cat_ref[rows, lanes] = yc.astype(BF16)

    def sgu_chunk(r0):
        rows = slice(r0, r0 + CHUNK)
        for gi in range(N_GROUPS):
            lanes = slice(gi * GROUP_W, (gi + 1) * GROUP_W)
            mixed = _dot(wm_ref[gi], vn_ref[rows, lanes]) + sgubt_ref[:, gi:gi + 1]
            yd = u_ref[rows, lanes] * mixed
            cat_ref[rows, W_MIX + gi * GROUP_W:W_MIX + (gi + 1) * GROUP_W] = yd.astype(BF16)

    def mix_pieces(rows):
        pieces = [functools.partial(pool_chunk, r0) for r0 in range(rows.start, rows.stop, ROW_CHUNK)]
        pieces.append(functools.partial(pool_project, rows))
        pieces += [functools.partial(sgu_chunk, r0) for r0 in range(rows.start, rows.stop, CHUNK)]
        return pieces

    def out_pieces(rows):
        def piece():
            o_ref[rows, :] = x_ref[rows, :] + _dot(cat_ref[rows, :], wout_ref[...])

        return [piece]

    blocks = [slice(r, r + MIX_BLOCK_ROWS) for r in range(0, tm, MIX_BLOCK_ROWS)]
    _emit_interleaved(project_pieces(blocks[0]), [])
    for i in range(len(blocks)):
        mxu = out_pieces(blocks[i - 1]) if i > 0 else []
        if i + 1 < len(blocks):
            mxu = project_pieces(blocks[i + 1]) + mxu
        _emit_interleaved(mxu, mix_pieces(blocks[i]))
    _emit_interleaved(out_pieces(blocks[-1]), [])

    @pl.when(s == pl.num_programs(1) - 1)
    def _():
        for gi in range(N_GROUPS):
            sp_ref[:, gi * GROUP_W:(gi + 1) * GROUP_W] = p_ext[gi, HIST_P + tm - POOL_HIST:HIST_P + tm, :]


def _odd_prompt(x, g, win, wpool, pscale, sguw, sgub, lng, lnb, wout, tm, cast_next=()):
    b, s, _ = x.shape
    n_s = s // tm
    tile = pl.BlockSpec((None, tm, D_MODEL), lambda i, j: (i, j, 0))
    n_in = win.shape[1]
    c_in, c_out, c_shapes = _cast_plumbing(cast_next, b * n_s, lambda i, j: i * n_s + j)
    return pl.pallas_call(
        functools.partial(_odd_prompt_kernel, tm=tm, n_cast=len(cast_next)),
        grid=(b, n_s),
        in_specs=[tile, _const_spec((1, D_MODEL)), _const_spec((D_MODEL, n_in)),
                  _const_spec((N_GROUPS, GROUP_W, GROUP_W)), _const_spec((1, W_MIX)),
                  _const_spec((N_GROUPS, CHUNK, CHUNK)), _const_spec((CHUNK, N_GROUPS)),
                  _const_spec((1, W_MIX)), _const_spec((1, W_MIX)),
                  _const_spec((2 * W_MIX, D_MODEL))] + c_in,
        out_specs=[tile, pl.BlockSpec((None, POOL_HIST, W_MIX), lambda i, j: (i, 0, 0))] + c_out,
        out_shape=[jax.ShapeDtypeStruct((b, s, D_MODEL), F32),
                   jax.ShapeDtypeStruct((b, POOL_HIST, W_MIX), F32)] + c_shapes,
        scratch_shapes=[pltpu.VMEM((tm, D_MODEL), BF16),
                        pltpu.VMEM((N_GROUPS, tm + HIST_P, GROUP_W), F32),
                        pltpu.VMEM((2, N_GROUPS, ROW_CHUNK + HIST_P, GROUP_W), F32),
                        pltpu.VMEM((len(POOL_WINDOWS), ROW_CHUNK, GROUP_W), F32),
                        pltpu.VMEM((tm, W_MIX), F32),
                        pltpu.VMEM((tm, W_MIX), BF16),
                        pltpu.VMEM((tm, W_MIX), BF16),
                        pltpu.VMEM((tm, 2 * W_MIX), BF16),
                        pltpu.VMEM((N_GROUPS, CHUNK, CHUNK), BF16)],
        compiler_params=_params(2),
        name="odd_prompt",
    )(x, g.reshape(1, D_MODEL), win, wpool, pscale.reshape(1, W_MIX), sguw, sgub.T,
      lng.reshape(1, W_MIX), lnb.reshape(1, W_MIX), wout, *(a for a, _ in cast_next))


def _even_sample_kernel(x_ref, g_ref, win_ref, ca_ref, cb_ref, cbb_ref, lng_ref, lnb_ref, wout_ref,
                        ha_ref, hbst_ref,
                        o_ref, sa_ref, sb_ref,
                        hb_ref, gate_ref, ua_ext, ub_ext, cbv_ref, cat_ref, *, t_len, n_seq,
                        x_seq_major):
    if x_seq_major:
        x = jnp.concatenate([x_ref[:, t, :] for t in range(t_len)], axis=0)
    else:
        x = x_ref[...].reshape(t_len * n_seq, D_MODEL)
    hb_ref[...] = _rms(x, g_ref[...]).astype(BF16)

    def seg(i):
        return _dot(hb_ref[...], win_ref[:, i * W_MIX:(i + 1) * W_MIX])

    gate_ref[...] = seg(0)
    for j in range(K_A - 1):
        ua_ext[j] = ha_ref[:, j, :]
    ub_ext[0:K_B - 1] = hbst_ref[...]
    ua_ext[K_A - 1:K_A - 1 + t_len] = (seg(1) * seg(2)).reshape(t_len, n_seq, W_MIX)
    ub_ext[K_B - 1:K_B - 1 + t_len] = (seg(3) * jax.nn.sigmoid(seg(4))).reshape(t_len, n_seq, W_MIX)

    def taps(ext_ref, w_ref, n_taps, t0, lanes):
        acc = None
        for k in range(n_taps):
            term = w_ref[k:k + 1, lanes] * ext_ref[t0 + k:t0 + k + SAMPLE_CONV_STEPS, :, lanes]
            acc = term if acc is None else acc + term
        return acc.reshape(SAMPLE_CONV_STEPS * n_seq, GROUP_W)

    for t0 in range(0, t_len, SAMPLE_CONV_STEPS):
        rows = slice(t0 * n_seq, (t0 + SAMPLE_CONV_STEPS) * n_seq)
        for gi in range(N_GROUPS):
            lanes = slice(gi * GROUP_W, (gi + 1) * GROUP_W)
            cat_ref[rows, lanes] = (gate_ref[rows, lanes] * taps(ua_ext, ca_ref, K_A, t0, lanes)).astype(BF16)
            cbv_ref[rows, lanes] = taps(ub_ext, cb_ref, K_B, t0, lanes) + cbb_ref[:, lanes]
        yb = _silu(_ln(cbv_ref[rows, :], lng_ref[...], lnb_ref[...]))
        cat_ref[rows, W_MIX:2 * W_MIX] = yb.astype(BF16)

    o_ref[...] = (x + _dot(cat_ref[...], wout_ref[...])).reshape(t_len, n_seq, D_MODEL)
    for j in range(K_A - 1):
        sa_ref[:, j, :] = ua_ext[t_len + j]
    sb_ref[...] = ub_ext[t_len:t_len + K_B - 1]


def _seq_block_spec(rows, nb, width):
    return pl.BlockSpec((rows, nb, width), lambda i, *_: (0, i, 0))


def _seq_major_spec(rows, nb, width):
    return pl.BlockSpec((nb, rows, width), lambda i, *_: (i, 0, 0))


def _xs_block_spec(t_len, nb):
    return pl.BlockSpec((None, t_len, nb, D_MODEL), lambda i, *_: (i, 0, 0, 0))


def _even_sample(x, g, win, ca, cb, cbb, lng, lnb, wout, hist_a, hist_b, nb, x_seq_major):
    n_seq, t_len = hist_a.shape[0], x.size // (hist_a.shape[0] * D_MODEL)
    m = t_len * nb
    n_in = win.shape[1]
    x_spec = _seq_major_spec(t_len, nb, D_MODEL) if x_seq_major else _xs_block_spec(t_len, nb)
    return pl.pallas_call(
        functools.partial(_even_sample_kernel, t_len=t_len, n_seq=nb, x_seq_major=x_seq_major),
        grid=(n_seq // nb,),
        in_specs=[x_spec, _const_spec((1, D_MODEL)),
                  _const_spec((D_MODEL, n_in)),
                  _const_spec((K_A, W_MIX)), _const_spec((K_B, W_MIX)), _const_spec((1, W_MIX)),
                  _const_spec((1, W_MIX)), _const_spec((1, W_MIX)), _const_spec((2 * W_MIX, D_MODEL)),
                  _seq_major_spec(K_A - 1, nb, W_MIX), _seq_block_spec(K_B - 1, nb, W_MIX)],
        out_specs=[_xs_block_spec(t_len, nb), _seq_major_spec(K_A - 1, nb, W_MIX),
                   _seq_block_spec(K_B - 1, nb, W_MIX)],
        out_shape=[jax.ShapeDtypeStruct((n_seq // nb, t_len, nb, D_MODEL), F32),
                   jax.ShapeDtypeStruct((n_seq, K_A - 1, W_MIX), F32),
                   jax.ShapeDtypeStruct((K_B - 1, n_seq, W_MIX), F32)],
        scratch_shapes=[pltpu.VMEM((m, D_MODEL), BF16),
                        pltpu.VMEM((m, W_MIX), F32),
                        pltpu.VMEM((t_len + K_A - 1, nb, W_MIX), F32),
                        pltpu.VMEM((t_len + K_B - 1, nb, W_MIX), F32),
                        pltpu.VMEM((m, W_MIX), F32),
                        pltpu.VMEM((m, 2 * W_MIX), BF16)],
        compiler_params=_params(1),
        name="even_sample",
    )(x, g.reshape(1, D_MODEL), win, ca, cb, cbb.reshape(1, W_MIX), lng.reshape(1, W_MIX),
      lnb.reshape(1, W_MIX), wout, hist_a, hist_b)


def _odd_sample_kernel(sw_ref, sb_ref_smem,
                       x_ref, g_ref, win_ref, wpool_ref, pscale_ref, lng_ref, lnb_ref, wout_ref,
                       hp_ref,
                       o_ref, sp_ref, vn_out_ref,
                       hb_ref, p_ext, u_ref, vn_ref, d_ref, cat_ref, *, t_len, n_seq, start_pos):
    x = x_ref[...].reshape(t_len * n_seq, D_MODEL)
    hb_ref[...] = _rms(x, g_ref[...]).astype(BF16)

    def seg(i):
        return _dot(hb_ref[...], win_ref[:, i * W_MIX:(i + 1) * W_MIX])

    p_ext[0:POOL_HIST] = hp_ref[...]
    p_ext[POOL_HIST:POOL_HIST + t_len] = seg(0).reshape(t_len, n_seq, W_MIX)
    u_ref[...] = seg(1).reshape(t_len, n_seq, W_MIX)
    vn_ref[...] = _ln(seg(2), lng_ref[...], lnb_ref[...]).reshape(t_len, n_seq, W_MIX)

    for t in range(t_len):
        rows = slice(t * n_seq, (t + 1) * n_seq)
        for gi, w in enumerate(POOL_WINDOWS):
            lanes = slice(gi * GROUP_W, (gi + 1) * GROUP_W)
            tok = p_ext[POOL_HIST + t, :, lanes]
            wsum = tok
            for j in range(1, w):
                wsum = wsum + p_ext[POOL_HIST + t - j, :, lanes]
            cnt = float(min(start_pos + t + 1, w))
            d_ref[rows, lanes] = (wsum / cnt - tok).astype(BF16)
            mixed = jnp.zeros((n_seq, GROUP_W), F32) + sb_ref_smem[gi * t_len + t]
            for s in range(t + 1):
                mixed = mixed + sw_ref[(gi * t_len + t) * t_len + s] * vn_ref[s, :, lanes]
            cat_ref[rows, W_MIX + gi * GROUP_W:W_MIX + (gi + 1) * GROUP_W] = (
                u_ref[t, :, lanes] * mixed).astype(BF16)

    for gi in range(N_GROUPS):
        lanes = slice(gi * GROUP_W, (gi + 1) * GROUP_W)
        yc = _dot(d_ref[:, lanes], wpool_ref[gi]) * pscale_ref[:, lanes]
        cat_ref[:, lanes] = yc.astype(BF16)

    o_ref[...] = (x + _dot(cat_ref[...], wout_ref[...])).reshape(t_len, n_seq, D_MODEL)
    sp_ref[...] = p_ext[t_len:t_len + POOL_HIST]
    for t in range(t_len):
        vn_out_ref[:, t, :] = vn_ref[t]


def _odd_sample(x, g, win, wpool, pscale, sguw, sgub, lng, lnb, wout, hist_p, nb, start_pos):
    n_blocks, t_len, _, _ = x.shape
    n_seq = n_blocks * nb
    m = t_len * nb
    n_in = win.shape[1]
    assert t_len <= CHUNK
    sw = sguw[:, :t_len, :t_len].reshape(-1)
    sb = sgub[:, :t_len].reshape(-1)

    grid_spec = pltpu.PrefetchScalarGridSpec(
        num_scalar_prefetch=2,
        grid=(n_seq // nb,),
        in_specs=[_xs_block_spec(t_len, nb), _const_spec((1, D_MODEL)),
                  _const_spec((D_MODEL, n_in)),
                  _const_spec((N_GROUPS, GROUP_W, GROUP_W)), _const_spec((1, W_MIX)),
                  _const_spec((1, W_MIX)), _const_spec((1, W_MIX)),
                  _const_spec((2 * W_MIX, D_MODEL)),
                  _seq_block_spec(POOL_HIST, nb, W_MIX)],
        out_specs=[_xs_block_spec(t_len, nb), _seq_block_spec(POOL_HIST, nb, W_MIX),
                   _seq_major_spec(t_len, nb, W_MIX)],
        scratch_shapes=[pltpu.VMEM((m, D_MODEL), BF16),
                        pltpu.VMEM((t_len + POOL_HIST, nb, W_MIX), F32),
                        pltpu.VMEM((t_len, nb, W_MIX), F32),
                        pltpu.VMEM((t_len, nb, W_MIX), F32),
                        pltpu.VMEM((m, W_MIX), BF16),
                        pltpu.VMEM((m, 2 * W_MIX), BF16)],
    )
    return pl.pallas_call(
        functools.partial(_odd_sample_kernel, t_len=t_len, n_seq=nb, start_pos=start_pos),
        grid_spec=grid_spec,
        out_shape=[jax.ShapeDtypeStruct((n_blocks, t_len, nb, D_MODEL), F32),
                   jax.ShapeDtypeStruct((POOL_HIST, n_seq, W_MIX), F32),
                   jax.ShapeDtypeStruct((n_seq, t_len, W_MIX), F32)],
        compiler_params=_params(1),
        name="odd_sample",
    )(sw, sb, x, g.reshape(1, D_MODEL), win, wpool, pscale.reshape(1, W_MIX),
      lng.reshape(1, W_MIX), lnb.reshape(1, W_MIX), wout, hist_p)


def _time_major(a):
    return jnp.swapaxes(a, 0, 1)


def kernel(x_prompt, x_sample, state_conv_a, state_conv_b, state_pool, norm_mix, norm_ffn, ev_w_in,
           ev_conv_a, ev_conv_b, ev_conv_b_bias, ev_ln_g, ev_ln_b, ev_w_out, od_w_in, od_pool_w,
           od_pool_scale, od_sgu_w, od_sgu_b, od_sgu_ln_g, od_sgu_ln_b, od_w_out, ffn_w1, ffn_w2,
           norm_final):
    depth = norm_mix.shape[0]
    batch, seq, _ = x_prompt.shape
    n_seq, t_len, _ = x_sample.shape
    tm_mix = 1024
    tm_ffn = 512
    nb = tm_ffn // t_len

    def mixer_weights(l):
        i = l // 2
        return ((ev_w_in, i), (ev_w_out, i)) if l % 2 == 0 else ((od_w_in, i), (od_w_out, i))

    xp = x_prompt
    xs = x_sample
    sa_p, sa_s, sb_p, sb_s, sc_p, sc_s, vn_s = [], [], [], [], [], [], []
    win, wout = (w[i].astype(BF16) for w, i in mixer_weights(0))
    for l in range(depth):
        i = l // 2
        ffn_f32 = ((ffn_w1, l), (ffn_w2, l))
        if l % 2 == 0:
            common = (norm_mix[l], win, ev_conv_a[i], ev_conv_b[i], ev_conv_b_bias[i], ev_ln_g[i],
                      ev_ln_b[i], wout)
            xp, a_p, b_p, w1, w2 = _even_prompt(xp, *common, tm=tm_mix, cast_next=ffn_f32)
            xs, a_s, b_s = _even_sample(xs, *common, state_conv_a[i], _time_major(state_conv_b[i]),
                                        nb, x_seq_major=(l == 0))
            sa_p.append(a_p)
            sb_p.append(b_p)
            sa_s.append(a_s)
            sb_s.append(_time_major(b_s))
        else:
            common = (norm_mix[l], win, od_pool_w[i].astype(BF16), od_pool_scale[i], od_sgu_w[i],
                      od_sgu_b[i], od_sgu_ln_g[i], od_sgu_ln_b[i], wout)
            xp, c_p, w1, w2 = _odd_prompt(xp, *common, tm=tm_mix, cast_next=ffn_f32)
            xs, c_s, v_s = _odd_sample(xs, *common, _time_major(state_pool[i]), nb, PAST_LEN)
            sc_p.append(c_p)
            sc_s.append(_time_major(c_s))
            vn_s.append(v_s)
        last = l == depth - 1
        xp, xs, *next_w = _ffn(xp.reshape(batch * seq, D_MODEL), xs.reshape(t_len * n_seq, D_MODEL),
                               norm_ffn[l], w1, w2, norm_final if last else None, tm_ffn, t_len,
                               cast_next=() if last else mixer_weights(l + 1))
        xp = xp.reshape(batch, seq, D_MODEL)
        if not last:
            xs = xs.reshape(n_seq // nb, t_len, nb, D_MODEL)
            win, wout = next_w

    return (xp, xs, jnp.stack(sa_p), jnp.stack(sa_s), jnp.stack(sb_p), jnp.stack(sb_s),
            jnp.stack(sc_p), jnp.stack(sc_s), jnp.stack(vn_s))
```

```python
import functools

import jax
import jax.numpy as jnp
from jax import lax
from jax.experimental import pallas as pl
from jax.experimental.pallas import tpu as pltpu

F32 = jnp.float32
BF16 = jnp.bfloat16

D_MODEL = 1024
W_MIX = D_MODEL // 2
K_A = 3
K_B = 31
POOL_WINDOWS = (2, 4, 8, 16)
POOL_HIST = max(POOL_WINDOWS) - 1
CHUNK = 128
N_GROUPS = 4
GROUP_W = W_MIX // N_GROUPS
D_FF = 4 * D_MODEL
PAST_LEN = 16384
EPS = 1e-6

SUBLANES = 8
HIST_A = 8
HIST_B = 32
HIST_P = SUBLANES * len(POOL_WINDOWS)
ROW_CHUNK = 64
CONV_ROWS = 64
MIX_BLOCK_ROWS = 256
SAMPLE_CONV_STEPS = 4
FFN_COL_CHUNK = 1024
VMEM_LIMIT = 56 * 1024 * 1024


def _rms(x, g):
    return x * lax.rsqrt(jnp.mean(x * x, axis=-1, keepdims=True) + EPS) * g


def _ln(x, g, b):
    mu = jnp.mean(x, axis=-1, keepdims=True)
    xc = x - mu
    return xc * lax.rsqrt(jnp.mean(xc * xc, axis=-1, keepdims=True) + EPS) * g + b


def _dot(a, b):
    return jnp.dot(a, b, preferred_element_type=F32)


def _silu(x):
    return x * jax.nn.sigmoid(x)


def _const_spec(shape):
    nd = len(shape)
    return pl.BlockSpec(shape, lambda *_: (0,) * nd, pipeline_mode=pl.Buffered(1))


def _emit_interleaved(mxu_pieces, vpu_pieces):
    n_m, n_v = len(mxu_pieces), len(vpu_pieces)
    i = j = 0
    while i < n_m or j < n_v:
        if j >= n_v or (i < n_m and i * n_v <= j * n_m):
            mxu_pieces[i]()
            i += 1
        else:
            vpu_pieces[j]()
            j += 1


def _params(n_grid_axes):
    return pltpu.CompilerParams(
        dimension_semantics=("arbitrary",) * n_grid_axes,
        vmem_limit_bytes=VMEM_LIMIT,
    )


def _cast_plumbing(arrays, n_steps, step_of):
    in_specs, out_specs, shapes = [], [], []
    for stacked, layer in arrays:
        _, n_rows, n_cols = stacked.shape
        rows, rem = divmod(n_rows, n_steps)
        assert rem == 0 and rows % (2 * SUBLANES) == 0, stacked.shape
        in_specs.append(pl.BlockSpec((None, rows, n_cols),
                                     lambda *idx, _l=layer: (_l, step_of(*idx), 0)))
        out_specs.append(pl.BlockSpec((rows, n_cols), lambda *idx: (step_of(*idx), 0)))
        shapes.append(jax.ShapeDtypeStruct((n_rows, n_cols), BF16))
    return in_specs, out_specs, shapes


def _cast_rows(src_refs, dst_refs):
    for src, dst in zip(src_refs, dst_refs, strict=True):
        dst[...] = src[...].astype(BF16)


def _ffn_kernel(*refs, final, n_prompt, n_cast, t_len):
    n_in = 6 if final else 5
    xp_ref, xs_ref, g_ref, w1_ref, w2_ref = refs[:5]
    cast_in = refs[n_in:n_in + n_cast]
    op_ref, os_ref = refs[n_in + n_cast:n_in + n_cast + 2]
    cast_out = refs[n_in + n_cast + 2:n_in + 2 * n_cast + 2]
    hb_ref, act_ref = refs[n_in + 2 * n_cast + 2:]
    step = pl.program_id(0)

    def rows_block(x_ref):
        hb_ref[...] = _rms(x_ref[...], g_ref[...]).astype(BF16)
        for c in range(D_FF // FFN_COL_CHUNK):
            cols = slice(c * FFN_COL_CHUNK, (c + 1) * FFN_COL_CHUNK)
            a = _dot(hb_ref[...], w1_ref[:, cols])
            act_ref[:, cols] = jnp.square(jnp.maximum(a, 0.0)).astype(BF16)
        y = x_ref[...] + _dot(act_ref[...], w2_ref[...])
        return _rms(y, refs[5][...]) if final else y

    @pl.when(step < n_prompt)
    def _():
        op_ref[...] = rows_block(xp_ref)
        _cast_rows(cast_in, cast_out)

    @pl.when(step >= n_prompt)
    def _():
        y = rows_block(xs_ref)
        if final:
            nb = y.shape[0] // t_len
            for t in range(t_len):
                os_ref[:, t, :] = y[t * nb:(t + 1) * nb]
        else:
            os_ref[...] = y


def _ffn(xp2d, xs2d, g, w1, w2, g_final, tm, t_len, cast_next=()):
    n_prompt, n_sample = xp2d.shape[0] // tm, xs2d.shape[0] // tm
    final = g_final is not None
    last = n_prompt - 1
    p_spec = pl.BlockSpec((tm, D_MODEL), lambda i: (jnp.minimum(i, last), 0))
    s_spec = pl.BlockSpec((tm, D_MODEL), lambda i: (jnp.maximum(i - n_prompt, 0), 0))
    in_specs = [p_spec, s_spec, _const_spec((1, D_MODEL)), _const_spec((D_MODEL, D_FF)),
                _const_spec((D_FF, D_MODEL))]
    args = [xp2d, xs2d, g.reshape(1, D_MODEL), w1, w2]
    s_out_spec, s_out_shape = s_spec, jax.ShapeDtypeStruct(xs2d.shape, F32)
    if final:
        in_specs.append(_const_spec((1, D_MODEL)))
        args.append(g_final.reshape(1, D_MODEL))
        nb = tm // t_len
        s_out_spec = pl.BlockSpec((nb, t_len, D_MODEL), lambda i: (jnp.maximum(i - n_prompt, 0), 0, 0))
        s_out_shape = jax.ShapeDtypeStruct((xs2d.shape[0] // t_len, t_len, D_MODEL), F32)
    c_in, c_out, c_shapes = _cast_plumbing(cast_next, n_prompt, lambda i: jnp.minimum(i, last))
    return pl.pallas_call(
        functools.partial(_ffn_kernel, final=final, n_prompt=n_prompt, n_cast=len(cast_next),
                          t_len=t_len),
        grid=(n_prompt + n_sample,),
        in_specs=in_specs + c_in,
        out_specs=[p_spec, s_out_spec] + c_out,
        out_shape=[jax.ShapeDtypeStruct(xp2d.shape, F32), s_out_shape] + c_shapes,
        scratch_shapes=[pltpu.VMEM((tm, D_MODEL), BF16), pltpu.VMEM((tm, D_FF), BF16)],
        compiler_params=_params(1),
        name="ffn_final" if final else "ffn",
    )(*args, *(a for a, _ in cast_next))


def _causal_dwconv(ext_ref, gi, w_ref, n_taps, hist, r0, rows):
    lanes = slice(gi * GROUP_W, (gi + 1) * GROUP_W)
    acc = None
    for k in range(n_taps):
        lo = hist + r0 - (n_taps - 1) + k
        term = w_ref[k:k + 1, lanes] * ext_ref[gi, lo:lo + rows, :]
        acc = term if acc is None else acc + term
    return acc


def _even_prompt_kernel(*refs, tm, n_cast):
    x_ref, g_ref, win_ref, ca_ref, cb_ref, cbb_ref, lng_ref, lnb_ref, wout_ref = refs[:9]
    cast_in = refs[9:9 + n_cast]
    o_ref, sa_ref, sb_ref = refs[9 + n_cast:12 + n_cast]
    cast_out = refs[12 + n_cast:12 + 2 * n_cast]
    hb_ref, z_ref, ua_ext, ub_ext, cbv_ref, cat_ref = refs[12 + 2 * n_cast:]
    s = pl.program_id(1)
    _cast_rows(cast_in, cast_out)

    @pl.when(s == 0)
    def _():
        ua_ext[:, 0:HIST_A, :] = jnp.zeros((N_GROUPS, HIST_A, GROUP_W), F32)
        ub_ext[:, 0:HIST_B, :] = jnp.zeros((N_GROUPS, HIST_B, GROUP_W), F32)

    @pl.when(s > 0)
    def _():
        ua_ext[:, 0:HIST_A, :] = ua_ext[:, tm:tm + HIST_A, :]
        ub_ext[:, 0:HIST_B, :] = ub_ext[:, tm:tm + HIST_B, :]

    def zseg(rows, i, lanes=slice(0, W_MIX)):
        return z_ref[rows, i * W_MIX + lanes.start:i * W_MIX + lanes.stop]

    def project_pieces(rows):
        def norm():
            hb_ref[rows, :] = _rms(x_ref[rows, :], g_ref[...]).astype(BF16)

        def dot_piece(i):
            cols = slice(i * W_MIX, (i + 1) * W_MIX)
            z_ref[rows, cols] = _dot(hb_ref[rows, :], win_ref[:, cols])

        return [norm] + [functools.partial(dot_piece, i) for i in (3, 4, 1, 2, 0)]

    def mix_pieces(rows):
        def gates(r0):
            crow = slice(r0, r0 + ROW_CHUNK)
            for gi in range(N_GROUPS):
                lanes = slice(gi * GROUP_W, (gi + 1) * GROUP_W)
                ua_ext[gi, HIST_A + r0:HIST_A + r0 + ROW_CHUNK, :] = (
                    zseg(crow, 1, lanes) * zseg(crow, 2, lanes))
                ub_ext[gi, HIST_B + r0:HIST_B + r0 + ROW_CHUNK, :] = (
                    zseg(crow, 3, lanes) * jax.nn.sigmoid(zseg(crow, 4, lanes)))

        def convs(r0, gi):
            crow = slice(r0, r0 + CONV_ROWS)
            lanes = slice(gi * GROUP_W, (gi + 1) * GROUP_W)
            ya = zseg(crow, 0, lanes) * _causal_dwconv(ua_ext, gi, ca_ref, K_A, HIST_A, r0, CONV_ROWS)
            cat_ref[crow, lanes] = ya.astype(BF16)
            cbv_ref[crow, lanes] = _causal_dwconv(ub_ext, gi, cb_ref, K_B, HIST_B, r0, CONV_ROWS)

        def norm_act(r0):
            crow = slice(r0, r0 + CONV_ROWS)
            yb = _silu(_ln(cbv_ref[crow, :] + cbb_ref[...], lng_ref[...], lnb_ref[...]))
            cat_ref[crow, W_MIX:2 * W_MIX] = yb.astype(BF16)

        pieces = [functools.partial(gates, r0) for r0 in range(rows.start, rows.stop, ROW_CHUNK)]
        for r0 in range(rows.start, rows.stop, CONV_ROWS):
            pieces += [functools.partial(convs, r0, gi) for gi in range(N_GROUPS)]
            pieces.append(functools.partial(norm_act, r0))
        return pieces

    def out_pieces(rows):
        def piece(j):
            cols = slice(j * W_MIX, (j + 1) * W_MIX)
            o_ref[rows, cols] = x_ref[rows, cols] + _dot(cat_ref[rows, :], wout_ref[:, cols])

        return [functools.partial(piece, j) for j in range(D_MODEL // W_MIX)]

    blocks = [slice(r, r + MIX_BLOCK_ROWS) for r in range(0, tm, MIX_BLOCK_ROWS)]
    _emit_interleaved(project_pieces(blocks[0]), [])
    for i in range(len(blocks)):
        mxu = out_pieces(blocks[i - 1]) if i > 0 else []
        if i + 1 < len(blocks):
            mxu = project_pieces(blocks[i + 1]) + mxu
        _emit_interleaved(mxu, mix_pieces(blocks[i]))
    _emit_interleaved(out_pieces(blocks[-1]), [])

    @pl.when(s == pl.num_programs(1) - 1)
    def _():
        for gi in range(N_GROUPS):
            lanes = slice(gi * GROUP_W, (gi + 1) * GROUP_W)
            sa_ref[:, lanes] = ua_ext[gi, HIST_A + tm - (K_A - 1):HIST_A + tm, :]
            sb_ref[:, lanes] = ub_ext[gi, HIST_B + tm - (K_B - 1):HIST_B + tm, :]


def _even_prompt(x, g, win, ca, cb, cbb, lng, lnb, wout, tm, cast_next=()):
    b, s, _ = x.shape
    n_s = s // tm
    tile = pl.BlockSpec((None, tm, D_MODEL), lambda i, j: (i, j, 0))
    n_in = win.shape[1]
    c_in, c_out, c_shapes = _cast_plumbing(cast_next, b * n_s, lambda i, j: i * n_s + j)
    return pl.pallas_call(
        functools.partial(_even_prompt_kernel, tm=tm, n_cast=len(cast_next)),
        grid=(b, n_s),
        in_specs=[tile, _const_spec((1, D_MODEL)), _const_spec((D_MODEL, n_in)),
                  _const_spec((K_A, W_MIX)), _const_spec((K_B, W_MIX)), _const_spec((1, W_MIX)),
                  _const_spec((1, W_MIX)), _const_spec((1, W_MIX)),
                  _const_spec((2 * W_MIX, D_MODEL))] + c_in,
        out_specs=[tile,
                   pl.BlockSpec((None, K_A - 1, W_MIX), lambda i, j: (i, 0, 0)),
                   pl.BlockSpec((None, K_B - 1, W_MIX), lambda i, j: (i, 0, 0))] + c_out,
        out_shape=[jax.ShapeDtypeStruct((b, s, D_MODEL), F32),
                   jax.ShapeDtypeStruct((b, K_A - 1, W_MIX), F32),
                   jax.ShapeDtypeStruct((b, K_B - 1, W_MIX), F32)] + c_shapes,
        scratch_shapes=[pltpu.VMEM((tm, D_MODEL), BF16),
                        pltpu.VMEM((tm, n_in), F32),
                        pltpu.VMEM((N_GROUPS, tm + HIST_A, GROUP_W), F32),
                        pltpu.VMEM((N_GROUPS, tm + HIST_B, GROUP_W), F32),
                        pltpu.VMEM((tm, W_MIX), F32),
                        pltpu.VMEM((tm, 2 * W_MIX), BF16)],
        compiler_params=_params(2),
        name="even_prompt",
    )(x, g.reshape(1, D_MODEL), win, ca, cb, cbb.reshape(1, W_MIX), lng.reshape(1, W_MIX),
      lnb.reshape(1, W_MIX), wout, *(a for a, _ in cast_next))


def _tril_bf16(w):
    t = lax.broadcasted_iota(jnp.int32, (CHUNK, CHUNK), 0)
    s = lax.broadcasted_iota(jnp.int32, (CHUNK, CHUNK), 1)
    return jnp.where(s <= t, w, 0.0).astype(BF16)


def _odd_prompt_kernel(*refs, tm, n_cast):
    (x_ref, g_ref, win_ref, wpool_ref, pscale_ref, sguw_ref, sgubt_ref, lng_ref, lnb_ref,
     wout_ref) = refs[:10]
    cast_in = refs[10:10 + n_cast]
    o_ref, sp_ref = refs[10 + n_cast:12 + n_cast]
    cast_out = refs[12 + n_cast:12 + 2 * n_cast]
    hb_ref, p_ext, ps_ref, inv_ref, u_ref, vn_ref, d_ref, cat_ref, wm_ref = refs[12 + 2 * n_cast:]
    s = pl.program_id(1)
    _cast_rows(cast_in, cast_out)

    @pl.when(jnp.logical_and(pl.program_id(0) == 0, s == 0))
    def _():
        for gi in range(N_GROUPS):
            wm_ref[gi] = _tril_bf16(sguw_ref[gi])

    @pl.when(s == 0)
    def _():
        p_ext[:, 0:HIST_P, :] = jnp.zeros((N_GROUPS, HIST_P, GROUP_W), F32)

    @pl.when(s > 0)
    def _():
        p_ext[:, 0:HIST_P, :] = p_ext[:, tm:tm + HIST_P, :]

    assert ROW_CHUNK >= max(POOL_WINDOWS)
    pos0 = s * tm + lax.broadcasted_iota(jnp.int32, (ROW_CHUNK, GROUP_W), 0)
    for gi, w in enumerate(POOL_WINDOWS):
        inv_ref[gi] = 1.0 / jnp.minimum(pos0 + 1, w).astype(F32)

    def project_pieces(rows):
        def seg(i):
            return _dot(hb_ref[rows, :], win_ref[:, i * W_MIX:(i + 1) * W_MIX])

        def norm():
            hb_ref[rows, :] = _rms(x_ref[rows, :], g_ref[...]).astype(BF16)

        def pooled():
            pc = seg(0)
            for gi in range(N_GROUPS):
                p_ext[gi, HIST_P + rows.start:HIST_P + rows.stop, :] = pc[:, gi * GROUP_W:(gi + 1) * GROUP_W]

        def gate():
            u_ref[rows, :] = seg(1)

        def normed():
            vn_ref[rows, :] = _ln(seg(2), lng_ref[...], lnb_ref[...]).astype(BF16)

        return [norm, pooled, gate, normed]

    def pool_chunk(r0):
        n = ROW_CHUNK + HIST_P
        for gi, w in enumerate(POOL_WINDOWS):
            win = p_ext.at[gi, r0:r0 + n, :]
            scratch = ps_ref.at[r0 // ROW_CHUNK % 2, gi]
            prev, lo, step = win, 0, 1
            sums = win[0:n, :]
            while step < w:
                lo += SUBLANES
                sums = sums[SUBLANES:] + prev[lo - step:n - step, :]
                step *= 2
                if step < w:
                    scratch[lo:n, :] = sums
                    prev = scratch
            tok = win[HIST_P:n, :]
            inv = inv_ref[gi] if r0 == 0 else 1.0 / w
            lanes = slice(gi * GROUP_W, (gi + 1) * GROUP_W)
            d_ref[r0:r0 + ROW_CHUNK, lanes] = (sums[HIST_P - lo:] * inv - tok).astype(BF16)

    def pool_project(rows):
        for gi in range(N_GROUPS):
            lanes = slice(gi * GROUP_W, (gi + 1) * GROUP_W)
            yc = _dot(d_ref[rows, lanes], wpool_ref[gi]) * pscale_ref[:, lanes]
            cat_ref[rows, lanes] = yc.astype(BF16)

    def sgu_chunk(r0):
        rows = slice(r0, r0 + CHUNK)
        for gi in range(N_GROUPS):
            lanes = slice(gi * GROUP_W, (gi + 1) * GROUP_W)
            mixed = _dot(wm_ref[gi], vn_ref[rows, lanes]) + sgubt_ref[:, gi:gi + 1]
            yd = u_ref[rows, lanes] * mixed
            cat_ref[rows, W_MIX + gi * GROUP_W:W_MIX + (gi + 1) * GROUP_W] = yd.astype(BF16)

    def mix_pieces(rows):
        pieces = [functools.partial(pool_chunk, r0) for r0 in range(rows.start, rows.stop, ROW_CHUNK)]
        pieces.append(functools.partial(pool_project, rows))
        pieces += [functools.partial(sgu_chunk, r0) for r0 in range(rows.start, rows.stop, CHUNK)]
        return pieces

    def out_pieces(rows):
        def piece():
            o_ref[rows, :] = x_ref[rows, :] + _dot(cat_ref[rows, :], wout_ref[...])

        return [piece]

    blocks = [slice(r, r + MIX_BLOCK_ROWS) for r in range(0, tm, MIX_BLOCK_ROWS)]
    _emit_interleaved(project_pieces(blocks[0]), [])
    for i in range(len(blocks)):
        mxu = out_pieces(blocks[i - 1]) if i > 0 else []
        if i + 1 < len(blocks):
            mxu = project_pieces(blocks[i + 1]) + mxu
        _emit_interleaved(mxu, mix_pieces(blocks[i]))
    _emit_interleaved(out_pieces(blocks[-1]), [])

    @pl.when(s == pl.num_programs(1) - 1)
    def _():
        for gi in range(N_GROUPS):
            sp_ref[:, gi * GROUP_W:(gi + 1) * GROUP_W] = p_ext[gi, HIST_P + tm - POOL_HIST:HIST_P + tm, :]


def _odd_prompt(x, g, win, wpool, pscale, sguw, sgub, lng, lnb, wout, tm, cast_next=()):
    b, s, _ = x.shape
    n_s = s // tm
    tile = pl.BlockSpec((None, tm, D_MODEL), lambda i, j: (i, j, 0))
    n_in = win.shape[1]
    c_in, c_out, c_shapes = _cast_plumbing(cast_next, b * n_s, lambda i, j: i * n_s + j)
    return pl.pallas_call(
        functools.partial(_odd_prompt_kernel, tm=tm, n_cast=len(cast_next)),
        grid=(b, n_s),
        in_specs=[tile, _const_spec((1, D_MODEL)), _const_spec((D_MODEL, n_in)),
                  _const_spec((N_GROUPS, GROUP_W, GROUP_W)), _const_spec((1, W_MIX)),
                  _const_spec((N_GROUPS, CHUNK, CHUNK)), _const_spec((CHUNK, N_GROUPS)),
                  _const_spec((1, W_MIX)), _const_spec((1, W_MIX)),
                  _const_spec((2 * W_MIX, D_MODEL))] + c_in,
        out_specs=[tile, pl.BlockSpec((None, POOL_HIST, W_MIX), lambda i, j: (i, 0, 0))] + c_out,
        out_shape=[jax.ShapeDtypeStruct((b, s, D_MODEL), F32),
                   jax.ShapeDtypeStruct((b, POOL_HIST, W_MIX), F32)] + c_shapes,
        scratch_shapes=[pltpu.VMEM((tm, D_MODEL), BF16),
                        pltpu.VMEM((N_GROUPS, tm + HIST_P, GROUP_W), F32),
                        pltpu.VMEM((2, N_GROUPS, ROW_CHUNK + HIST_P, GROUP_W), F32),
                        pltpu.VMEM((len(POOL_WINDOWS), ROW_CHUNK, GROUP_W), F32),
                        pltpu.VMEM((tm, W_MIX), F32),
                        pltpu.VMEM((tm, W_MIX), BF16),
                        pltpu.VMEM((tm, W_MIX), BF16),
                        pltpu.VMEM((tm, 2 * W_MIX), BF16),
                        pltpu.VMEM((N_GROUPS, CHUNK, CHUNK), BF16)],
        compiler_params=_params(2),
        name="odd_prompt",
    )(x, g.reshape(1, D_MODEL), win, wpool, pscale.reshape(1, W_MIX), sguw, sgub.T,
      lng.reshape(1, W_MIX), lnb.reshape(1, W_MIX), wout, *(a for a, _ in cast_next))


def _even_sample_kernel(x_ref, g_ref, win_ref, ca_ref, cb_ref, cbb_ref, lng_ref, lnb_ref, wout_ref,
                        ha_ref, hbst_ref,
                        o_ref, sa_ref, sb_ref,
                        hb_ref, gate_ref, ua_ext, ub_ext, cbv_ref, cat_ref, *, t_len, n_seq,
                        x_seq_major):
    if x_seq_major:
        x = jnp.concatenate([x_ref[:, t, :] for t in range(t_len)], axis=0)
    else:
        x = x_ref[...].reshape(t_len * n_seq, D_MODEL)
    hb_ref[...] = _rms(x, g_ref[...]).astype(BF16)

    def seg(i):
        return _dot(hb_ref[...], win_ref[:, i * W_MIX:(i + 1) * W_MIX])

    gate_ref[...] = seg(0)
    for j in range(K_A - 1):
        ua_ext[j] = ha_ref[:, j, :]
    ub_ext[0:K_B - 1] = hbst_ref[...]
    ua_ext[K_A - 1:K_A - 1 + t_len] = (seg(1) * seg(2)).reshape(t_len, n_seq, W_MIX)
    ub_ext[K_B - 1:K_B - 1 + t_len] = (seg(3) * jax.nn.sigmoid(seg(4))).reshape(t_len, n_seq, W_MIX)

    def taps(ext_ref, w_ref, n_taps, t0, lanes):
        acc = None
        for k in range(n_taps):
            term = w_ref[k:k + 1, lanes] * ext_ref[t0 + k:t0 + k + SAMPLE_CONV_STEPS, :, lanes]
            acc = term if acc is None else acc + term
        return acc.reshape(SAMPLE_CONV_STEPS * n_seq, GROUP_W)

    for t0 in range(0, t_len, SAMPLE_CONV_STEPS):
        rows = slice(t0 * n_seq, (t0 + SAMPLE_CONV_STEPS) * n_seq)
        for gi in range(N_GROUPS):
            lanes = slice(gi * GROUP_W, (gi + 1) * GROUP_W)
            cat_ref[rows, lanes] = (gate_ref[rows, lanes] * taps(ua_ext, ca_ref, K_A, t0, lanes)).astype(BF16)
            cbv_ref[rows, lanes] = taps(ub_ext, cb_ref, K_B, t0, lanes) + cbb_ref[:, lanes]
        yb = _silu(_ln(cbv_ref[rows, :], lng_ref[...], lnb_ref[...]))
        cat_ref[rows, W_MIX:2 * W_MIX] = yb.astype(BF16)

    o_ref[...] = (x + _dot(cat_ref[...], wout_ref[...])).reshape(t_len, n_seq, D_MODEL)
    for j in range(K_A - 1):
        sa_ref[:, j, :] = ua_ext[t_len + j]
    sb_ref[...] = ub_ext[t_len:t_len + K_B - 1]


def _seq_block_spec(rows, nb, width):
    return pl.BlockSpec((rows, nb, width), lambda i, *_: (0, i, 0))


def _seq_major_spec(rows, nb, width):
    return pl.BlockSpec((nb, rows, width), lambda i, *_: (i, 0, 0))


def _xs_block_spec(t_len, nb):
    return pl.BlockSpec((None, t_len, nb, D_MODEL), lambda i, *_: (i, 0, 0, 0))


def _even_sample(x, g, win, ca, cb, cbb, lng, lnb, wout, hist_a, hist_b, nb, x_seq_major):
    n_seq, t_len = hist_a.shape[0], x.size // (hist_a.shape[0] * D_MODEL)
    m = t_len * nb
    n_in = win.shape[1]
    x_spec = _seq_major_spec(t_len, nb, D_MODEL) if x_seq_major else _xs_block_spec(t_len, nb)
    return pl.pallas_call(
        functools.partial(_even_sample_kernel, t_len=t_len, n_seq=nb, x_seq_major=x_seq_major),
        grid=(n_seq // nb,),
        in_specs=[x_spec, _const_spec((1, D_MODEL)),
                  _const_spec((D_MODEL, n_in)),
                  _const_spec((K_A, W_MIX)), _const_spec((K_B, W_MIX)), _const_spec((1, W_MIX)),
                  _const_spec((1, W_MIX)), _const_spec((1, W_MIX)), _const_spec((2 * W_MIX, D_MODEL)),
                  _seq_major_spec(K_A - 1, nb, W_MIX), _seq_block_spec(K_B - 1, nb, W_MIX)],
        out_specs=[_xs_block_spec(t_len, nb), _seq_major_spec(K_A - 1, nb, W_MIX),
                   _seq_block_spec(K_B - 1, nb, W_MIX)],
        out_shape=[jax.ShapeDtypeStruct((n_seq // nb, t_len, nb, D_MODEL), F32),
                   jax.ShapeDtypeStruct((n_seq, K_A - 1, W_MIX), F32),
                   jax.ShapeDtypeStruct((K_B - 1, n_seq, W_MIX), F32)],
        scratch_shapes=[pltpu.VMEM((m, D_MODEL), BF16),
                        pltpu.VMEM((m, W_MIX), F32),
                        pltpu.VMEM((t_len + K_A - 1, nb, W_MIX), F32),
                        pltpu.VMEM((t_len + K_B - 1, nb, W_MIX), F32),
                        pltpu.VMEM((m, W_MIX), F32),
                        pltpu.VMEM((m, 2 * W_MIX), BF16)],
        compiler_params=_params(1),
        name="even_sample",
    )(x, g.reshape(1, D_MODEL), win, ca, cb, cbb.reshape(1, W_MIX), lng.reshape(1, W_MIX),
      lnb.reshape(1, W_MIX), wout, hist_a, hist_b)


def _odd_sample_kernel(sw_ref, sb_ref_smem,
                       x_ref, g_ref, win_ref, wpool_ref, pscale_ref, lng_ref, lnb_ref, wout_ref,
                       hp_ref,
                       o_ref, sp_ref, vn_out_ref,
                       hb_ref, p_ext, u_ref, vn_ref, d_ref, cat_ref, *, t_len, n_seq, start_pos):
    x = x_ref[...].reshape(t_len * n_seq, D_MODEL)
    hb_ref[...] = _rms(x, g_ref[...]).astype(BF16)

    def seg(i):
        return _dot(hb_ref[...], win_ref[:, i * W_MIX:(i + 1) * W_MIX])

    p_ext[0:POOL_HIST] = hp_ref[...]
    p_ext[POOL_HIST:POOL_HIST + t_len] = seg(0).reshape(t_len, n_seq, W_MIX)
    u_ref[...] = seg(1).reshape(t_len, n_seq, W_MIX)
    vn_ref[...] = _ln(seg(2), lng_ref[...], lnb_ref[...]).reshape(t_len, n_seq, W_MIX)

    for t in range(t_len):
        rows = slice(t * n_seq, (t + 1) * n_seq)
        for gi, w in enumerate(POOL_WINDOWS):
            lanes = slice(gi * GROUP_W, (gi + 1) * GROUP_W)
            tok = p_ext[POOL_HIST + t, :, lanes]
            wsum = tok
            for j in range(1, w):
                wsum = wsum + p_ext[POOL_HIST + t - j, :, lanes]
            cnt = float(min(start_pos + t + 1, w))
            d_ref[rows, lanes] = (wsum / cnt - tok).astype(BF16)
            mixed = jnp.zeros((n_seq, GROUP_W), F32) + sb_ref_smem[gi * t_len + t]
            for s in range(t + 1):
                mixed = mixed + sw_ref[(gi * t_len + t) * t_len + s] * vn_ref[s, :, lanes]
            cat_ref[rows, W_MIX + gi * GROUP_W:W_MIX + (gi + 1) * GROUP_W] = (
                u_ref[t, :, lanes] * mixed).astype(BF16)

    for gi in range(N_GROUPS):
        lanes = slice(gi * GROUP_W, (gi + 1) * GROUP_W)
        yc = _dot(d_ref[:, lanes], wpool_ref[gi]) * pscale_ref[:, lanes]
        cat_ref[:, lanes] = yc.astype(BF16)

    o_ref[...] = (x + _dot(cat_ref[...], wout_ref[...])).reshape(t_len, n_seq, D_MODEL)
    sp_ref[...] = p_ext[t_len:t_len + POOL_HIST]
    for t in range(t_len):
        vn_out_ref[:, t, :] = vn_ref[t]


def _odd_sample(x, g, win, wpool, pscale, sguw, sgub, lng, lnb, wout, hist_p, nb, start_pos):
    n_blocks, t_len, _, _ = x.shape
    n_seq = n_blocks * nb
    m = t_len * nb
    n_in = win.shape[1]
    assert t_len <= CHUNK
    sw = sguw[:, :t_len, :t_len].reshape(-1)
    sb = sgub[:, :t_len].reshape(-1)

    grid_spec = pltpu.PrefetchScalarGridSpec(
        num_scalar_prefetch=2,
        grid=(n_seq // nb,),
        in_specs=[_xs_block_spec(t_len, nb), _const_spec((1, D_MODEL)),
                  _const_spec((D_MODEL, n_in)),
                  _const_spec((N_GROUPS, GROUP_W, GROUP_W)), _const_spec((1, W_MIX)),
                  _const_spec((1, W_MIX)), _const_spec((1, W_MIX)),
                  _const_spec((2 * W_MIX, D_MODEL)),
                  _seq_block_spec(POOL_HIST, nb, W_MIX)],
        out_specs=[_xs_block_spec(t_len, nb), _seq_block_spec(POOL_HIST, nb, W_MIX),
                   _seq_major_spec(t_len, nb, W_MIX)],
        scratch_shapes=[pltpu.VMEM((m, D_MODEL), BF16),
                        pltpu.VMEM((t_len + POOL_HIST, nb, W_MIX), F32),
                        pltpu.VMEM((t_len, nb, W_MIX), F32),
                        pltpu.VMEM((t_len, nb, W_MIX), F32),
                        pltpu.VMEM((m, W_MIX), BF16),
                        pltpu.VMEM((m, 2 * W_MIX), BF16)],
    )
    return pl.pallas_call(
        functools.partial(_odd_sample_kernel, t_len=t_len, n_seq=nb, start_pos=start_pos),
        grid_spec=grid_spec,
        out_shape=[jax.ShapeDtypeStruct((n_blocks, t_len, nb, D_MODEL), F32),
                   jax.ShapeDtypeStruct((POOL_HIST, n_seq, W_MIX), F32),
                   jax.ShapeDtypeStruct((n_seq, t_len, W_MIX), F32)],
        compiler_params=_params(1),
        name="odd_sample",
    )(sw, sb, x, g.reshape(1, D_MODEL), win, wpool, pscale.reshape(1, W_MIX),
      lng.reshape(1, W_MIX), lnb.reshape(1, W_MIX), wout, hist_p)


def _time_major(a):
    return jnp.swapaxes(a, 0, 1)


def kernel(x_prompt, x_sample, state_conv_a, state_conv_b, state_pool, norm_mix, norm_ffn, ev_w_in,
           ev_conv_a, ev_conv_b, ev_conv_b_bias, ev_ln_g, ev_ln_b, ev_w_out, od_w_in, od_pool_w,
           od_pool_scale, od_sgu_w, od_sgu_b, od_sgu_ln_g, od_sgu_ln_b, od_w_out, ffn_w1, ffn_w2,
           norm_final):
    depth = norm_mix.shape[0]
    batch, seq, _ = x_prompt.shape
    n_seq, t_len, _ = x_sample.shape
    tm_mix = 1024
    tm_ffn = 512
    nb = tm_ffn // t_len

    def mixer_weights(l):
        i = l // 2
        return ((ev_w_in, i), (ev_w_out, i)) if l % 2 == 0 else ((od_w_in, i), (od_w_out, i))

    xp = x_prompt
    xs = x_sample
    sa_p, sa_s, sb_p, sb_s, sc_p, sc_s, vn_s = [], [], [], [], [], [], []
    win, wout = (w[i].astype(BF16) for w, i in mixer_weights(0))
    w1 = w2 = None
    for l in range(depth):
        i = l // 2
        ffn_f32 = ((ffn_w1, l), (ffn_w2, l)) if l == 0 else ()
        if l % 2 == 0:
            common = (norm_mix[l], win, ev_conv_a[i], ev_conv_b[i], ev_conv_b_bias[i], ev_ln_g[i],
                      ev_ln_b[i], wout)
            xp, a_p, b_p, *own_ffn = _even_prompt(xp, *common, tm=tm_mix, cast_next=ffn_f32)
            xs, a_s, b_s = _even_sample(xs, *common, state_conv_a[i], _time_major(state_conv_b[i]),
                                        nb, x_seq_major=(l == 0))
            sa_p.append(a_p)
            sb_p.append(b_p)
            sa_s.append(a_s)
            sb_s.append(_time_major(b_s))
        else:
            common = (norm_mix[l], win, od_pool_w[i].astype(BF16), od_pool_scale[i], od_sgu_w[i],
                      od_sgu_b[i], od_sgu_ln_g[i], od_sgu_ln_b[i], wout)
            xp, c_p, *own_ffn = _odd_prompt(xp, *common, tm=tm_mix, cast_next=ffn_f32)
            xs, c_s, v_s = _odd_sample(xs, *common, _time_major(state_pool[i]), nb, PAST_LEN)
            sc_p.append(c_p)
            sc_s.append(_time_major(c_s))
            vn_s.append(v_s)
        if own_ffn:
            w1, w2 = own_ffn
        last = l == depth - 1
        next_f32 = () if last else mixer_weights(l + 1) + ((ffn_w1, l + 1), (ffn_w2, l + 1))
        xp, xs, *next_w = _ffn(xp.reshape(batch * seq, D_MODEL), xs.reshape(t_len * n_seq, D_MODEL),
                               norm_ffn[l], w1, w2, norm_final if last else None, tm_ffn, t_len,
                               cast_next=next_f32)
        xp = xp.reshape(batch, seq, D_MODEL)
        if not last:
            xs = xs.reshape(n_seq // nb, t_len, nb, D_MODEL)
            win, wout, w1, w2 = next_w

    return (xp, xs, jnp.stack(sa_p), jnp.stack(sa_s), jnp.stack(sb_p), jnp.stack(sb_s),
            jnp.stack(sc_p), jnp.stack(sc_s), jnp.stack(vn_s))
```

```python
import functools

import jax
import jax.numpy as jnp
from jax import lax
from jax.experimental import pallas as pl
from jax.experimental.pallas import tpu as pltpu

F32 = jnp.float32
BF16 = jnp.bfloat16

D_MODEL = 1024
W_MIX = D_MODEL // 2
K_A = 3
K_B = 31
POOL_WINDOWS = (2, 4, 8, 16)
POOL_HIST = max(POOL_WINDOWS) - 1
CHUNK = 128
N_GROUPS = 4
GROUP_W = W_MIX // N_GROUPS
D_FF = 4 * D_MODEL
PAST_LEN = 16384
EPS = 1e-6

SUBLANES = 8
HIST_A = 8
HIST_B = 32
HIST_P = SUBLANES * len(POOL_WINDOWS)
ROW_CHUNK = 64
CONV_ROWS = 16
MIX_BLOCK_ROWS = 256
SAMPLE_CONV_STEPS = 4
FFN_COL_CHUNK = 1024
VMEM_LIMIT = 56 * 1024 * 1024


def _rms(x, g):
    return x * lax.rsqrt(jnp.mean(x * x, axis=-1, keepdims=True) + EPS) * g


def _ln(x, g, b):
    mu = jnp.mean(x, axis=-1, keepdims=True)
    xc = x - mu
    return xc * lax.rsqrt(jnp.mean(xc * xc, axis=-1, keepdims=True) + EPS) * g + b


def _dot(a, b):
    return jnp.dot(a, b, preferred_element_type=F32)


def _silu(x):
    return x * jax.nn.sigmoid(x)


def _const_spec(shape):
    nd = len(shape)
    return pl.BlockSpec(shape, lambda *_: (0,) * nd, pipeline_mode=pl.Buffered(1))


def _emit_interleaved(mxu_pieces, vpu_pieces):
    n_m, n_v = len(mxu_pieces), len(vpu_pieces)
    i = j = 0
    while i < n_m or j < n_v:
        if j >= n_v or (i < n_m and i * n_v <= j * n_m):
            mxu_pieces[i]()
            i += 1
        else:
            vpu_pieces[j]()
            j += 1


def _params(n_grid_axes):
    return pltpu.CompilerParams(
        dimension_semantics=("arbitrary",) * n_grid_axes,
        vmem_limit_bytes=VMEM_LIMIT,
    )


def _cast_plumbing(arrays, n_steps, step_of):
    in_specs, out_specs, shapes = [], [], []
    for stacked, layer in arrays:
        _, n_rows, n_cols = stacked.shape
        rows, rem = divmod(n_rows, n_steps)
        assert rem == 0 and rows % (2 * SUBLANES) == 0, stacked.shape
        in_specs.append(pl.BlockSpec((None, rows, n_cols),
                                     lambda *idx, _l=layer: (_l, step_of(*idx), 0)))
        out_specs.append(pl.BlockSpec((rows, n_cols), lambda *idx: (step_of(*idx), 0)))
        shapes.append(jax.ShapeDtypeStruct((n_rows, n_cols), BF16))
    return in_specs, out_specs, shapes


def _cast_rows(src_refs, dst_refs):
    for src, dst in zip(src_refs, dst_refs, strict=True):
        dst[...] = src[...].astype(BF16)


def _ffn_kernel(*refs, final, n_prompt, n_cast, t_len):
    n_in = 6 if final else 5
    xp_ref, xs_ref, g_ref, w1_ref, w2_ref = refs[:5]
    cast_in = refs[n_in:n_in + n_cast]
    op_ref, os_ref = refs[n_in + n_cast:n_in + n_cast + 2]
    cast_out = refs[n_in + n_cast + 2:n_in + 2 * n_cast + 2]
    hb_ref, act_ref = refs[n_in + 2 * n_cast + 2:]
    step = pl.program_id(0)

    def rows_block(x_ref):
        hb_ref[...] = _rms(x_ref[...], g_ref[...]).astype(BF16)
        for c in range(D_FF // FFN_COL_CHUNK):
            cols = slice(c * FFN_COL_CHUNK, (c + 1) * FFN_COL_CHUNK)
            a = _dot(hb_ref[...], w1_ref[:, cols])
            act_ref[:, cols] = jnp.square(jnp.maximum(a, 0.0)).astype(BF16)
        y = x_ref[...] + _dot(act_ref[...], w2_ref[...])
        return _rms(y, refs[5][...]) if final else y

    @pl.when(step < n_prompt)
    def _():
        op_ref[...] = rows_block(xp_ref)
        _cast_rows(cast_in, cast_out)

    @pl.when(step >= n_prompt)
    def _():
        y = rows_block(xs_ref)
        if final:
            nb = y.shape[0] // t_len
            for t in range(t_len):
                os_ref[:, t, :] = y[t * nb:(t + 1) * nb]
        else:
            os_ref[...] = y


def _ffn(xp2d, xs2d, g, w1, w2, g_final, tm, t_len, cast_next=()):
    n_prompt, n_sample = xp2d.shape[0] // tm, xs2d.shape[0] // tm
    final = g_final is not None
    last = n_prompt - 1
    p_spec = pl.BlockSpec((tm, D_MODEL), lambda i: (jnp.minimum(i, last), 0))
    s_spec = pl.BlockSpec((tm, D_MODEL), lambda i: (jnp.maximum(i - n_prompt, 0), 0))
    in_specs = [p_spec, s_spec, _const_spec((1, D_MODEL)), _const_spec((D_MODEL, D_FF)),
                _const_spec((D_FF, D_MODEL))]
    args = [xp2d, xs2d, g.reshape(1, D_MODEL), w1, w2]
    s_out_spec, s_out_shape = s_spec, jax.ShapeDtypeStruct(xs2d.shape, F32)
    if final:
        in_specs.append(_const_spec((1, D_MODEL)))
        args.append(g_final.reshape(1, D_MODEL))
        nb = tm // t_len
        s_out_spec = pl.BlockSpec((nb, t_len, D_MODEL), lambda i: (jnp.maximum(i - n_prompt, 0), 0, 0))
        s_out_shape = jax.ShapeDtypeStruct((xs2d.shape[0] // t_len, t_len, D_MODEL), F32)
    c_in, c_out, c_shapes = _cast_plumbing(cast_next, n_prompt, lambda i: jnp.minimum(i, last))
    return pl.pallas_call(
        functools.partial(_ffn_kernel, final=final, n_prompt=n_prompt, n_cast=len(cast_next),
                          t_len=t_len),
        grid=(n_prompt + n_sample,),
        in_specs=in_specs + c_in,
        out_specs=[p_spec, s_out_spec] + c_out,
        out_shape=[jax.ShapeDtypeStruct(xp2d.shape, F32), s_out_shape] + c_shapes,
        scratch_shapes=[pltpu.VMEM((tm, D_MODEL), BF16), pltpu.VMEM((tm, D_FF), BF16)],
        compiler_params=_params(1),
        name="ffn_final" if final else "ffn",
    )(*args, *(a for a, _ in cast_next))


def _causal_dwconv(ext_ref, gi, w_ref, n_taps, hist, r0, rows):
    lanes = slice(gi * GROUP_W, (gi + 1) * GROUP_W)
    acc = None
    for k in range(n_taps):
        lo = hist + r0 - (n_taps - 1) + k
        term = w_ref[k:k + 1, lanes] * ext_ref[gi, lo:lo + rows, :]
        acc = term if acc is None else acc + term
    return acc


def _even_prompt_kernel(*refs, tm, n_cast):
    x_ref, g_ref, win_ref, ca_ref, cb_ref, cbb_ref, lng_ref, lnb_ref, wout_ref = refs[:9]
    cast_in = refs[9:9 + n_cast]
    o_ref, sa_ref, sb_ref = refs[9 + n_cast:12 + n_cast]
    cast_out = refs[12 + n_cast:12 + 2 * n_cast]
    hb_ref, z_ref, ua_ext, ub_ext, cbv_ref, cat_ref = refs[12 + 2 * n_cast:]
    s = pl.program_id(1)
    _cast_rows(cast_in, cast_out)

    @pl.when(s == 0)
    def _():
        ua_ext[:, 0:HIST_A, :] = jnp.zeros((N_GROUPS, HIST_A, GROUP_W), F32)
        ub_ext[:, 0:HIST_B, :] = jnp.zeros((N_GROUPS, HIST_B, GROUP_W), F32)

    @pl.when(s > 0)
    def _():
        ua_ext[:, 0:HIST_A, :] = ua_ext[:, tm:tm + HIST_A, :]
        ub_ext[:, 0:HIST_B, :] = ub_ext[:, tm:tm + HIST_B, :]

    def zseg(rows, i, lanes=slice(0, W_MIX)):
        return z_ref[rows, i * W_MIX + lanes.start:i * W_MIX + lanes.stop]

    def project_pieces(rows):
        def norm():
            hb_ref[rows, :] = _rms(x_ref[rows, :], g_ref[...]).astype(BF16)

        def dot_piece(i):
            cols = slice(i * W_MIX, (i + 1) * W_MIX)
            z_ref[rows, cols] = _dot(hb_ref[rows, :], win_ref[:, cols])

        return [norm] + [functools.partial(dot_piece, i) for i in (3, 4, 1, 2, 0)]

    def mix_pieces(rows):
        def gates(r0):
            crow = slice(r0, r0 + ROW_CHUNK)
            for gi in range(N_GROUPS):
                lanes = slice(gi * GROUP_W, (gi + 1) * GROUP_W)
                ua_ext[gi, HIST_A + r0:HIST_A + r0 + ROW_CHUNK, :] = (
                    zseg(crow, 1, lanes) * zseg(crow, 2, lanes))
                ub_ext[gi, HIST_B + r0:HIST_B + r0 + ROW_CHUNK, :] = (
                    zseg(crow, 3, lanes) * jax.nn.sigmoid(zseg(crow, 4, lanes)))

        def convs(r0, gi):
            crow = slice(r0, r0 + CONV_ROWS)
            lanes = slice(gi * GROUP_W, (gi + 1) * GROUP_W)
            ya = zseg(crow, 0, lanes) * _causal_dwconv(ua_ext, gi, ca_ref, K_A, HIST_A, r0, CONV_ROWS)
            cat_ref[crow, lanes] = ya.astype(BF16)
            cbv_ref[crow, lanes] = _causal_dwconv(ub_ext, gi, cb_ref, K_B, HIST_B, r0, CONV_ROWS)

        def norm_act(r0):
            crow = slice(r0, r0 + CONV_ROWS)
            yb = _silu(_ln(cbv_ref[crow, :] + cbb_ref[...], lng_ref[...], lnb_ref[...]))
            cat_ref[crow, W_MIX:2 * W_MIX] = yb.astype(BF16)

        pieces = [functools.partial(gates, r0) for r0 in range(rows.start, rows.stop, ROW_CHUNK)]
        for r0 in range(rows.start, rows.stop, CONV_ROWS):
            pieces += [functools.partial(convs, r0, gi) for gi in range(N_GROUPS)]
            pieces.append(functools.partial(norm_act, r0))
        return pieces

    def out_pieces(rows):
        def piece(j):
            cols = slice(j * W_MIX, (j + 1) * W_MIX)
            o_ref[rows, cols] = x_ref[rows, cols] + _dot(cat_ref[rows, :], wout_ref[:, cols])

        return [functools.partial(piece, j) for j in range(D_MODEL // W_MIX)]

    blocks = [slice(r, r + MIX_BLOCK_ROWS) for r in range(0, tm, MIX_BLOCK_ROWS)]
    _emit_interleaved(project_pieces(blocks[0]), [])
    for i in range(len(blocks)):
        mxu = out_pieces(blocks[i - 1]) if i > 0 else []
        if i + 1 < len(blocks):
            mxu = project_pieces(blocks[i + 1]) + mxu
        _emit_interleaved(mxu, mix_pieces(blocks[i]))
    _emit_interleaved(out_pieces(blocks[-1]), [])

    @pl.when(s == pl.num_programs(1) - 1)
    def _():
        for gi in range(N_GROUPS):
            lanes = slice(gi * GROUP_W, (gi + 1) * GROUP_W)
            sa_ref[:, lanes] = ua_ext[gi, HIST_A + tm - (K_A - 1):HIST_A + tm, :]
            sb_ref[:, lanes] = ub_ext[gi, HIST_B + tm - (K_B - 1):HIST_B + tm, :]


def _even_prompt(x, g, win, ca, cb, cbb, lng, lnb, wout, tm, cast_next=()):
    b, s, _ = x.shape
    n_s = s // tm
    tile = pl.BlockSpec((None, tm, D_MODEL), lambda i, j: (i, j, 0))
    n_in = win.shape[1]
    c_in, c_out, c_shapes = _cast_plumbing(cast_next, b * n_s, lambda i, j: i * n_s + j)
    return pl.pallas_call(
        functools.partial(_even_prompt_kernel, tm=tm, n_cast=len(cast_next)),
        grid=(b, n_s),
        in_specs=[tile, _const_spec((1, D_MODEL)), _const_spec((D_MODEL, n_in)),
                  _const_spec((K_A, W_MIX)), _const_spec((K_B, W_MIX)), _const_spec((1, W_MIX)),
                  _const_spec((1, W_MIX)), _const_spec((1, W_MIX)),
                  _const_spec((2 * W_MIX, D_MODEL))] + c_in,
        out_specs=[tile,
                   pl.BlockSpec((None, K_A - 1, W_MIX), lambda i, j: (i, 0, 0)),
                   pl.BlockSpec((None, K_B - 1, W_MIX), lambda i, j: (i, 0, 0))] + c_out,
        out_shape=[jax.ShapeDtypeStruct((b, s, D_MODEL), F32),
                   jax.ShapeDtypeStruct((b, K_A - 1, W_MIX), F32),
                   jax.ShapeDtypeStruct((b, K_B - 1, W_MIX), F32)] + c_shapes,
        scratch_shapes=[pltpu.VMEM((tm, D_MODEL), BF16),
                        pltpu.VMEM((tm, n_in), F32),
                        pltpu.VMEM((N_GROUPS, tm + HIST_A, GROUP_W), F32),
                        pltpu.VMEM((N_GROUPS, tm + HIST_B, GROUP_W), F32),
                        pltpu.VMEM((tm, W_MIX), F32),
                        pltpu.VMEM((tm, 2 * W_MIX), BF16)],
        compiler_params=_params(2),
        name="even_prompt",
    )(x, g.reshape(1, D_MODEL), win, ca, cb, cbb.reshape(1, W_MIX), lng.reshape(1, W_MIX),
      lnb.reshape(1, W_MIX), wout, *(a for a, _ in cast_next))


def _tril_bf16(w):
    t = lax.broadcasted_iota(jnp.int32, (CHUNK, CHUNK), 0)
    s = lax.broadcasted_iota(jnp.int32, (CHUNK, CHUNK), 1)
    return jnp.where(s <= t, w, 0.0).astype(BF16)


def _odd_prompt_kernel(*refs, tm, n_cast):
    (x_ref, g_ref, win_ref, wpool_ref, pscale_ref, sguw_ref, sgubt_ref, lng_ref, lnb_ref,
     wout_ref) = refs[:10]
    cast_in = refs[10:10 + n_cast]
    o_ref, sp_ref = refs[10 + n_cast:12 + n_cast]
    cast_out = refs[12 + n_cast:12 + 2 * n_cast]
    hb_ref, p_ext, ps_ref, inv_ref, u_ref, vn_ref, d_ref, cat_ref, wm_ref = refs[12 + 2 * n_cast:]
    s = pl.program_id(1)
    _cast_rows(cast_in, cast_out)

    @pl.when(jnp.logical_and(pl.program_id(0) == 0, s == 0))
    def _():
        for gi in range(N_GROUPS):
            wm_ref[gi] = _tril_bf16(sguw_ref[gi])

    @pl.when(s == 0)
    def _():
        p_ext[:, 0:HIST_P, :] = jnp.zeros((N_GROUPS, HIST_P, GROUP_W), F32)

    @pl.when(s > 0)
    def _():
        p_ext[:, 0:HIST_P, :] = p_ext[:, tm:tm + HIST_P, :]

    assert ROW_CHUNK >= max(POOL_WINDOWS)
    pos0 = s * tm + lax.broadcasted_iota(jnp.int32, (ROW_CHUNK, GROUP_W), 0)
    for gi, w in enumerate(POOL_WINDOWS):
        inv_ref[gi] = 1.0 / jnp.minimum(pos0 + 1, w).astype(F32)

    def project_pieces(rows):
        def seg(i):
            return _dot(hb_ref[rows, :], win_ref[:, i * W_MIX:(i + 1) * W_MIX])

        def norm():
            hb_ref[rows, :] = _rms(x_ref[rows, :], g_ref[...]).astype(BF16)

        def pooled():
            pc = seg(0)
            for gi in range(N_GROUPS):
                p_ext[gi, HIST_P + rows.start:HIST_P + rows.stop, :] = pc[:, gi * GROUP_W:(gi + 1) * GROUP_W]

        def gate():
            u_ref[rows, :] = seg(1)

        def normed():
            vn_ref[rows, :] = _ln(seg(2), lng_ref[...], lnb_ref[...]).astype(BF16)

        return [norm, pooled, gate, normed]

    def pool_chunk(r0):
        n = ROW_CHUNK + HIST_P
        for gi, w in enumerate(POOL_WINDOWS):
            win = p_ext.at[gi, r0:r0 + n, :]
            scratch = ps_ref.at[r0 // ROW_CHUNK % 2, gi]
            prev, lo, step = win, 0, 1
            sums = win[0:n, :]
            while step < w:
                lo += SUBLANES
                sums = sums[SUBLANES:] + prev[lo - step:n - step, :]
                step *= 2
                if step < w:
                    scratch[lo:n, :] = sums
                    prev = scratch
            tok = win[HIST_P:n, :]
            inv = inv_ref[gi] if r0 == 0 else 1.0 / w
            lanes = slice(gi * GROUP_W, (gi + 1) * GROUP_W)
            d_ref[r0:r0 + ROW_CHUNK, lanes] = (sums[HIST_P - lo:] * inv - tok).astype(BF16)

    def pool_project(rows):
        for gi in range(N_GROUPS):
            lanes = slice(gi * GROUP_W, (gi + 1) * GROUP_W)
            yc = _dot(d_ref[rows, lanes], wpool_ref[gi]) * pscale_ref[:, lanes]
            cat_ref[rows, lanes] = yc.astype(BF16)

    def sgu_chunk(r0):
        rows = slice(r0, r0 + CHUNK)
        for gi in range(N_GROUPS):
            lanes = slice(gi * GROUP_W, (gi + 1) * GROUP_W)
            mixed = _dot(wm_ref[gi], vn_ref[rows, lanes]) + sgubt_ref[:, gi:gi + 1]
            yd = u_ref[rows, lanes] * mixed
            cat_ref[rows, W_MIX + gi * GROUP_W:W_MIX + (gi + 1) * GROUP_W] = yd.astype(BF16)

    def mix_pieces(rows):
        pieces = [functools.partial(pool_chunk, r0) for r0 in range(rows.start, rows.stop, ROW_CHUNK)]
        pieces.append(functools.partial(pool_project, rows))
        pieces += [functools.partial(sgu_chunk, r0) for r0 in range(rows.start, rows.stop, CHUNK)]
        return pieces

    def out_pieces(rows):
        def piece():
            o_ref[rows, :] = x_ref[rows, :] + _dot(cat_ref[rows, :], wout_ref[...])

        return [piece]

    blocks = [slice(r, r + MIX_BLOCK_ROWS) for r in range(0, tm, MIX_BLOCK_ROWS)]
    _emit_interleaved(project_pieces(blocks[0]), [])
    for i in range(len(blocks)):
        mxu = out_pieces(blocks[i - 1]) if i > 0 else []
        if i + 1 < len(blocks):
            mxu = project_pieces(blocks[i + 1]) + mxu
        _emit_interleaved(mxu, mix_pieces(blocks[i]))
    _emit_interleaved(out_pieces(blocks[-1]), [])

    @pl.when(s == pl.num_programs(1) - 1)
    def _():
        for gi in range(N_GROUPS):
            sp_ref[:, gi * GROUP_W:(gi + 1) * GROUP_W] = p_ext[gi, HIST_P + tm - POOL_HIST:HIST_P + tm, :]


def _odd_prompt(x, g, win, wpool, pscale, sguw, sgub, lng, lnb, wout, tm, cast_next=()):
    b, s, _ = x.shape
    n_s = s // tm
    tile = pl.BlockSpec((None, tm, D_MODEL), lambda i, j: (i, j, 0))
    n_in = win.shape[1]
    c_in, c_out, c_shapes = _cast_plumbing(cast_next, b * n_s, lambda i, j: i * n_s + j)
    return pl.pallas_call(
        functools.partial(_odd_prompt_kernel, tm=tm, n_cast=len(cast_next)),
        grid=(b, n_s),
        in_specs=[tile, _const_spec((1, D_MODEL)), _const_spec((D_MODEL, n_in)),
                  _const_spec((N_GROUPS, GROUP_W, GROUP_W)), _const_spec((1, W_MIX)),
                  _const_spec((N_GROUPS, CHUNK, CHUNK)), _const_spec((CHUNK, N_GROUPS)),
                  _const_spec((1, W_MIX)), _const_spec((1, W_MIX)),
                  _const_spec((2 * W_MIX, D_MODEL))] + c_in,
        out_specs=[tile, pl.BlockSpec((None, POOL_HIST, W_MIX), lambda i, j: (i, 0, 0))] + c_out,
        out_shape=[jax.ShapeDtypeStruct((b, s, D_MODEL), F32),
                   jax.ShapeDtypeStruct((b, POOL_HIST, W_MIX), F32)] + c_shapes,
        scratch_shapes=[pltpu.VMEM((tm, D_MODEL), BF16),
                        pltpu.VMEM((N_GROUPS, tm + HIST_P, GROUP_W), F32),
                        pltpu.VMEM((2, N_GROUPS, ROW_CHUNK + HIST_P, GROUP_W), F32),
                        pltpu.VMEM((len(POOL_WINDOWS), ROW_CHUNK, GROUP_W), F32),
                        pltpu.VMEM((tm, W_MIX), F32),
                        pltpu.VMEM((tm, W_MIX), BF16),
                        pltpu.VMEM((tm, W_MIX), BF16),
                        pltpu.VMEM((tm, 2 * W_MIX), BF16),
                        pltpu.VMEM((N_GROUPS, CHUNK, CHUNK), BF16)],
        compiler_params=_params(2),
        name="odd_prompt",
    )(x, g.reshape(1, D_MODEL), win, wpool, pscale.reshape(1, W_MIX), sguw, sgub.T,
      lng.reshape(1, W_MIX), lnb.reshape(1, W_MIX), wout, *(a for a, _ in cast_next))


def _even_sample_kernel(x_ref, g_ref, win_ref, ca_ref, cb_ref, cbb_ref, lng_ref, lnb_ref, wout_ref,
                        ha_ref, hbst_ref,
                        o_ref, sa_ref, sb_ref,
                        hb_ref, gate_ref, ua_ext, ub_ext, cbv_ref, cat_ref, *, t_len, n_seq,
                        x_seq_major):
    if x_seq_major:
        x = jnp.concatenate([x_ref[:, t, :] for t in range(t_len)], axis=0)
    else:
        x = x_ref[...].reshape(t_len * n_seq, D_MODEL)
    hb_ref[...] = _rms(x, g_ref[...]).astype(BF16)

    def seg(i):
        return _dot(hb_ref[...], win_ref[:, i * W_MIX:(i + 1) * W_MIX])

    gate_ref[...] = seg(0)
    for j in range(K_A - 1):
        ua_ext[j] = ha_ref[:, j, :]
    ub_ext[0:K_B - 1] = hbst_ref[...]
    ua_ext[K_A - 1:K_A - 1 + t_len] = (seg(1) * seg(2)).reshape(t_len, n_seq, W_MIX)
    ub_ext[K_B - 1:K_B - 1 + t_len] = (seg(3) * jax.nn.sigmoid(seg(4))).reshape(t_len, n_seq, W_MIX)

    def taps(ext_ref, w_ref, n_taps, t0, lanes):
        acc = None
        for k in range(n_taps):
            term = w_ref[k:k + 1, lanes] * ext_ref[t0 + k:t0 + k + SAMPLE_CONV_STEPS, :, lanes]
            acc = term if acc is None else acc + term
        return acc.reshape(SAMPLE_CONV_STEPS * n_seq, GROUP_W)

    for t0 in range(0, t_len, SAMPLE_CONV_STEPS):
        rows = slice(t0 * n_seq, (t0 + SAMPLE_CONV_STEPS) * n_seq)
        for gi in range(N_GROUPS):
            lanes = slice(gi * GROUP_W, (gi + 1) * GROUP_W)
            cat_ref[rows, lanes] = (gate_ref[rows, lanes] * taps(ua_ext, ca_ref, K_A, t0, lanes)).astype(BF16)
            cbv_ref[rows, lanes] = taps(ub_ext, cb_ref, K_B, t0, lanes) + cbb_ref[:, lanes]
        yb = _silu(_ln(cbv_ref[rows, :], lng_ref[...], lnb_ref[...]))
        cat_ref[rows, W_MIX:2 * W_MIX] = yb.astype(BF16)

    o_ref[...] = (x + _dot(cat_ref[...], wout_ref[...])).reshape(t_len, n_seq, D_MODEL)
    for j in range(K_A - 1):
        sa_ref[:, j, :] = ua_ext[t_len + j]
    sb_ref[...] = ub_ext[t_len:t_len + K_B - 1]


def _seq_block_spec(rows, nb, width):
    return pl.BlockSpec((rows, nb, width), lambda i, *_: (0, i, 0))


def _seq_major_spec(rows, nb, width):
    return pl.BlockSpec((nb, rows, width), lambda i, *_: (i, 0, 0))


def _xs_block_spec(t_len, nb):
    return pl.BlockSpec((None, t_len, nb, D_MODEL), lambda i, *_: (i, 0, 0, 0))


def _even_sample(x, g, win, ca, cb, cbb, lng, lnb, wout, hist_a, hist_b, nb, x_seq_major):
    n_seq, t_len = hist_a.shape[0], x.size // (hist_a.shape[0] * D_MODEL)
    m = t_len * nb
    n_in = win.shape[1]
    x_spec = _seq_major_spec(t_len, nb, D_MODEL) if x_seq_major else _xs_block_spec(t_len, nb)
    return pl.pallas_call(
        functools.partial(_even_sample_kernel, t_len=t_len, n_seq=nb, x_seq_major=x_seq_major),
        grid=(n_seq // nb,),
        in_specs=[x_spec, _const_spec((1, D_MODEL)),
                  _const_spec((D_MODEL, n_in)),
                  _const_spec((K_A, W_MIX)), _const_spec((K_B, W_MIX)), _const_spec((1, W_MIX)),
                  _const_spec((1, W_MIX)), _const_spec((1, W_MIX)), _const_spec((2 * W_MIX, D_MODEL)),
                  _seq_major_spec(K_A - 1, nb, W_MIX), _seq_block_spec(K_B - 1, nb, W_MIX)],
        out_specs=[_xs_block_spec(t_len, nb), _seq_major_spec(K_A - 1, nb, W_MIX),
                   _seq_block_spec(K_B - 1, nb, W_MIX)],
        out_shape=[jax.ShapeDtypeStruct((n_seq // nb, t_len, nb, D_MODEL), F32),
                   jax.ShapeDtypeStruct((n_seq, K_A - 1, W_MIX), F32),
                   jax.ShapeDtypeStruct((K_B - 1, n_seq, W_MIX), F32)],
        scratch_shapes=[pltpu.VMEM((m, D_MODEL), BF16),
                        pltpu.VMEM((m, W_MIX), F32),
                        pltpu.VMEM((t_len + K_A - 1, nb, W_MIX), F32),
                        pltpu.VMEM((t_len + K_B - 1, nb, W_MIX), F32),
                        pltpu.VMEM((m, W_MIX), F32),
                        pltpu.VMEM((m, 2 * W_MIX), BF16)],
        compiler_params=_params(1),
        name="even_sample",
    )(x, g.reshape(1, D_MODEL), win, ca, cb, cbb.reshape(1, W_MIX), lng.reshape(1, W_MIX),
      lnb.reshape(1, W_MIX), wout, hist_a, hist_b)


def _odd_sample_kernel(sw_ref, sb_ref_smem,
                       x_ref, g_ref, win_ref, wpool_ref, pscale_ref, lng_ref, lnb_ref, wout_ref,
                       hp_ref,
                       o_ref, sp_ref, vn_out_ref,
                       hb_ref, p_ext, u_ref, vn_ref, d_ref, cat_ref, *, t_len, n_seq, start_pos):
    x = x_ref[...].reshape(t_len * n_seq, D_MODEL)
    hb_ref[...] = _rms(x, g_ref[...]).astype(BF16)

    def seg(i):
        return _dot(hb_ref[...], win_ref[:, i * W_MIX:(i + 1) * W_MIX])

    p_ext[0:POOL_HIST] = hp_ref[...]
    p_ext[POOL_HIST:POOL_HIST + t_len] = seg(0).reshape(t_len, n_seq, W_MIX)
    u_ref[...] = seg(1).reshape(t_len, n_seq, W_MIX)
    vn_ref[...] = _ln(seg(2), lng_ref[...], lnb_ref[...]).reshape(t_len, n_seq, W_MIX)

    for t in range(t_len):
        rows = slice(t * n_seq, (t + 1) * n_seq)
        for gi, w in enumerate(POOL_WINDOWS):
            lanes = slice(gi * GROUP_W, (gi + 1) * GROUP_W)
            tok = p_ext[POOL_HIST + t, :, lanes]
            wsum = tok
            for j in range(1, w):
                wsum = wsum + p_ext[POOL_HIST + t - j, :, lanes]
            cnt = float(min(start_pos + t + 1, w))
            d_ref[rows, lanes] = (wsum / cnt - tok).astype(BF16)
            mixed = jnp.zeros((n_seq, GROUP_W), F32) + sb_ref_smem[gi * t_len + t]
            for s in range(t + 1):
                mixed = mixed + sw_ref[(gi * t_len + t) * t_len + s] * vn_ref[s, :, lanes]
            cat_ref[rows, W_MIX + gi * GROUP_W:W_MIX + (gi + 1) * GROUP_W] = (
                u_ref[t, :, lanes] * mixed).astype(BF16)

    for gi in range(N_GROUPS):
        lanes = slice(gi * GROUP_W, (gi + 1) * GROUP_W)
        yc = _dot(d_ref[:, lanes], wpool_ref[gi]) * pscale_ref[:, lanes]
        cat_ref[:, lanes] = yc.astype(BF16)

    o_ref[...] = (x + _dot(cat_ref[...], wout_ref[...])).reshape(t_len, n_seq, D_MODEL)
    sp_ref[...] = p_ext[t_len:t_len + POOL_HIST]
    for t in range(t_len):
        vn_out_ref[:, t, :] = vn_ref[t]


def _odd_sample(x, g, win, wpool, pscale, sguw, sgub, lng, lnb, wout, hist_p, nb, start_pos):
    n_blocks, t_len, _, _ = x.shape
    n_seq = n_blocks * nb
    m = t_len * nb
    n_in = win.shape[1]
    assert t_len <= CHUNK
    sw = sguw[:, :t_len, :t_len].reshape(-1)
    sb = sgub[:, :t_len].reshape(-1)

    grid_spec = pltpu.PrefetchScalarGridSpec(
        num_scalar_prefetch=2,
        grid=(n_seq // nb,),
        in_specs=[_xs_block_spec(t_len, nb), _const_spec((1, D_MODEL)),
                  _const_spec((D_MODEL, n_in)),
                  _const_spec((N_GROUPS, GROUP_W, GROUP_W)), _const_spec((1, W_MIX)),
                  _const_spec((1, W_MIX)), _const_spec((1, W_MIX)),
                  _const_spec((2 * W_MIX, D_MODEL)),
                  _seq_block_spec(POOL_HIST, nb, W_MIX)],
        out_specs=[_xs_block_spec(t_len, nb), _seq_block_spec(POOL_HIST, nb, W_MIX),
                   _seq_major_spec(t_len, nb, W_MIX)],
        scratch_shapes=[pltpu.VMEM((m, D_MODEL), BF16),
                        pltpu.VMEM((t_len + POOL_HIST, nb, W_MIX), F32),
                        pltpu.VMEM((t_len, nb, W_MIX), F32),
                        pltpu.VMEM((t_len, nb, W_MIX), F32),
                        pltpu.VMEM((m, W_MIX), BF16),
                        pltpu.VMEM((m, 2 * W_MIX), BF16)],
    )
    return pl.pallas_call(
        functools.partial(_odd_sample_kernel, t_len=t_len, n_seq=nb, start_pos=start_pos),
        grid_spec=grid_spec,
        out_shape=[jax.ShapeDtypeStruct((n_blocks, t_len, nb, D_MODEL), F32),
                   jax.ShapeDtypeStruct((POOL_HIST, n_seq, W_MIX), F32),
                   jax.ShapeDtypeStruct((n_seq, t_len, W_MIX), F32)],
        compiler_params=_params(1),
        name="odd_sample",
    )(sw, sb, x, g.reshape(1, D_MODEL), win, wpool, pscale.reshape(1, W_MIX),
      lng.reshape(1, W_MIX), lnb.reshape(1, W_MIX), wout, hist_p)


def _time_major(a):
    return jnp.swapaxes(a, 0, 1)


def kernel(x_prompt, x_sample, state_conv_a, state_conv_b, state_pool, norm_mix, norm_ffn, ev_w_in,
           ev_conv_a, ev_conv_b, ev_conv_b_bias, ev_ln_g, ev_ln_b, ev_w_out, od_w_in, od_pool_w,
           od_pool_scale, od_sgu_w, od_sgu_b, od_sgu_ln_g, od_sgu_ln_b, od_w_out, ffn_w1, ffn_w2,
           norm_final):
    depth = norm_mix.shape[0]
    batch, seq, _ = x_prompt.shape
    n_seq, t_len, _ = x_sample.shape
    tm_mix = 1024
    tm_ffn = 512
    nb = tm_ffn // t_len

    def mixer_weights(l):
        i = l // 2
        return ((ev_w_in, i), (ev_w_out, i)) if l % 2 == 0 else ((od_w_in, i), (od_w_out, i))

    xp = x_prompt
    xs = x_sample
    sa_p, sa_s, sb_p, sb_s, sc_p, sc_s, vn_s = [], [], [], [], [], [], []
    win, wout = (w[i].astype(BF16) for w, i in mixer_weights(0))
    w1 = w2 = None
    for l in range(depth):
        i = l // 2
        ffn_f32 = ((ffn_w1, l), (ffn_w2, l)) if l == 0 else ()
        if l % 2 == 0:
            common = (norm_mix[l], win, ev_conv_a[i], ev_conv_b[i], ev_conv_b_bias[i], ev_ln_g[i],
                      ev_ln_b[i], wout)
            xp, a_p, b_p, *own_ffn = _even_prompt(xp, *common, tm=tm_mix, cast_next=ffn_f32)
            xs, a_s, b_s = _even_sample(xs, *common, state_conv_a[i], _time_major(state_conv_b[i]),
                                        nb, x_seq_major=(l == 0))
            sa_p.append(a_p)
            sb_p.append(b_p)
            sa_s.append(a_s)
            sb_s.append(_time_major(b_s))
        else:
            common = (norm_mix[l], win, od_pool_w[i].astype(BF16), od_pool_scale[i], od_sgu_w[i],
                      od_sgu_b[i], od_sgu_ln_g[i], od_sgu_ln_b[i], wout)
            xp, c_p, *own_ffn = _odd_prompt(xp, *common, tm=tm_mix, cast_next=ffn_f32)
            xs, c_s, v_s = _odd_sample(xs, *common, _time_major(state_pool[i]), nb, PAST_LEN)
            sc_p.append(c_p)
            sc_s.append(_time_major(c_s))
            vn_s.append(v_s)
        if own_ffn:
            w1, w2 = own_ffn
        last = l == depth - 1
        next_f32 = () if last else mixer_weights(l + 1) + ((ffn_w1, l + 1), (ffn_w2, l + 1))
        xp, xs, *next_w = _ffn(xp.reshape(batch * seq, D_MODEL), xs.reshape(t_len * n_seq, D_MODEL),
                               norm_ffn[l], w1, w2, norm_final if last else None, tm_ffn, t_len,
                               cast_next=next_f32)
        xp = xp.reshape(batch, seq, D_MODEL)
        if not last:
            xs = xs.reshape(n_seq // nb, t_len, nb, D_MODEL)
            win, wout, w1, w2 = next_w

    return (xp, xs, jnp.stack(sa_p), jnp.stack(sa_s), jnp.stack(sb_p), jnp.stack(sb_s),
            jnp.stack(sc_p), jnp.stack(sc_s), jnp.stack(vn_s))
```

```python
import functools

import jax
import jax.numpy as jnp
from jax import lax
from jax.experimental import pallas as pl
from jax.experimental.pallas import tpu as pltpu

F32 = jnp.float32
BF16 = jnp.bfloat16

D_MODEL = 1024
W_MIX = D_MODEL // 2
K_A = 3
K_B = 31
POOL_WINDOWS = (2, 4, 8, 16)
POOL_HIST = max(POOL_WINDOWS) - 1
CHUNK = 128
N_GROUPS = 4
GROUP_W = W_MIX // N_GROUPS
D_FF = 4 * D_MODEL
PAST_LEN = 16384
EPS = 1e-6

SUBLANES = 8
HIST_A = 8
HIST_B = 32
HIST_P = SUBLANES * len(POOL_WINDOWS)
ROW_CHUNK = 64
CONV_ROWS = 16
MIX_BLOCK_ROWS = 256
SAMPLE_CONV_STEPS = 4
FFN_COL_CHUNK = 1024
VMEM_LIMIT = 56 * 1024 * 1024


def _rms(x, g):
    return x * lax.rsqrt(jnp.mean(x * x, axis=-1, keepdims=True) + EPS) * g


def _ln(x, g, b):
    mu = jnp.mean(x, axis=-1, keepdims=True)
    xc = x - mu
    return xc * lax.rsqrt(jnp.mean(xc * xc, axis=-1, keepdims=True) + EPS) * g + b


def _dot(a, b):
    return jnp.dot(a, b, preferred_element_type=F32)


def _silu(x):
    return x * jax.nn.sigmoid(x)


def _const_spec(shape):
    nd = len(shape)
    return pl.BlockSpec(shape, lambda *_: (0,) * nd, pipeline_mode=pl.Buffered(1))


def _emit_interleaved(mxu_pieces, vpu_pieces):
    n_m, n_v = len(mxu_pieces), len(vpu_pieces)
    i = j = 0
    while i < n_m or j < n_v:
        if j >= n_v or (i < n_m and i * n_v <= j * n_m):
            mxu_pieces[i]()
            i += 1
        else:
            vpu_pieces[j]()
            j += 1


def _params(n_grid_axes):
    return pltpu.CompilerParams(
        dimension_semantics=("arbitrary",) * n_grid_axes,
        vmem_limit_bytes=VMEM_LIMIT,
    )


def _cast_plumbing(arrays, n_steps, step_of):
    in_specs, out_specs, shapes = [], [], []
    for stacked, layer in arrays:
        _, n_rows, n_cols = stacked.shape
        rows, rem = divmod(n_rows, n_steps)
        assert rem == 0 and rows % (2 * SUBLANES) == 0, stacked.shape
        in_specs.append(pl.BlockSpec((None, rows, n_cols),
                                     lambda *idx, _l=layer: (_l, step_of(*idx), 0)))
        out_specs.append(pl.BlockSpec((rows, n_cols), lambda *idx: (step_of(*idx), 0)))
        shapes.append(jax.ShapeDtypeStruct((n_rows, n_cols), BF16))
    return in_specs, out_specs, shapes


def _cast_rows(src_refs, dst_refs):
    for src, dst in zip(src_refs, dst_refs, strict=True):
        dst[...] = src[...].astype(BF16)


def _ffn_kernel(*refs, final, n_prompt, n_cast, t_len):
    n_in = 6 if final else 5
    xp_ref, xs_ref, g_ref, w1_ref, w2_ref = refs[:5]
    cast_in = refs[n_in:n_in + n_cast]
    op_ref, os_ref = refs[n_in + n_cast:n_in + n_cast + 2]
    cast_out = refs[n_in + n_cast + 2:n_in + 2 * n_cast + 2]
    hb_ref, act_ref, w1_vmem, w2_vmem, sem = refs[n_in + 2 * n_cast + 2:]
    step = pl.program_id(0)
    n_chunks = D_FF // FFN_COL_CHUNK

    def w1_copy(c):
        cols = slice(c * FFN_COL_CHUNK, (c + 1) * FFN_COL_CHUNK)
        return pltpu.make_async_copy(w1_ref.at[:, cols], w1_vmem.at[:, cols], sem.at[c])

    def w2_copy():
        return pltpu.make_async_copy(w2_ref, w2_vmem, sem.at[n_chunks])

    def rows_block(x_ref, wait_for_weights=False):
        hb_ref[...] = _rms(x_ref[...], g_ref[...]).astype(BF16)
        for c in range(n_chunks):
            cols = slice(c * FFN_COL_CHUNK, (c + 1) * FFN_COL_CHUNK)
            if wait_for_weights:
                w1_copy(c).wait()
            a = _dot(hb_ref[...], w1_vmem[:, cols])
            act_ref[:, cols] = jnp.square(jnp.maximum(a, 0.0)).astype(BF16)
        if wait_for_weights:
            w2_copy().wait()
        y = x_ref[...] + _dot(act_ref[...], w2_vmem[...])
        return _rms(y, refs[5][...]) if final else y

    @pl.when(step == 0)
    def _():
        for c in range(n_chunks):
            w1_copy(c).start()
        w2_copy().start()
        op_ref[...] = rows_block(xp_ref, wait_for_weights=True)
        _cast_rows(cast_in, cast_out)

    @pl.when(jnp.logical_and(step > 0, step < n_prompt))
    def _():
        op_ref[...] = rows_block(xp_ref)
        _cast_rows(cast_in, cast_out)

    @pl.when(step >= n_prompt)
    def _():
        y = rows_block(xs_ref)
        if final:
            nb = y.shape[0] // t_len
            for t in range(t_len):
                os_ref[:, t, :] = y[t * nb:(t + 1) * nb]
        else:
            os_ref[...] = y


def _ffn(xp2d, xs2d, g, w1, w2, g_final, tm, t_len, cast_next=()):
    n_prompt, n_sample = xp2d.shape[0] // tm, xs2d.shape[0] // tm
    final = g_final is not None
    last = n_prompt - 1
    p_spec = pl.BlockSpec((tm, D_MODEL), lambda i: (jnp.minimum(i, last), 0))
    s_spec = pl.BlockSpec((tm, D_MODEL), lambda i: (jnp.maximum(i - n_prompt, 0), 0))
    hbm_spec = pl.BlockSpec(memory_space=pl.ANY)
    in_specs = [p_spec, s_spec, _const_spec((1, D_MODEL)), hbm_spec, hbm_spec]
    args = [xp2d, xs2d, g.reshape(1, D_MODEL), w1, w2]
    s_out_spec, s_out_shape = s_spec, jax.ShapeDtypeStruct(xs2d.shape, F32)
    if final:
        in_specs.append(_const_spec((1, D_MODEL)))
        args.append(g_final.reshape(1, D_MODEL))
        nb = tm // t_len
        s_out_spec = pl.BlockSpec((nb, t_len, D_MODEL), lambda i: (jnp.maximum(i - n_prompt, 0), 0, 0))
        s_out_shape = jax.ShapeDtypeStruct((xs2d.shape[0] // t_len, t_len, D_MODEL), F32)
    c_in, c_out, c_shapes = _cast_plumbing(cast_next, n_prompt, lambda i: jnp.minimum(i, last))
    return pl.pallas_call(
        functools.partial(_ffn_kernel, final=final, n_prompt=n_prompt, n_cast=len(cast_next),
                          t_len=t_len),
        grid=(n_prompt + n_sample,),
        in_specs=in_specs + c_in,
        out_specs=[p_spec, s_out_spec] + c_out,
        out_shape=[jax.ShapeDtypeStruct(xp2d.shape, F32), s_out_shape] + c_shapes,
        scratch_shapes=[pltpu.VMEM((tm, D_MODEL), BF16), pltpu.VMEM((tm, D_FF), BF16),
                        pltpu.VMEM((D_MODEL, D_FF), BF16), pltpu.VMEM((D_FF, D_MODEL), BF16),
                        pltpu.SemaphoreType.DMA((D_FF // FFN_COL_CHUNK + 1,))],
        compiler_params=_params(1),
        name="ffn_final" if final else "ffn",
    )(*args, *(a for a, _ in cast_next))


def _causal_dwconv(ext_ref, gi, w_ref, n_taps, hist, r0, rows):
    lanes = slice(gi * GROUP_W, (gi + 1) * GROUP_W)
    acc = None
    for k in range(n_taps):
        lo = hist + r0 - (n_taps - 1) + k
        term = w_ref[k:k + 1, lanes] * ext_ref[gi, lo:lo + rows, :]
        acc = term if acc is None else acc + term
    return acc


def _even_prompt_kernel(*refs, tm, n_cast):
    x_ref, g_ref, win_ref, ca_ref, cb_ref, cbb_ref, lng_ref, lnb_ref, wout_ref = refs[:9]
    cast_in = refs[9:9 + n_cast]
    o_ref, sa_ref, sb_ref = refs[9 + n_cast:12 + n_cast]
    cast_out = refs[12 + n_cast:12 + 2 * n_cast]
    hb_ref, z_ref, ua_ext, ub_ext, cbv_ref, cat_ref = refs[12 + 2 * n_cast:]
    s = pl.program_id(1)
    _cast_rows(cast_in, cast_out)

    @pl.when(s == 0)
    def _():
        ua_ext[:, 0:HIST_A, :] = jnp.zeros((N_GROUPS, HIST_A, GROUP_W), F32)
        ub_ext[:, 0:HIST_B, :] = jnp.zeros((N_GROUPS, HIST_B, GROUP_W), F32)

    @pl.when(s > 0)
    def _():
        ua_ext[:, 0:HIST_A, :] = ua_ext[:, tm:tm + HIST_A, :]
        ub_ext[:, 0:HIST_B, :] = ub_ext[:, tm:tm + HIST_B, :]

    def zseg(rows, i, lanes=slice(0, W_MIX)):
        return z_ref[rows, i * W_MIX + lanes.start:i * W_MIX + lanes.stop]

    def project_pieces(rows):
        def norm():
            hb_ref[rows, :] = _rms(x_ref[rows, :], g_ref[...]).astype(BF16)

        def dot_piece(i):
            cols = slice(i * W_MIX, (i + 1) * W_MIX)
            z_ref[rows, cols] = _dot(hb_ref[rows, :], win_ref[:, cols])

        return [norm] + [functools.partial(dot_piece, i) for i in (3, 4, 1, 2, 0)]

    def mix_pieces(rows):
        def gates(r0):
            crow = slice(r0, r0 + ROW_CHUNK)
            for gi in range(N_GROUPS):
                lanes = slice(gi * GROUP_W, (gi + 1) * GROUP_W)
                ua_ext[gi, HIST_A + r0:HIST_A + r0 + ROW_CHUNK, :] = (
                    zseg(crow, 1, lanes) * zseg(crow, 2, lanes))
                ub_ext[gi, HIST_B + r0:HIST_B + r0 + ROW_CHUNK, :] = (
                    zseg(crow, 3, lanes) * jax.nn.sigmoid(zseg(crow, 4, lanes)))

        def convs(r0, gi):
            crow = slice(r0, r0 + CONV_ROWS)
            lanes = slice(gi * GROUP_W, (gi + 1) * GROUP_W)
            ya = zseg(crow, 0, lanes) * _causal_dwconv(ua_ext, gi, ca_ref, K_A, HIST_A, r0, CONV_ROWS)
            cat_ref[crow, lanes] = ya.astype(BF16)
            cbv_ref[crow, lanes] = _causal_dwconv(ub_ext, gi, cb_ref, K_B, HIST_B, r0, CONV_ROWS)

        def norm_act(r0):
            crow = slice(r0, r0 + CONV_ROWS)
            yb = _silu(_ln(cbv_ref[crow, :] + cbb_ref[...], lng_ref[...], lnb_ref[...]))
            cat_ref[crow, W_MIX:2 * W_MIX] = yb.astype(BF16)

        pieces = [functools.partial(gates, r0) for r0 in range(rows.start, rows.stop, ROW_CHUNK)]
        for r0 in range(rows.start, rows.stop, CONV_ROWS):
            pieces += [functools.partial(convs, r0, gi) for gi in range(N_GROUPS)]
            pieces.append(functools.partial(norm_act, r0))
        return pieces

    def out_pieces(rows):
        def piece(j):
            cols = slice(j * W_MIX, (j + 1) * W_MIX)
            o_ref[rows, cols] = x_ref[rows, cols] + _dot(cat_ref[rows, :], wout_ref[:, cols])

        return [functools.partial(piece, j) for j in range(D_MODEL // W_MIX)]

    blocks = [slice(r, r + MIX_BLOCK_ROWS) for r in range(0, tm, MIX_BLOCK_ROWS)]
    _emit_interleaved(project_pieces(blocks[0]), [])
    for i in range(len(blocks)):
        mxu = out_pieces(blocks[i - 1]) if i > 0 else []
        if i + 1 < len(blocks):
            mxu = project_pieces(blocks[i + 1]) + mxu
        _emit_interleaved(mxu, mix_pieces(blocks[i]))
    _emit_interleaved(out_pieces(blocks[-1]), [])

    @pl.when(s == pl.num_programs(1) - 1)
    def _():
        for gi in range(N_GROUPS):
            lanes = slice(gi * GROUP_W, (gi + 1) * GROUP_W)
            sa_ref[:, lanes] = ua_ext[gi, HIST_A + tm - (K_A - 1):HIST_A + tm, :]
            sb_ref[:, lanes] = ub_ext[gi, HIST_B + tm - (K_B - 1):HIST_B + tm, :]


def _even_prompt(x, g, win, ca, cb, cbb, lng, lnb, wout, tm, cast_next=()):
    b, s, _ = x.shape
    n_s = s // tm
    tile = pl.BlockSpec((None, tm, D_MODEL), lambda i, j: (i, j, 0))
    n_in = win.shape[1]
    c_in, c_out, c_shapes = _cast_plumbing(cast_next, b * n_s, lambda i, j: i * n_s + j)
    return pl.pallas_call(
        functools.partial(_even_prompt_kernel, tm=tm, n_cast=len(cast_next)),
        grid=(b, n_s),
        in_specs=[tile, _const_spec((1, D_MODEL)), _const_spec((D_MODEL, n_in)),
                  _const_spec((K_A, W_MIX)), _const_spec((K_B, W_MIX)), _const_spec((1, W_MIX)),
                  _const_spec((1, W_MIX)), _const_spec((1, W_MIX)),
                  _const_spec((2 * W_MIX, D_MODEL))] + c_in,
        out_specs=[tile,
                   pl.BlockSpec((None, K_A - 1, W_MIX), lambda i, j: (i, 0, 0)),
                   pl.BlockSpec((None, K_B - 1, W_MIX), lambda i, j: (i, 0, 0))] + c_out,
        out_shape=[jax.ShapeDtypeStruct((b, s, D_MODEL), F32),
                   jax.ShapeDtypeStruct((b, K_A - 1, W_MIX), F32),
                   jax.ShapeDtypeStruct((b, K_B - 1, W_MIX), F32)] + c_shapes,
        scratch_shapes=[pltpu.VMEM((tm, D_MODEL), BF16),
                        pltpu.VMEM((tm, n_in), F32),
                        pltpu.VMEM((N_GROUPS, tm + HIST_A, GROUP_W), F32),
                        pltpu.VMEM((N_GROUPS, tm + HIST_B, GROUP_W), F32),
                        pltpu.VMEM((tm, W_MIX), F32),
                        pltpu.VMEM((tm, 2 * W_MIX), BF16)],
        compiler_params=_params(2),
        name="even_prompt",
    )(x, g.reshape(1, D_MODEL), win, ca, cb, cbb.reshape(1, W_MIX), lng.reshape(1, W_MIX),
      lnb.reshape(1, W_MIX), wout, *(a for a, _ in cast_next))


def _tril_bf16(w):
    t = lax.broadcasted_iota(jnp.int32, (CHUNK, CHUNK), 0)
    s = lax.broadcasted_iota(jnp.int32, (CHUNK, CHUNK), 1)
    return jnp.where(s <= t, w, 0.0).astype(BF16)


def _odd_prompt_kernel(*refs, tm, n_cast):
    (x_ref, g_ref, win_ref, wpool_ref, pscale_ref, sguw_ref, sgubt_ref, lng_ref, lnb_ref,
     wout_ref) = refs[:10]
    cast_in = refs[10:10 + n_cast]
    o_ref, sp_ref = refs[10 + n_cast:12 + n_cast]
    cast_out = refs[12 + n_cast:12 + 2 * n_cast]
    hb_ref, p_ext, ps_ref, inv_ref, u_ref, vn_ref, d_ref, cat_ref, wm_ref = refs[12 + 2 * n_cast:]
    s = pl.program_id(1)
    _cast_rows(cast_in, cast_out)

    @pl.when(jnp.logical_and(pl.program_id(0) == 0, s == 0))
    def _():
        for gi in range(N_GROUPS):
            wm_ref[gi] = _tril_bf16(sguw_ref[gi])

    @pl.when(s == 0)
    def _():
        p_ext[:, 0:HIST_P, :] = jnp.zeros((N_GROUPS, HIST_P, GROUP_W), F32)

    @pl.when(s > 0)
    def _():
        p_ext[:, 0:HIST_P, :] = p_ext[:, tm:tm + HIST_P, :]

    assert ROW_CHUNK >= max(POOL_WINDOWS)
    pos0 = s * tm + lax.broadcasted_iota(jnp.int32, (ROW_CHUNK, GROUP_W), 0)
    for gi, w in enumerate(POOL_WINDOWS):
        inv_ref[gi] = 1.0 / jnp.minimum(pos0 + 1, w).astype(F32)

    def project_pieces(rows):
        def seg(i):
            return _dot(hb_ref[rows, :], win_ref[:, i * W_MIX:(i + 1) * W_MIX])

        def norm():
            hb_ref[rows, :] = _rms(x_ref[rows, :], g_ref[...]).astype(BF16)

        def pooled():
            pc = seg(0)
            for gi in range(N_GROUPS):
                p_ext[gi, HIST_P + rows.start:HIST_P + rows.stop, :] = pc[:, gi * GROUP_W:(gi + 1) * GROUP_W]

        def gate():
            u_ref[rows, :] = seg(1)

        def normed():
            vn_ref[rows, :] = _ln(seg(2), lng_ref[...], lnb_ref[...]).astype(BF16)

        return [norm, pooled, gate, normed]

    def pool_chunk(r0):
        n = ROW_CHUNK + HIST_P
        for gi, w in enumerate(POOL_WINDOWS):
            win = p_ext.at[gi, r0:r0 + n, :]
            scratch = ps_ref.at[r0 // ROW_CHUNK % 2, gi]
            prev, lo, step = win, 0, 1
            sums = win[0:n, :]
            while step < w:
                lo += SUBLANES
                sums = sums[SUBLANES:] + prev[lo - step:n - step, :]
                step *= 2
                if step < w:
                    scratch[lo:n, :] = sums
                    prev = scratch
            tok = win[HIST_P:n, :]
            inv = inv_ref[gi] if r0 == 0 else 1.0 / w
            lanes = slice(gi * GROUP_W, (gi + 1) * GROUP_W)
            d_ref[r0:r0 + ROW_CHUNK, lanes] = (sums[HIST_P - lo:] * inv - tok).astype(BF16)

    def pool_project(rows):
        for gi in range(N_GROUPS):
            lanes = slice(gi * GROUP_W, (gi + 1) * GROUP_W)
            yc = _dot(d_ref[rows, lanes], wpool_ref[gi]) * pscale_ref[:, lanes]
            cat_ref[rows, lanes] = yc.astype(BF16)

    def sgu_chunk(r0):
        rows = slice(r0, r0 + CHUNK)
        for gi in range(N_GROUPS):
            lanes = slice(gi * GROUP_W, (gi + 1) * GROUP_W)
            mixed = _dot(wm_ref[gi], vn_ref[rows, lanes]) + sgubt_ref[:, gi:gi + 1]
            yd = u_ref[rows, lanes] * mixed
            cat_ref[rows, W_MIX + gi * GROUP_W:W_MIX + (gi + 1) * GROUP_W] = yd.astype(BF16)

    def mix_pieces(rows):
        pieces = [functools.partial(pool_chunk, r0) for r0 in range(rows.start, rows.stop, ROW_CHUNK)]
        pieces.append(functools.partial(pool_project, rows))
        pieces += [functools.partial(sgu_chunk, r0) for r0 in range(rows.start, rows.stop, CHUNK)]
        return pieces

    def out_pieces(rows):
        def piece():
            o_ref[rows, :] = x_ref[rows, :] + _dot(cat_ref[rows, :], wout_ref[...])

        return [piece]

    blocks = [slice(r, r + MIX_BLOCK_ROWS) for r in range(0, tm, MIX_BLOCK_ROWS)]
    _emit_interleaved(project_pieces(blocks[0]), [])
    for i in range(len(blocks)):
        mxu = out_pieces(blocks[i - 1]) if i > 0 else []
        if i + 1 < len(blocks):
            mxu = project_pieces(blocks[i + 1]) + mxu
        _emit_interleaved(mxu, mix_pieces(blocks[i]))
    _emit_interleaved(out_pieces(blocks[-1]), [])

    @pl.when(s == pl.num_programs(1) - 1)
    def _():
        for gi in range(N_GROUPS):
            sp_ref[:, gi * GROUP_W:(gi + 1) * GROUP_W] = p_ext[gi, HIST_P + tm - POOL_HIST:HIST_P + tm, :]


def _odd_prompt(x, g, win, wpool, pscale, sguw, sgub, lng, lnb, wout, tm, cast_next=()):
    b, s, _ = x.shape
    n_s = s // tm
    tile = pl.BlockSpec((None, tm, D_MODEL), lambda i, j: (i, j, 0))
    n_in = win.shape[1]
    c_in, c_out, c_shapes = _cast_plumbing(cast_next, b * n_s, lambda i, j: i * n_s + j)
    return pl.pallas_call(
        functools.partial(_odd_prompt_kernel, tm=tm, n_cast=len(cast_next)),
        grid=(b, n_s),
        in_specs=[tile, _const_spec((1, D_MODEL)), _const_spec((D_MODEL, n_in)),
                  _const_spec((N_GROUPS, GROUP_W, GROUP_W)), _const_spec((1, W_MIX)),
                  _const_spec((N_GROUPS, CHUNK, CHUNK)), _const_spec((CHUNK, N_GROUPS)),
                  _const_spec((1, W_MIX)), _const_spec((1, W_MIX)),
                  _const_spec((2 * W_MIX, D_MODEL))] + c_in,
        out_specs=[tile, pl.BlockSpec((None, POOL_HIST, W_MIX), lambda i, j: (i, 0, 0))] + c_out,
        out_shape=[jax.ShapeDtypeStruct((b, s, D_MODEL), F32),
                   jax.ShapeDtypeStruct((b, POOL_HIST, W_MIX), F32)] + c_shapes,
        scratch_shapes=[pltpu.VMEM((tm, D_MODEL), BF16),
                        pltpu.VMEM((N_GROUPS, tm + HIST_P, GROUP_W), F32),
                        pltpu.VMEM((2, N_GROUPS, ROW_CHUNK + HIST_P, GROUP_W), F32),
                        pltpu.VMEM((len(POOL_WINDOWS), ROW_CHUNK, GROUP_W), F32),
                        pltpu.VMEM((tm, W_MIX), F32),
                        pltpu.VMEM((tm, W_MIX), BF16),
                        pltpu.VMEM((tm, W_MIX), BF16),
                        pltpu.VMEM((tm, 2 * W_MIX), BF16),
                        pltpu.VMEM((N_GROUPS, CHUNK, CHUNK), BF16)],
        compiler_params=_params(2),
        name="odd_prompt",
    )(x, g.reshape(1, D_MODEL), win, wpool, pscale.reshape(1, W_MIX), sguw, sgub.T,
      lng.reshape(1, W_MIX), lnb.reshape(1, W_MIX), wout, *(a for a, _ in cast_next))


def _even_sample_kernel(x_ref, g_ref, win_ref, ca_ref, cb_ref, cbb_ref, lng_ref, lnb_ref, wout_ref,
                        ha_ref, hbst_ref,
                        o_ref, sa_ref, sb_ref,
                        hb_ref, gate_ref, ua_ext, ub_ext, cbv_ref, cat_ref, *, t_len, n_seq,
                        x_seq_major):
    if x_seq_major:
        x = jnp.concatenate([x_ref[:, t, :] for t in range(t_len)], axis=0)
    else:
        x = x_ref[...].reshape(t_len * n_seq, D_MODEL)
    hb_ref[...] = _rms(x, g_ref[...]).astype(BF16)

    def seg(i):
        return _dot(hb_ref[...], win_ref[:, i * W_MIX:(i + 1) * W_MIX])

    gate_ref[...] = seg(0)
    for j in range(K_A - 1):
        ua_ext[j] = ha_ref[:, j, :]
    ub_ext[0:K_B - 1] = hbst_ref[...]
    ua_ext[K_A - 1:K_A - 1 + t_len] = (seg(1) * seg(2)).reshape(t_len, n_seq, W_MIX)
    ub_ext[K_B - 1:K_B - 1 + t_len] = (seg(3) * jax.nn.sigmoid(seg(4))).reshape(t_len, n_seq, W_MIX)

    def taps(ext_ref, w_ref, n_taps, t0, lanes):
        acc = None
        for k in range(n_taps):
            term = w_ref[k:k + 1, lanes] * ext_ref[t0 + k:t0 + k + SAMPLE_CONV_STEPS, :, lanes]
            acc = term if acc is None else acc + term
        return acc.reshape(SAMPLE_CONV_STEPS * n_seq, GROUP_W)

    for t0 in range(0, t_len, SAMPLE_CONV_STEPS):
        rows = slice(t0 * n_seq, (t0 + SAMPLE_CONV_STEPS) * n_seq)
        for gi in range(N_GROUPS):
            lanes = slice(gi * GROUP_W, (gi + 1) * GROUP_W)
            cat_ref[rows, lanes] = (gate_ref[rows, lanes] * taps(ua_ext, ca_ref, K_A, t0, lanes)).astype(BF16)
            cbv_ref[rows, lanes] = taps(ub_ext, cb_ref, K_B, t0, lanes) + cbb_ref[:, lanes]
        yb = _silu(_ln(cbv_ref[rows, :], lng_ref[...], lnb_ref[...]))
        cat_ref[rows, W_MIX:2 * W_MIX] = yb.astype(BF16)

    o_ref[...] = (x + _dot(cat_ref[...], wout_ref[...])).reshape(t_len, n_seq, D_MODEL)
    for j in range(K_A - 1):
        sa_ref[:, j, :] = ua_ext[t_len + j]
    sb_ref[...] = ub_ext[t_len:t_len + K_B - 1]


def _seq_block_spec(rows, nb, width):
    return pl.BlockSpec((rows, nb, width), lambda i, *_: (0, i, 0))


def _seq_major_spec(rows, nb, width):
    return pl.BlockSpec((nb, rows, width), lambda i, *_: (i, 0, 0))


def _xs_block_spec(t_len, nb):
    return pl.BlockSpec((None, t_len, nb, D_MODEL), lambda i, *_: (i, 0, 0, 0))


def _even_sample(x, g, win, ca, cb, cbb, lng, lnb, wout, hist_a, hist_b, nb, x_seq_major):
    n_seq, t_len = hist_a.shape[0], x.size // (hist_a.shape[0] * D_MODEL)
    m = t_len * nb
    n_in = win.shape[1]
    x_spec = _seq_major_spec(t_len, nb, D_MODEL) if x_seq_major else _xs_block_spec(t_len, nb)
    return pl.pallas_call(
        functools.partial(_even_sample_kernel, t_len=t_len, n_seq=nb, x_seq_major=x_seq_major),
        grid=(n_seq // nb,),
        in_specs=[x_spec, _const_spec((1, D_MODEL)),
                  _const_spec((D_MODEL, n_in)),
                  _const_spec((K_A, W_MIX)), _const_spec((K_B, W_MIX)), _const_spec((1, W_MIX)),
                  _const_spec((1, W_MIX)), _const_spec((1, W_MIX)), _const_spec((2 * W_MIX, D_MODEL)),
                  _seq_major_spec(K_A - 1, nb, W_MIX), _seq_block_spec(K_B - 1, nb, W_MIX)],
        out_specs=[_xs_block_spec(t_len, nb), _seq_major_spec(K_A - 1, nb, W_MIX),
                   _seq_block_spec(K_B - 1, nb, W_MIX)],
        out_shape=[jax.ShapeDtypeStruct((n_seq // nb, t_len, nb, D_MODEL), F32),
                   jax.ShapeDtypeStruct((n_seq, K_A - 1, W_MIX), F32),
                   jax.ShapeDtypeStruct((K_B - 1, n_seq, W_MIX), F32)],
        scratch_shapes=[pltpu.VMEM((m, D_MODEL), BF16),
                        pltpu.VMEM((m, W_MIX), F32),
                        pltpu.VMEM((t_len + K_A - 1, nb, W_MIX), F32),
                        pltpu.VMEM((t_len + K_B - 1, nb, W_MIX), F32),
                        pltpu.VMEM((m, W_MIX), F32),
                        pltpu.VMEM((m, 2 * W_MIX), BF16)],
        compiler_params=_params(1),
        name="even_sample",
    )(x, g.reshape(1, D_MODEL), win, ca, cb, cbb.reshape(1, W_MIX), lng.reshape(1, W_MIX),
      lnb.reshape(1, W_MIX), wout, hist_a, hist_b)


def _odd_sample_kernel(sw_ref, sb_ref_smem,
                       x_ref, g_ref, win_ref, wpool_ref, pscale_ref, lng_ref, lnb_ref, wout_ref,
                       hp_ref,
                       o_ref, sp_ref, vn_out_ref,
                       hb_ref, p_ext, u_ref, vn_ref, d_ref, cat_ref, *, t_len, n_seq, start_pos):
    x = x_ref[...].reshape(t_len * n_seq, D_MODEL)
    hb_ref[...] = _rms(x, g_ref[...]).astype(BF16)

    def seg(i):
        return _dot(hb_ref[...], win_ref[:, i * W_MIX:(i + 1) * W_MIX])

    p_ext[0:POOL_HIST] = hp_ref[...]
    p_ext[POOL_HIST:POOL_HIST + t_len] = seg(0).reshape(t_len, n_seq, W_MIX)
    u_ref[...] = seg(1).reshape(t_len, n_seq, W_MIX)
    vn_ref[...] = _ln(seg(2), lng_ref[...], lnb_ref[...]).reshape(t_len, n_seq, W_MIX)

    for t in range(t_len):
        rows = slice(t * n_seq, (t + 1) * n_seq)
        for gi, w in enumerate(POOL_WINDOWS):
            lanes = slice(gi * GROUP_W, (gi + 1) * GROUP_W)
            tok = p_ext[POOL_HIST + t, :, lanes]
            wsum = tok
            for j in range(1, w):
                wsum = wsum + p_ext[POOL_HIST + t - j, :, lanes]
            cnt = float(min(start_pos + t + 1, w))
            d_ref[rows, lanes] = (wsum / cnt - tok).astype(BF16)
            mixed = jnp.zeros((n_seq, GROUP_W), F32) + sb_ref_smem[gi * t_len + t]
            for s in range(t + 1):
                mixed = mixed + sw_ref[(gi * t_len + t) * t_len + s] * vn_ref[s, :, lanes]
            cat_ref[rows, W_MIX + gi * GROUP_W:W_MIX + (gi + 1) * GROUP_W] = (
                u_ref[t, :, lanes] * mixed).astype(BF16)

    for gi in range(N_GROUPS):
        lanes = slice(gi * GROUP_W, (gi + 1) * GROUP_W)
        yc = _dot(d_ref[:, lanes], wpool_ref[gi]) * pscale_ref[:, lanes]
        cat_ref[:, lanes] = yc.astype(BF16)

    o_ref[...] = (x + _dot(cat_ref[...], wout_ref[...])).reshape(t_len, n_seq, D_MODEL)
    sp_ref[...] = p_ext[t_len:t_len + POOL_HIST]
    for t in range(t_len):
        vn_out_ref[:, t, :] = vn_ref[t]


def _odd_sample(x, g, win, wpool, pscale, sguw, sgub, lng, lnb, wout, hist_p, nb, start_pos):
    n_blocks, t_len, _, _ = x.shape
    n_seq = n_blocks * nb
    m = t_len * nb
    n_in = win.shape[1]
    assert t_len <= CHUNK
    sw = sguw[:, :t_len, :t_len].reshape(-1)
    sb = sgub[:, :t_len].reshape(-1)

    grid_spec = pltpu.PrefetchScalarGridSpec(
        num_scalar_prefetch=2,
        grid=(n_seq // nb,),
        in_specs=[_xs_block_spec(t_len, nb), _const_spec((1, D_MODEL)),
                  _const_spec((D_MODEL, n_in)),
                  _const_spec((N_GROUPS, GROUP_W, GROUP_W)), _const_spec((1, W_MIX)),
                  _const_spec((1, W_MIX)), _const_spec((1, W_MIX)),
                  _const_spec((2 * W_MIX, D_MODEL)),
                  _seq_block_spec(POOL_HIST, nb, W_MIX)],
        out_specs=[_xs_block_spec(t_len, nb), _seq_block_spec(POOL_HIST, nb, W_MIX),
                   _seq_major_spec(t_len, nb, W_MIX)],
        scratch_shapes=[pltpu.VMEM((m, D_MODEL), BF16),
                        pltpu.VMEM((t_len + POOL_HIST, nb, W_MIX), F32),
                        pltpu.VMEM((t_len, nb, W_MIX), F32),
                        pltpu.VMEM((t_len, nb, W_MIX), F32),
                        pltpu.VMEM((m, W_MIX), BF16),
                        pltpu.VMEM((m, 2 * W_MIX), BF16)],
    )
    return pl.pallas_call(
        functools.partial(_odd_sample_kernel, t_len=t_len, n_seq=nb, start_pos=start_pos),
        grid_spec=grid_spec,
        out_shape=[jax.ShapeDtypeStruct((n_blocks, t_len, nb, D_MODEL), F32),
                   jax.ShapeDtypeStruct((POOL_HIST, n_seq, W_MIX), F32),
                   jax.ShapeDtypeStruct((n_seq, t_len, W_MIX), F32)],
        compiler_params=_params(1),
        name="odd_sample",
    )(sw, sb, x, g.reshape(1, D_MODEL), win, wpool, pscale.reshape(1, W_MIX),
      lng.reshape(1, W_MIX), lnb.reshape(1, W_MIX), wout, hist_p)


def _time_major(a):
    return jnp.swapaxes(a, 0, 1)


def kernel(x_prompt, x_sample, state_conv_a, state_conv_b, state_pool, norm_mix, norm_ffn, ev_w_in,
           ev_conv_a, ev_conv_b, ev_conv_b_bias, ev_ln_g, ev_ln_b, ev_w_out, od_w_in, od_pool_w,
           od_pool_scale, od_sgu_w, od_sgu_b, od_sgu_ln_g, od_sgu_ln_b, od_w_out, ffn_w1, ffn_w2,
           norm_final):
    depth = norm_mix.shape[0]
    batch, seq, _ = x_prompt.shape
    n_seq, t_len, _ = x_sample.shape
    tm_mix = 1024
    tm_ffn = 512
    nb = tm_ffn // t_len

    def mixer_weights(l):
        i = l // 2
        return ((ev_w_in, i), (ev_w_out, i)) if l % 2 == 0 else ((od_w_in, i), (od_w_out, i))

    xp = x_prompt
    xs = x_sample
    sa_p, sa_s, sb_p, sb_s, sc_p, sc_s, vn_s = [], [], [], [], [], [], []
    win, wout = (w[i].astype(BF16) for w, i in mixer_weights(0))
    w1 = w2 = None
    for l in range(depth):
        i = l // 2
        ffn_f32 = ((ffn_w1, l), (ffn_w2, l)) if l == 0 else ()
        if l % 2 == 0:
            common = (norm_mix[l], win, ev_conv_a[i], ev_conv_b[i], ev_conv_b_bias[i], ev_ln_g[i],
                      ev_ln_b[i], wout)
            xp, a_p, b_p, *own_ffn = _even_prompt(xp, *common, tm=tm_mix, cast_next=ffn_f32)
            xs, a_s, b_s = _even_sample(xs, *common, state_conv_a[i], _time_major(state_conv_b[i]),
                                        nb, x_seq_major=(l == 0))
            sa_p.append(a_p)
            sb_p.append(b_p)
            sa_s.append(a_s)
            sb_s.append(_time_major(b_s))
        else:
            common = (norm_mix[l], win, od_pool_w[i].astype(BF16), od_pool_scale[i], od_sgu_w[i],
                      od_sgu_b[i], od_sgu_ln_g[i], od_sgu_ln_b[i], wout)
            xp, c_p, *own_ffn = _odd_prompt(xp, *common, tm=tm_mix, cast_next=ffn_f32)
            xs, c_s, v_s = _odd_sample(xs, *common, _time_major(state_pool[i]), nb, PAST_LEN)
            sc_p.append(c_p)
            sc_s.append(_time_major(c_s))
            vn_s.append(v_s)
        if own_ffn:
            w1, w2 = own_ffn
        last = l == depth - 1
        next_f32 = () if last else mixer_weights(l + 1) + ((ffn_w1, l + 1), (ffn_w2, l + 1))
        xp, xs, *next_w = _ffn(xp.reshape(batch * seq, D_MODEL), xs.reshape(t_len * n_seq, D_MODEL),
                               norm_ffn[l], w1, w2, norm_final if last else None, tm_ffn, t_len,
                               cast_next=next_f32)
        xp = xp.reshape(batch, seq, D_MODEL)
        if not last:
            xs = xs.reshape(n_seq // nb, t_len, nb, D_MODEL)
            win, wout, w1, w2 = next_w

    return (xp, xs, jnp.stack(sa_p), jnp.stack(sa_s), jnp.stack(sb_p), jnp.stack(sb_s),
            jnp.stack(sc_p), jnp.stack(sc_s), jnp.stack(vn_s))
```

```python
import functools

import jax
import jax.numpy as jnp
from jax import lax
from jax.experimental import pallas as pl
from jax.experimental.pallas import tpu as pltpu

F32 = jnp.float32
BF16 = jnp.bfloat16

D_MODEL = 1024
W_MIX = D_MODEL // 2
K_A = 3
K_B = 31
POOL_WINDOWS = (2, 4, 8, 16)
POOL_HIST = max(POOL_WINDOWS) - 1
CHUNK = 128
N_GROUPS = 4
GROUP_W = W_MIX // N_GROUPS
D_FF = 4 * D_MODEL
PAST_LEN = 16384
EPS = 1e-6

SUBLANES = 8
HIST_A = 8
HIST_B = 32
HIST_P = SUBLANES * len(POOL_WINDOWS)
ROW_CHUNK = 64
CONV_ROWS = 16
MIX_BLOCK_ROWS = 256
SAMPLE_CONV_STEPS = 4
FFN_COL_CHUNK = 1024
VMEM_LIMIT = 56 * 1024 * 1024


def _rms(x, g):
    return x * lax.rsqrt(jnp.mean(x * x, axis=-1, keepdims=True) + EPS) * g


def _ln(x, g, b):
    mu = jnp.mean(x, axis=-1, keepdims=True)
    xc = x - mu
    return xc * lax.rsqrt(jnp.mean(xc * xc, axis=-1, keepdims=True) + EPS) * g + b


def _dot(a, b):
    return jnp.dot(a, b, preferred_element_type=F32)


def _silu(x):
    return x * jax.nn.sigmoid(x)


def _const_spec(shape):
    nd = len(shape)
    return pl.BlockSpec(shape, lambda *_: (0,) * nd, pipeline_mode=pl.Buffered(1))


def _emit_interleaved(mxu_pieces, vpu_pieces):
    n_m, n_v = len(mxu_pieces), len(vpu_pieces)
    i = j = 0
    while i < n_m or j < n_v:
        if j >= n_v or (i < n_m and i * n_v <= j * n_m):
            mxu_pieces[i]()
            i += 1
        else:
            vpu_pieces[j]()
            j += 1


def _params(n_grid_axes):
    return pltpu.CompilerParams(
        dimension_semantics=("arbitrary",) * n_grid_axes,
        vmem_limit_bytes=VMEM_LIMIT,
    )


def _cast_plumbing(arrays, n_steps, step_of):
    in_specs, out_specs, shapes = [], [], []
    for stacked, layer in arrays:
        _, n_rows, n_cols = stacked.shape
        rows, rem = divmod(n_rows, n_steps)
        assert rem == 0 and rows % (2 * SUBLANES) == 0, stacked.shape
        in_specs.append(pl.BlockSpec((None, rows, n_cols),
                                     lambda *idx, _l=layer: (_l, step_of(*idx), 0)))
        out_specs.append(pl.BlockSpec((rows, n_cols), lambda *idx: (step_of(*idx), 0)))
        shapes.append(jax.ShapeDtypeStruct((n_rows, n_cols), BF16))
    return in_specs, out_specs, shapes


def _cast_rows(src_refs, dst_refs):
    for src, dst in zip(src_refs, dst_refs, strict=True):
        dst[...] = src[...].astype(BF16)


def _ffn_kernel(*refs, final, n_prompt, n_cast, t_len):
    n_in = 6 if final else 5
    xp_ref, xs_ref, g_ref, w1_ref, w2_ref = refs[:5]
    cast_in = refs[n_in:n_in + n_cast]
    op_ref, os_ref = refs[n_in + n_cast:n_in + n_cast + 2]
    cast_out = refs[n_in + n_cast + 2:n_in + 2 * n_cast + 2]
    hb_ref, act_ref, w1_vmem, w2_vmem, sem = refs[n_in + 2 * n_cast + 2:]
    step = pl.program_id(0)
    n_chunks = D_FF // FFN_COL_CHUNK

    def w1_copy(c):
        cols = slice(c * FFN_COL_CHUNK, (c + 1) * FFN_COL_CHUNK)
        return pltpu.make_async_copy(w1_ref.at[:, cols], w1_vmem.at[:, cols], sem.at[c])

    def w2_copy():
        return pltpu.make_async_copy(w2_ref, w2_vmem, sem.at[n_chunks])

    def rows_block(x_ref, wait_for_weights=False):
        hb_ref[...] = _rms(x_ref[...], g_ref[...]).astype(BF16)
        for c in range(n_chunks):
            cols = slice(c * FFN_COL_CHUNK, (c + 1) * FFN_COL_CHUNK)
            if wait_for_weights:
                w1_copy(c).wait()
            a = _dot(hb_ref[...], w1_vmem[:, cols])
            act_ref[:, cols] = jnp.square(jnp.maximum(a, 0.0)).astype(BF16)
        if wait_for_weights:
            w2_copy().wait()
        y = x_ref[...] + _dot(act_ref[...], w2_vmem[...])
        return _rms(y, refs[5][...]) if final else y

    @pl.when(step == 0)
    def _():
        for c in range(n_chunks):
            w1_copy(c).start()
        w2_copy().start()
        op_ref[...] = rows_block(xp_ref, wait_for_weights=True)
        _cast_rows(cast_in, cast_out)

    @pl.when(jnp.logical_and(step > 0, step < n_prompt))
    def _():
        op_ref[...] = rows_block(xp_ref)
        _cast_rows(cast_in, cast_out)

    @pl.when(step >= n_prompt)
    def _():
        y = rows_block(xs_ref)
        if final:
            nb = y.shape[0] // t_len
            for t in range(t_len):
                os_ref[:, t, :] = y[t * nb:(t + 1) * nb]
        else:
            os_ref[...] = y


def _ffn(xp2d, xs2d, g, w1, w2, g_final, tm, t_len, cast_next=()):
    n_prompt, n_sample = xp2d.shape[0] // tm, xs2d.shape[0] // tm
    final = g_final is not None
    last = n_prompt - 1
    p_spec = pl.BlockSpec((tm, D_MODEL), lambda i: (jnp.minimum(i, last), 0))
    s_spec = pl.BlockSpec((tm, D_MODEL), lambda i: (jnp.maximum(i - n_prompt, 0), 0))
    hbm_spec = pl.BlockSpec(memory_space=pl.ANY)
    in_specs = [p_spec, s_spec, _const_spec((1, D_MODEL)), hbm_spec, hbm_spec]
    args = [xp2d, xs2d, g.reshape(1, D_MODEL), w1, w2]
    s_out_spec, s_out_shape = s_spec, jax.ShapeDtypeStruct(xs2d.shape, F32)
    if final:
        in_specs.append(_const_spec((1, D_MODEL)))
        args.append(g_final.reshape(1, D_MODEL))
        nb = tm // t_len
        s_out_spec = pl.BlockSpec((nb, t_len, D_MODEL), lambda i: (jnp.maximum(i - n_prompt, 0), 0, 0))
        s_out_shape = jax.ShapeDtypeStruct((xs2d.shape[0] // t_len, t_len, D_MODEL), F32)
    c_in, c_out, c_shapes = _cast_plumbing(cast_next, n_prompt, lambda i: jnp.minimum(i, last))
    return pl.pallas_call(
        functools.partial(_ffn_kernel, final=final, n_prompt=n_prompt, n_cast=len(cast_next),
                          t_len=t_len),
        grid=(n_prompt + n_sample,),
        in_specs=in_specs + c_in,
        out_specs=[p_spec, s_out_spec] + c_out,
        out_shape=[jax.ShapeDtypeStruct(xp2d.shape, F32), s_out_shape] + c_shapes,
        scratch_shapes=[pltpu.VMEM((tm, D_MODEL), BF16), pltpu.VMEM((tm, D_FF), BF16),
                        pltpu.VMEM((D_MODEL, D_FF), BF16), pltpu.VMEM((D_FF, D_MODEL), BF16),
                        pltpu.SemaphoreType.DMA((D_FF // FFN_COL_CHUNK + 1,))],
        compiler_params=_params(1),
        name="ffn_final" if final else "ffn",
    )(*args, *(a for a, _ in cast_next))


def _causal_dwconv(ext_ref, gi, w_ref, n_taps, hist, r0, rows):
    lanes = slice(gi * GROUP_W, (gi + 1) * GROUP_W)
    acc = None
    for k in range(n_taps):
        lo = hist + r0 - (n_taps - 1) + k
        term = w_ref[k:k + 1, lanes] * ext_ref[gi, lo:lo + rows, :]
        acc = term if acc is None else acc + term
    return acc


def _even_prompt_kernel(*refs, tm, n_cast):
    x_ref, g_ref, win_ref, ca_ref, cb_ref, cbb_ref, lng_ref, lnb_ref, wout_ref = refs[:9]
    cast_in = refs[9:9 + n_cast]
    o_ref, sa_ref, sb_ref = refs[9 + n_cast:12 + n_cast]
    cast_out = refs[12 + n_cast:12 + 2 * n_cast]
    hb_ref, z_ref, ua_ext, ub_ext, cbv_ref, cat_ref = refs[12 + 2 * n_cast:]
    s = pl.program_id(1)
    _cast_rows(cast_in, cast_out)

    @pl.when(s == 0)
    def _():
        ua_ext[:, 0:HIST_A, :] = jnp.zeros((N_GROUPS, HIST_A, GROUP_W), F32)
        ub_ext[:, 0:HIST_B, :] = jnp.zeros((N_GROUPS, HIST_B, GROUP_W), F32)

    @pl.when(s > 0)
    def _():
        ua_ext[:, 0:HIST_A, :] = ua_ext[:, tm:tm + HIST_A, :]
        ub_ext[:, 0:HIST_B, :] = ub_ext[:, tm:tm + HIST_B, :]

    def zseg(rows, i, lanes=slice(0, W_MIX)):
        return z_ref[rows, i * W_MIX + lanes.start:i * W_MIX + lanes.stop]

    def project_pieces(rows):
        def norm():
            hb_ref[rows, :] = _rms(x_ref[rows, :], g_ref[...]).astype(BF16)

        def dot_piece(i):
            cols = slice(i * W_MIX, (i + 1) * W_MIX)
            z_ref[rows, cols] = _dot(hb_ref[rows, :], win_ref[:, cols])

        return [norm] + [functools.partial(dot_piece, i) for i in (3, 4, 1, 2, 0)]

    def mix_pieces(rows):
        def gates(r0):
            crow = slice(r0, r0 + ROW_CHUNK)
            for gi in range(N_GROUPS):
                lanes = slice(gi * GROUP_W, (gi + 1) * GROUP_W)
                ua_ext[gi, HIST_A + r0:HIST_A + r0 + ROW_CHUNK, :] = (
                    zseg(crow, 1, lanes) * zseg(crow, 2, lanes))
                ub_ext[gi, HIST_B + r0:HIST_B + r0 + ROW_CHUNK, :] = (
                    zseg(crow, 3, lanes) * jax.nn.sigmoid(zseg(crow, 4, lanes)))

        def convs(r0, gi):
            crow = slice(r0, r0 + CONV_ROWS)
            lanes = slice(gi * GROUP_W, (gi + 1) * GROUP_W)
            ya = zseg(crow, 0, lanes) * _causal_dwconv(ua_ext, gi, ca_ref, K_A, HIST_A, r0, CONV_ROWS)
            cat_ref[crow, lanes] = ya.astype(BF16)
            cbv_ref[crow, lanes] = _causal_dwconv(ub_ext, gi, cb_ref, K_B, HIST_B, r0, CONV_ROWS)

        def norm_act(r0):
            crow = slice(r0, r0 + CONV_ROWS)
            yb = _silu(_ln(cbv_ref[crow, :] + cbb_ref[...], lng_ref[...], lnb_ref[...]))
            cat_ref[crow, W_MIX:2 * W_MIX] = yb.astype(BF16)

        pieces = [functools.partial(gates, r0) for r0 in range(rows.start, rows.stop, ROW_CHUNK)]
        for r0 in range(rows.start, rows.stop, CONV_ROWS):
            pieces += [functools.partial(convs, r0, gi) for gi in range(N_GROUPS)]
            pieces.append(functools.partial(norm_act, r0))
        return pieces

    def out_pieces(rows):
        def piece(j):
            cols = slice(j * W_MIX, (j + 1) * W_MIX)
            o_ref[rows, cols] = x_ref[rows, cols] + _dot(cat_ref[rows, :], wout_ref[:, cols])

        return [functools.partial(piece, j) for j in range(D_MODEL // W_MIX)]

    blocks = [slice(r, r + MIX_BLOCK_ROWS) for r in range(0, tm, MIX_BLOCK_ROWS)]
    _emit_interleaved(project_pieces(blocks[0]), [])
    for i in range(len(blocks)):
        mxu = out_pieces(blocks[i - 1]) if i > 0 else []
        if i + 1 < len(blocks):
            mxu = project_pieces(blocks[i + 1]) + mxu
        _emit_interleaved(mxu, mix_pieces(blocks[i]))
    _emit_interleaved(out_pieces(blocks[-1]), [])

    @pl.when(s == pl.num_programs(1) - 1)
    def _():
        for gi in range(N_GROUPS):
            lanes = slice(gi * GROUP_W, (gi + 1) * GROUP_W)
            sa_ref[:, lanes] = ua_ext[gi, HIST_A + tm - (K_A - 1):HIST_A + tm, :]
            sb_ref[:, lanes] = ub_ext[gi, HIST_B + tm - (K_B - 1):HIST_B + tm, :]


def _even_prompt(x, g, win, ca, cb, cbb, lng, lnb, wout, tm, cast_next=()):
    b, s, _ = x.shape
    n_s = s // tm
    tile = pl.BlockSpec((None, tm, D_MODEL), lambda i, j: (i, j, 0))
    n_in = win.shape[1]
    c_in, c_out, c_shapes = _cast_plumbing(cast_next, b * n_s, lambda i, j: i * n_s + j)
    return pl.pallas_call(
        functools.partial(_even_prompt_kernel, tm=tm, n_cast=len(cast_next)),
        grid=(b, n_s),
        in_specs=[tile, _const_spec((1, D_MODEL)), _const_spec((D_MODEL, n_in)),
                  _const_spec((K_A, W_MIX)), _const_spec((K_B, W_MIX)), _const_spec((1, W_MIX)),
                  _const_spec((1, W_MIX)), _const_spec((1, W_MIX)),
                  _const_spec((2 * W_MIX, D_MODEL))] + c_in,
        out_specs=[tile,
                   pl.BlockSpec((None, K_A - 1, W_MIX), lambda i, j: (i, 0, 0)),
                   pl.BlockSpec((None, K_B - 1, W_MIX), lambda i, j: (i, 0, 0))] + c_out,
        out_shape=[jax.ShapeDtypeStruct((b, s, D_MODEL), F32),
                   jax.ShapeDtypeStruct((b, K_A - 1, W_MIX), F32),
                   jax.ShapeDtypeStruct((b, K_B - 1, W_MIX), F32)] + c_shapes,
        scratch_shapes=[pltpu.VMEM((tm, D_MODEL), BF16),
                        pltpu.VMEM((tm, n_in), F32),
                        pltpu.VMEM((N_GROUPS, tm + HIST_A, GROUP_W), F32),
                        pltpu.VMEM((N_GROUPS, tm + HIST_B, GROUP_W), F32),
                        pltpu.VMEM((tm, W_MIX), F32),
                        pltpu.VMEM((tm, 2 * W_MIX), BF16)],
        compiler_params=_params(2),
        name="even_prompt",
    )(x, g.reshape(1, D_MODEL), win, ca, cb, cbb.reshape(1, W_MIX), lng.reshape(1, W_MIX),
      lnb.reshape(1, W_MIX), wout, *(a for a, _ in cast_next))


def _tril_bf16(w):
    t = lax.broadcasted_iota(jnp.int32, (CHUNK, CHUNK), 0)
    s = lax.broadcasted_iota(jnp.int32, (CHUNK, CHUNK), 1)
    return jnp.where(s <= t, w, 0.0).astype(BF16)


def _odd_prompt_kernel(*refs, tm, n_cast):
    (x_ref, g_ref, win_ref, wpool_ref, pscale_ref, sguw_ref, sgubt_ref, lng_ref, lnb_ref,
     wout_ref) = refs[:10]
    cast_in = refs[10:10 + n_cast]
    o_ref, sp_ref = refs[10 + n_cast:12 + n_cast]
    cast_out = refs[12 + n_cast:12 + 2 * n_cast]
    hb_ref, p_ext, ps_ref, inv_ref, u_ref, vn_ref, d_ref, cat_ref, wm_ref = refs[12 + 2 * n_cast:]
    s = pl.program_id(1)
    _cast_rows(cast_in, cast_out)

    @pl.when(jnp.logical_and(pl.program_id(0) == 0, s == 0))
    def _():
        for gi in range(N_GROUPS):
            wm_ref[gi] = _tril_bf16(sguw_ref[gi])

    @pl.when(s == 0)
    def _():
        p_ext[:, 0:HIST_P, :] = jnp.zeros((N_GROUPS, HIST_P, GROUP_W), F32)

    @pl.when(s > 0)
    def _():
        p_ext[:, 0:HIST_P, :] = p_ext[:, tm:tm + HIST_P, :]

    assert ROW_CHUNK >= max(POOL_WINDOWS)
    pos0 = s * tm + lax.broadcasted_iota(jnp.int32, (ROW_CHUNK, GROUP_W), 0)
    for gi, w in enumerate(POOL_WINDOWS):
        inv_ref[gi] = 1.0 / jnp.minimum(pos0 + 1, w).astype(F32)

    def project_pieces(rows):
        def seg(i):
            return _dot(hb_ref[rows, :], win_ref[:, i * W_MIX:(i + 1) * W_MIX])

        def norm():
            hb_ref[rows, :] = _rms(x_ref[rows, :], g_ref[...]).astype(BF16)

        def pooled():
            pc = seg(0)
            for gi in range(N_GROUPS):
                p_ext[gi, HIST_P + rows.start:HIST_P + rows.stop, :] = pc[:, gi * GROUP_W:(gi + 1) * GROUP_W]

        def gate():
            u_ref[rows, :] = seg(1)

        def normed():
            vn_ref[rows, :] = _ln(seg(2), lng_ref[...], lnb_ref[...]).astype(BF16)

        return [norm, pooled, gate, normed]

    def pool_chunk(r0):
        n = ROW_CHUNK + HIST_P
        for gi, w in enumerate(POOL_WINDOWS):
            win = p_ext.at[gi, r0:r0 + n, :]
            scratch = ps_ref.at[r0 // ROW_CHUNK % 2, gi]
            prev, lo, step = win, 0, 1
            sums = win[0:n, :]
            while step < w:
                lo += SUBLANES
                sums = sums[SUBLANES:] + prev[lo - step:n - step, :]
                step *= 2
                if step < w:
                    scratch[lo:n, :] = sums
                    prev = scratch
            tok = win[HIST_P:n, :]
            inv = inv_ref[gi] if r0 == 0 else 1.0 / w
            lanes = slice(gi * GROUP_W, (gi + 1) * GROUP_W)
            d_ref[r0:r0 + ROW_CHUNK, lanes] = (sums[HIST_P - lo:] * inv - tok).astype(BF16)

    def pool_project(rows):
        for gi in range(N_GROUPS):
            lanes = slice(gi * GROUP_W, (gi + 1) * GROUP_W)
            yc = _dot(d_ref[rows, lanes], wpool_ref[gi]) * pscale_ref[:, lanes]
            cat_ref[rows, lanes] = yc.astype(BF16)

    def sgu_chunk(r0):
        rows = slice(r0, r0 + CHUNK)
        for gi in range(N_GROUPS):
            lanes = slice(gi * GROUP_W, (gi + 1) * GROUP_W)
            mixed = _dot(wm_ref[gi], vn_ref[rows, lanes]) + sgubt_ref[:, gi:gi + 1]
            yd = u_ref[rows, lanes] * mixed
            cat_ref[rows, W_MIX + gi * GROUP_W:W_MIX + (gi + 1) * GROUP_W] = yd.astype(BF16)

    def mix_pieces(rows):
        pieces = [functools.partial(pool_chunk, r0) for r0 in range(rows.start, rows.stop, ROW_CHUNK)]
        pieces.append(functools.partial(pool_project, rows))
        pieces += [functools.partial(sgu_chunk, r0) for r0 in range(rows.start, rows.stop, CHUNK)]
        return pieces

    def out_pieces(rows):
        def piece():
            o_ref[rows, :] = x_ref[rows, :] + _dot(cat_ref[rows, :], wout_ref[...])

        return [piece]

    blocks = [slice(r, r + MIX_BLOCK_ROWS) for r in range(0, tm, MIX_BLOCK_ROWS)]
    _emit_interleaved(project_pieces(blocks[0]), [])
    for i in range(len(blocks)):
        mxu = out_pieces(blocks[i - 1]) if i > 0 else []
        if i + 1 < len(blocks):
            mxu = project_pieces(blocks[i + 1]) + mxu
        _emit_interleaved(mxu, mix_pieces(blocks[i]))
    _emit_interleaved(out_pieces(blocks[-1]), [])

    @pl.when(s == pl.num_programs(1) - 1)
    def _():
        for gi in range(N_GROUPS):
            sp_ref[:, gi * GROUP_W:(gi + 1) * GROUP_W] = p_ext[gi, HIST_P + tm - POOL_HIST:HIST_P + tm, :]


def _odd_prompt(x, g, win, wpool, pscale, sguw, sgub, lng, lnb, wout, tm, cast_next=()):
    b, s, _ = x.shape
    n_s = s // tm
    tile = pl.BlockSpec((None, tm, D_MODEL), lambda i, j: (i, j, 0))
    n_in = win.shape[1]
    c_in, c_out, c_shapes = _cast_plumbing(cast_next, b * n_s, lambda i, j: i * n_s + j)
    return pl.pallas_call(
        functools.partial(_odd_prompt_kernel, tm=tm, n_cast=len(cast_next)),
        grid=(b, n_s),
        in_specs=[tile, _const_spec((1, D_MODEL)), _const_spec((D_MODEL, n_in)),
                  _const_spec((N_GROUPS, GROUP_W, GROUP_W)), _const_spec((1, W_MIX)),
                  _const_spec((N_GROUPS, CHUNK, CHUNK)), _const_spec((CHUNK, N_GROUPS)),
                  _const_spec((1, W_MIX)), _const_spec((1, W_MIX)),
                  _const_spec((2 * W_MIX, D_MODEL))] + c_in,
        out_specs=[tile, pl.BlockSpec((None, POOL_HIST, W_MIX), lambda i, j: (i, 0, 0))] + c_out,
        out_shape=[jax.ShapeDtypeStruct((b, s, D_MODEL), F32),
                   jax.ShapeDtypeStruct((b, POOL_HIST, W_MIX), F32)] + c_shapes,
        scratch_shapes=[pltpu.VMEM((tm, D_MODEL), BF16),
                        pltpu.VMEM((N_GROUPS, tm + HIST_P, GROUP_W), F32),
                        pltpu.VMEM((2, N_GROUPS, ROW_CHUNK + HIST_P, GROUP_W), F32),
                        pltpu.VMEM((len(POOL_WINDOWS), ROW_CHUNK, GROUP_W), F32),
                        pltpu.VMEM((tm, W_MIX), F32),
                        pltpu.VMEM((tm, W_MIX), BF16),
                        pltpu.VMEM((tm, W_MIX), BF16),
                        pltpu.VMEM((tm, 2 * W_MIX), BF16),
                        pltpu.VMEM((N_GROUPS, CHUNK, CHUNK), BF16)],
        compiler_params=_params(2),
        name="odd_prompt",
    )(x, g.reshape(1, D_MODEL), win, wpool, pscale.reshape(1, W_MIX), sguw, sgub.T,
      lng.reshape(1, W_MIX), lnb.reshape(1, W_MIX), wout, *(a for a, _ in cast_next))


def _mixer_weight_copies(win_ref, wout_ref, win_vmem, wout_vmem, sem):
    n_seg = win_ref.shape[1] // W_MIX
    copies = [pltpu.make_async_copy(win_ref.at[:, i * W_MIX:(i + 1) * W_MIX],
                                    win_vmem.at[:, i * W_MIX:(i + 1) * W_MIX], sem.at[i])
              for i in range(n_seg)]
    return copies + [pltpu.make_async_copy(wout_ref, wout_vmem, sem.at[n_seg])]


def _even_sample_kernel(*refs, **static):
    copies = _mixer_weight_copies(refs[2], refs[8], *refs[-3:])

    @pl.when(pl.program_id(0) == 0)
    def _():
        for cp in copies:
            cp.start()
        _even_sample_body(copies, *refs, **static)

    @pl.when(pl.program_id(0) > 0)
    def _():
        _even_sample_body(None, *refs, **static)


def _even_sample_body(pending, x_ref, g_ref, win_ref, ca_ref, cb_ref, cbb_ref, lng_ref, lnb_ref,
                      wout_ref, ha_ref, hbst_ref,
                      o_ref, sa_ref, sb_ref,
                      hb_ref, gate_ref, ua_ext, ub_ext, cbv_ref, cat_ref, win_vmem, wout_vmem, sem,
                      *, t_len, n_seq, x_seq_major):
    if x_seq_major:
        x = jnp.concatenate([x_ref[:, t, :] for t in range(t_len)], axis=0)
    else:
        x = x_ref[...].reshape(t_len * n_seq, D_MODEL)
    hb_ref[...] = _rms(x, g_ref[...]).astype(BF16)

    def seg(i):
        if pending is not None:
            pending[i].wait()
        return _dot(hb_ref[...], win_vmem[:, i * W_MIX:(i + 1) * W_MIX])

    gate_ref[...] = seg(0)
    for j in range(K_A - 1):
        ua_ext[j] = ha_ref[:, j, :]
    ub_ext[0:K_B - 1] = hbst_ref[...]
    ua_ext[K_A - 1:K_A - 1 + t_len] = (seg(1) * seg(2)).reshape(t_len, n_seq, W_MIX)
    ub_ext[K_B - 1:K_B - 1 + t_len] = (seg(3) * jax.nn.sigmoid(seg(4))).reshape(t_len, n_seq, W_MIX)

    def taps(ext_ref, w_ref, n_taps, t0, lanes):
        acc = None
        for k in range(n_taps):
            term = w_ref[k:k + 1, lanes] * ext_ref[t0 + k:t0 + k + SAMPLE_CONV_STEPS, :, lanes]
            acc = term if acc is None else acc + term
        return acc.reshape(SAMPLE_CONV_STEPS * n_seq, GROUP_W)

    for t0 in range(0, t_len, SAMPLE_CONV_STEPS):
        rows = slice(t0 * n_seq, (t0 + SAMPLE_CONV_STEPS) * n_seq)
        for gi in range(N_GROUPS):
            lanes = slice(gi * GROUP_W, (gi + 1) * GROUP_W)
            cat_ref[rows, lanes] = (gate_ref[rows, lanes] * taps(ua_ext, ca_ref, K_A, t0, lanes)).astype(BF16)
            cbv_ref[rows, lanes] = taps(ub_ext, cb_ref, K_B, t0, lanes) + cbb_ref[:, lanes]
        yb = _silu(_ln(cbv_ref[rows, :], lng_ref[...], lnb_ref[...]))
        cat_ref[rows, W_MIX:2 * W_MIX] = yb.astype(BF16)

    if pending is not None:
        pending[-1].wait()
    o_ref[...] = (x + _dot(cat_ref[...], wout_vmem[...])).reshape(t_len, n_seq, D_MODEL)
    for j in range(K_A - 1):
        sa_ref[:, j, :] = ua_ext[t_len + j]
    sb_ref[...] = ub_ext[t_len:t_len + K_B - 1]


def _seq_block_spec(rows, nb, width):
    return pl.BlockSpec((rows, nb, width), lambda i, *_: (0, i, 0))


def _seq_major_spec(rows, nb, width):
    return pl.BlockSpec((nb, rows, width), lambda i, *_: (i, 0, 0))


def _xs_block_spec(t_len, nb):
    return pl.BlockSpec((None, t_len, nb, D_MODEL), lambda i, *_: (i, 0, 0, 0))


def _even_sample(x, g, win, ca, cb, cbb, lng, lnb, wout, hist_a, hist_b, nb, x_seq_major):
    n_seq, t_len = hist_a.shape[0], x.size // (hist_a.shape[0] * D_MODEL)
    m = t_len * nb
    n_in = win.shape[1]
    x_spec = _seq_major_spec(t_len, nb, D_MODEL) if x_seq_major else _xs_block_spec(t_len, nb)
    return pl.pallas_call(
        functools.partial(_even_sample_kernel, t_len=t_len, n_seq=nb, x_seq_major=x_seq_major),
        grid=(n_seq // nb,),
        in_specs=[x_spec, _const_spec((1, D_MODEL)),
                  pl.BlockSpec(memory_space=pl.ANY),
                  _const_spec((K_A, W_MIX)), _const_spec((K_B, W_MIX)), _const_spec((1, W_MIX)),
                  _const_spec((1, W_MIX)), _const_spec((1, W_MIX)), pl.BlockSpec(memory_space=pl.ANY),
                  _seq_major_spec(K_A - 1, nb, W_MIX), _seq_block_spec(K_B - 1, nb, W_MIX)],
        out_specs=[_xs_block_spec(t_len, nb), _seq_major_spec(K_A - 1, nb, W_MIX),
                   _seq_block_spec(K_B - 1, nb, W_MIX)],
        out_shape=[jax.ShapeDtypeStruct((n_seq // nb, t_len, nb, D_MODEL), F32),
                   jax.ShapeDtypeStruct((n_seq, K_A - 1, W_MIX), F32),
                   jax.ShapeDtypeStruct((K_B - 1, n_seq, W_MIX), F32)],
        scratch_shapes=[pltpu.VMEM((m, D_MODEL), BF16),
                        pltpu.VMEM((m, W_MIX), F32),
                        pltpu.VMEM((t_len + K_A - 1, nb, W_MIX), F32),
                        pltpu.VMEM((t_len + K_B - 1, nb, W_MIX), F32),
                        pltpu.VMEM((m, W_MIX), F32),
                        pltpu.VMEM((m, 2 * W_MIX), BF16),
                        pltpu.VMEM((D_MODEL, n_in), BF16),
                        pltpu.VMEM((2 * W_MIX, D_MODEL), BF16),
                        pltpu.SemaphoreType.DMA((n_in // W_MIX + 1,))],
        compiler_params=_params(1),
        name="even_sample",
    )(x, g.reshape(1, D_MODEL), win, ca, cb, cbb.reshape(1, W_MIX), lng.reshape(1, W_MIX),
      lnb.reshape(1, W_MIX), wout, hist_a, hist_b)


def _odd_sample_kernel(sw_ref, sb_ref_smem,
                       x_ref, g_ref, win_ref, wpool_ref, pscale_ref, lng_ref, lnb_ref, wout_ref,
                       hp_ref,
                       o_ref, sp_ref, vn_out_ref,
                       hb_ref, p_ext, u_ref, vn_ref, d_ref, cat_ref, *, t_len, n_seq, start_pos):
    x = x_ref[...].reshape(t_len * n_seq, D_MODEL)
    hb_ref[...] = _rms(x, g_ref[...]).astype(BF16)

    def seg(i):
        return _dot(hb_ref[...], win_ref[:, i * W_MIX:(i + 1) * W_MIX])

    p_ext[0:POOL_HIST] = hp_ref[...]
    p_ext[POOL_HIST:POOL_HIST + t_len] = seg(0).reshape(t_len, n_seq, W_MIX)
    u_ref[...] = seg(1).reshape(t_len, n_seq, W_MIX)
    vn_ref[...] = _ln(seg(2), lng_ref[...], lnb_ref[...]).reshape(t_len, n_seq, W_MIX)

    for t in range(t_len):
        rows = slice(t * n_seq, (t + 1) * n_seq)
        for gi, w in enumerate(POOL_WINDOWS):
            lanes = slice(gi * GROUP_W, (gi + 1) * GROUP_W)
            tok = p_ext[POOL_HIST + t, :, lanes]
            wsum = tok
            for j in range(1, w):
                wsum = wsum + p_ext[POOL_HIST + t - j, :, lanes]
            cnt = float(min(start_pos + t + 1, w))
            d_ref[rows, lanes] = (wsum / cnt - tok).astype(BF16)
            mixed = jnp.zeros((n_seq, GROUP_W), F32) + sb_ref_smem[gi * t_len + t]
            for s in range(t + 1):
                mixed = mixed + sw_ref[(gi * t_len + t) * t_len + s] * vn_ref[s, :, lanes]
            cat_ref[rows, W_MIX + gi * GROUP_W:W_MIX + (gi + 1) * GROUP_W] = (
                u_ref[t, :, lanes] * mixed).astype(BF16)

    for gi in range(N_GROUPS):
        lanes = slice(gi * GROUP_W, (gi + 1) * GROUP_W)
        yc = _dot(d_ref[:, lanes], wpool_ref[gi]) * pscale_ref[:, lanes]
        cat_ref[:, lanes] = yc.astype(BF16)

    o_ref[...] = (x + _dot(cat_ref[...], wout_ref[...])).reshape(t_len, n_seq, D_MODEL)
    sp_ref[...] = p_ext[t_len:t_len + POOL_HIST]
    for t in range(t_len):
        vn_out_ref[:, t, :] = vn_ref[t]


def _odd_sample(x, g, win, wpool, pscale, sguw, sgub, lng, lnb, wout, hist_p, nb, start_pos):
    n_blocks, t_len, _, _ = x.shape
    n_seq = n_blocks * nb
    m = t_len * nb
    n_in = win.shape[1]
    assert t_len <= CHUNK
    sw = sguw[:, :t_len, :t_len].reshape(-1)
    sb = sgub[:, :t_len].reshape(-1)

    grid_spec = pltpu.PrefetchScalarGridSpec(
        num_scalar_prefetch=2,
        grid=(n_seq // nb,),
        in_specs=[_xs_block_spec(t_len, nb), _const_spec((1, D_MODEL)),
                  _const_spec((D_MODEL, n_in)),
                  _const_spec((N_GROUPS, GROUP_W, GROUP_W)), _const_spec((1, W_MIX)),
                  _const_spec((1, W_MIX)), _const_spec((1, W_MIX)),
                  _const_spec((2 * W_MIX, D_MODEL)),
                  _seq_block_spec(POOL_HIST, nb, W_MIX)],
        out_specs=[_xs_block_spec(t_len, nb), _seq_block_spec(POOL_HIST, nb, W_MIX),
                   _seq_major_spec(t_len, nb, W_MIX)],
        scratch_shapes=[pltpu.VMEM((m, D_MODEL), BF16),
                        pltpu.VMEM((t_len + POOL_HIST, nb, W_MIX), F32),
                        pltpu.VMEM((t_len, nb, W_MIX), F32),
                        pltpu.VMEM((t_len, nb, W_MIX), F32),
                        pltpu.VMEM((m, W_MIX), BF16),
                        pltpu.VMEM((m, 2 * W_MIX), BF16)],
    )
    return pl.pallas_call(
        functools.partial(_odd_sample_kernel, t_len=t_len, n_seq=nb, start_pos=start_pos),
        grid_spec=grid_spec,
        out_shape=[jax.ShapeDtypeStruct((n_blocks, t_len, nb, D_MODEL), F32),
                   jax.ShapeDtypeStruct((POOL_HIST, n_seq, W_MIX), F32),
                   jax.ShapeDtypeStruct((n_seq, t_len, W_MIX), F32)],
        compiler_params=_params(1),
        name="odd_sample",
    )(sw, sb, x, g.reshape(1, D_MODEL), win, wpool, pscale.reshape(1, W_MIX),
      lng.reshape(1, W_MIX), lnb.reshape(1, W_MIX), wout, hist_p)


def _time_major(a):
    return jnp.swapaxes(a, 0, 1)


def kernel(x_prompt, x_sample, state_conv_a, state_conv_b, state_pool, norm_mix, norm_ffn, ev_w_in,
           ev_conv_a, ev_conv_b, ev_conv_b_bias, ev_ln_g, ev_ln_b, ev_w_out, od_w_in, od_pool_w,
           od_pool_scale, od_sgu_w, od_sgu_b, od_sgu_ln_g, od_sgu_ln_b, od_w_out, ffn_w1, ffn_w2,
           norm_final):
    depth = norm_mix.shape[0]
    batch, seq, _ = x_prompt.shape
    n_seq, t_len, _ = x_sample.shape
    tm_mix = 1024
    tm_ffn = 512
    nb = tm_ffn // t_len

    def mixer_weights(l):
        i = l // 2
        return ((ev_w_in, i), (ev_w_out, i)) if l % 2 == 0 else ((od_w_in, i), (od_w_out, i))

    xp = x_prompt
    xs = x_sample
    sa_p, sa_s, sb_p, sb_s, sc_p, sc_s, vn_s = [], [], [], [], [], [], []
    win, wout = (w[i].astype(BF16) for w, i in mixer_weights(0))
    w1 = w2 = None
    for l in range(depth):
        i = l // 2
        ffn_f32 = ((ffn_w1, l), (ffn_w2, l)) if l == 0 else ()
        if l % 2 == 0:
            common = (norm_mix[l], win, ev_conv_a[i], ev_conv_b[i], ev_conv_b_bias[i], ev_ln_g[i],
                      ev_ln_b[i], wout)
            xp, a_p, b_p, *own_ffn = _even_prompt(xp, *common, tm=tm_mix, cast_next=ffn_f32)
            xs, a_s, b_s = _even_sample(xs, *common, state_conv_a[i], _time_major(state_conv_b[i]),
                                        nb, x_seq_major=(l == 0))
            sa_p.append(a_p)
            sb_p.append(b_p)
            sa_s.append(a_s)
            sb_s.append(_time_major(b_s))
        else:
            common = (norm_mix[l], win, od_pool_w[i].astype(BF16), od_pool_scale[i], od_sgu_w[i],
                      od_sgu_b[i], od_sgu_ln_g[i], od_sgu_ln_b[i], wout)
            xp, c_p, *own_ffn = _odd_prompt(xp, *common, tm=tm_mix, cast_next=ffn_f32)
            xs, c_s, v_s = _odd_sample(xs, *common, _time_major(state_pool[i]), nb, PAST_LEN)
            sc_p.append(c_p)
            sc_s.append(_time_major(c_s))
            vn_s.append(v_s)
        if own_ffn:
            w1, w2 = own_ffn
        last = l == depth - 1
        next_f32 = () if last else mixer_weights(l + 1) + ((ffn_w1, l + 1), (ffn_w2, l + 1))
        xp, xs, *next_w = _ffn(xp.reshape(batch * seq, D_MODEL), xs.reshape(t_len * n_seq, D_MODEL),
                               norm_ffn[l], w1, w2, norm_final if last else None, tm_ffn, t_len,
                               cast_next=next_f32)
        xp = xp.reshape(batch, seq, D_MODEL)
        if not last:
            xs = xs.reshape(n_seq // nb, t_len, nb, D_MODEL)
            win, wout, w1, w2 = next_w

    return (xp, xs, jnp.stack(sa_p), jnp.stack(sa_s), jnp.stack(sb_p), jnp.stack(sb_s),
            jnp.stack(sc_p), jnp.stack(sc_s), jnp.stack(vn_s))
```
